```python
import math
import jax, jax.numpy as jnp
from jax import lax
import numpy as np

D_MODEL = 4096
BATCH = 8
SEQ = 2048
DEPTH = 2

HEAD_DIM = 128
ATTN_HEADS = 16
KV_GROUPS = 2
HEADS_PER_GROUP = ATTN_HEADS // KV_GROUPS
ATTN_WIDTH = ATTN_HEADS * HEAD_DIM
KV_WIDTH = KV_GROUPS * HEAD_DIM
ROPE_DIM = HEAD_DIM // 4
ROPE_THETA = 500000.0
CMP_BLOCK = 32
CMP_STRIDE = 16
SLC_BLOCK = 32
SLC_TOPN = 8
WINDOW = 512
Q_BLOCK = 128
SLC_Q_CHUNK = 64

HG_HEADS = 8
HG_DK = 128
HG_DV = 128
HG_KWIDTH = HG_HEADS * HG_DK
HG_WIDTH = HG_HEADS * HG_DV
HG_CHUNK = 16

POOL_WINDOWS = (2, 4, 8, 16)
POOL_GROUP = 256
POOL_WIDTH = POOL_GROUP * len(POOL_WINDOWS)

N_BRANCH = 3
SPLIT_SIZES = (ATTN_WIDTH, KV_WIDTH, KV_WIDTH, KV_WIDTH, KV_WIDTH, KV_WIDTH, KV_WIDTH, 3 * ATTN_HEADS,
               HG_KWIDTH, HG_KWIDTH, HG_WIDTH, HG_WIDTH, POOL_WIDTH, D_MODEL, D_MODEL, D_MODEL)
N_IN = sum(SPLIT_SIZES)

N_EXPERTS = 64
TOP_K = 8
D_EXPERT = 128
D_SHARED = 256
ROUTE_SCALE = 2.5

DN_ALPHA = (2.0 * DEPTH) ** 0.25
DN_BETA = (8.0 * DEPTH) ** -0.25
LN_EPS = 1e-5
RMS_EPS = 1e-6

kernel_name = "hybrid_nsa_hgrn2_pool_moe_deepnorm"

F32 = jnp.float32


def layer_norm(x, g, b):
    xf = x.astype(F32)
    mu = jnp.mean(xf, -1, keepdims=True)
    var = jnp.mean(jnp.square(xf - mu), -1, keepdims=True)
    return ((xf - mu) * lax.rsqrt(var + LN_EPS) * g.astype(F32) + b.astype(F32)).astype(x.dtype)


def partial_rope(x, positions):
    half = ROPE_DIM // 2
    inv = ROPE_THETA ** (-jnp.arange(half, dtype=F32) * 2.0 / ROPE_DIM)
    ang = positions.astype(F32)[..., None] * inv
    cos = jnp.cos(ang)[:, :, None, :]
    sin = jnp.sin(ang)[:, :, None, :]
    xf = x.astype(F32)
    x1 = xf[..., :half]
    x2 = xf[..., half:ROPE_DIM]
    out = jnp.concatenate([x1 * cos - x2 * sin, x2 * cos + x1 * sin, xf[..., ROPE_DIM:]], -1)
    return out.astype(x.dtype)


def compress_tokens(t, pos_emb, w1, w2):
    B, S, G, d = t.shape
    n_cmp = (S - CMP_BLOCK) // CMP_STRIDE + 1
    idx = jnp.arange(n_cmp)[:, None] * CMP_STRIDE + jnp.arange(CMP_BLOCK)[None, :]
    blk = t[:, idx] + pos_emb[:, None, :]
    flat = jnp.moveaxis(blk, 3, 2).reshape(B, n_cmp, G, CMP_BLOCK * d)
    return jax.nn.gelu(flat @ w1) @ w2


def nsa_attention(q, kc, vc, ks, vs, kw, vw, gate, positions, cpk, cpv, cw1k, cw2k, cw1v, cw2v):
    B, S = q.shape[:2]
    G, HPG, d = KV_GROUPS, HEADS_PER_GROUP, HEAD_DIM
    scale = d ** -0.5
    t_idx = jnp.arange(S)

    kcmp = compress_tokens(kc, cpk, cw1k, cw2k)
    vcmp = compress_tokens(vc, cpv, cw1v, cw2v)
    n_cmp = kcmp.shape[1]
    qg = q.reshape(B, S, G, HPG, d)
    s_c = jnp.einsum('bsghd,bcgd->bghsc', qg, kcmp, preferred_element_type=F32) * scale
    cmp_end = jnp.arange(n_cmp) * CMP_STRIDE + CMP_BLOCK - 1
    vis_c = cmp_end[None, :] <= t_idx[:, None]
    p_c = jax.nn.softmax(jnp.where(vis_c, s_c, -1e30), -1) * vis_c
    o_c = jnp.einsum('bghsc,bcgd->bsghd', p_c.astype(q.dtype), vcmp)

    n_slc = S // SLC_BLOCK
    cs = jnp.arange(n_cmp) * CMP_STRIDE
    ss = jnp.arange(n_slc) * SLC_BLOCK
    overlap = jnp.clip(jnp.minimum(cs[:, None] + CMP_BLOCK, ss[None, :] + SLC_BLOCK)
                       - jnp.maximum(cs[:, None], ss[None, :]), 0) / CMP_STRIDE
    imp = jnp.einsum('bghsc,cn->bgsn', p_c, overlap.astype(F32))
    blk = jnp.arange(n_slc)
    blk_t = t_idx // SLC_BLOCK
    causal_blk = blk[None, :] <= blk_t[:, None]
    forced = (blk[None, :] == 0) | (blk[None, :] == blk_t[:, None]) | (blk[None, :] == blk_t[:, None] - 1)
    imp = jnp.where(forced, jnp.inf, jnp.where(causal_blk, imp, -jnp.inf))
    k_top = min(SLC_TOPN, n_slc)
    _, sel = lax.top_k(imp, k_top)
    sel_ok = jnp.take_along_axis(jnp.broadcast_to(causal_blk, (B, G, S, n_slc)), sel, -1)

    qs = partial_rope(q, positions)
    ks_r = partial_rope(ks, positions)
    kw_r = partial_rope(kw, positions)

    ks_blk = jnp.moveaxis(ks_r.reshape(B, n_slc, SLC_BLOCK, G, d), 3, 1)
    vs_blk = jnp.moveaxis(vs.reshape(B, n_slc, SLC_BLOCK, G, d), 3, 1)
    n_ch = S // SLC_Q_CHUNK
    q_ch = jnp.moveaxis(qs.reshape(B, n_ch, SLC_Q_CHUNK, G, HPG, d), 1, 0)
    sel_ch = jnp.moveaxis(sel.reshape(B, G, n_ch, SLC_Q_CHUNK, k_top), 2, 0)
    ok_ch = jnp.moveaxis(sel_ok.reshape(B, G, n_ch, SLC_Q_CHUNK, k_top), 2, 0)
    t_ch = t_idx.reshape(n_ch, SLC_Q_CHUNK)
    bi = jnp.arange(B)[:, None, None, None]
    gi = jnp.arange(G)[None, :, None, None]
    key_off = jnp.arange(SLC_BLOCK)

    def slc_chunk(args):
        qc, sc, okc, tc = args
        kg = ks_blk[bi, gi, sc]
        vg = vs_blk[bi, gi, sc]
        s = jnp.einsum('bqghd,bgqkld->bghqkl', qc, kg, preferred_element_type=F32) * scale
        kpos = sc[..., None] * SLC_BLOCK + key_off
        m = okc[..., None] & (kpos <= tc[None, None, :, None, None])
        s = jnp.where(m[:, :, None], s, -1e30)
        p = jax.nn.softmax(s.reshape(*s.shape[:4], -1), -1).reshape(s.shape)
        return jnp.einsum('bghqkl,bgqkld->bqghd', p.astype(qc.dtype), vg)

    o_s = lax.map(slc_chunk, (q_ch, sel_ch, ok_ch, t_ch))
    o_s = jnp.moveaxis(o_s, 0, 1).reshape(B, S, G, HPG, d)

    n_qb = S // Q_BLOCK
    span = WINDOW + Q_BLOCK
    kw_pad = jnp.pad(kw_r, ((0, 0), (WINDOW, 0), (0, 0), (0, 0)))
    vw_pad = jnp.pad(vw, ((0, 0), (WINDOW, 0), (0, 0), (0, 0)))
    qs_g = qs.reshape(B, S, G, HPG, d)

    def win_block(i):
        start = i * Q_BLOCK
        qb = lax.dynamic_slice_in_dim(qs_g, start, Q_BLOCK, axis=1)
        kb = lax.dynamic_slice_in_dim(kw_pad, start, span, axis=1)
        vb = lax.dynamic_slice_in_dim(vw_pad, start, span, axis=1)
        tq = start + jnp.arange(Q_BLOCK)
        tk = start - WINDOW + jnp.arange(span)
        m = (tk[None, :] >= 0) & (tk[None, :] <= tq[:, None]) & (tq[:, None] - tk[None, :] < WINDOW)
        s = jnp.einsum('bqghd,bkgd->bghqk', qb, kb, preferred_element_type=F32) * scale
        p = jax.nn.softmax(jnp.where(m, s, -1e30), -1)
        return jnp.einsum('bghqk,bkgd->bqghd', p.astype(qb.dtype), vb)

    o_w = lax.map(win_block, jnp.arange(n_qb))
    o_w = jnp.moveaxis(o_w, 0, 1).reshape(B, S, G, HPG, d)

    g = jax.nn.sigmoid(gate.astype(F32)).reshape(B, S, 3, G, HPG)[..., None].astype(q.dtype)
    o = g[:, :, 0] * o_c + g[:, :, 1] * o_s + g[:, :, 2] * o_w
    return o.reshape(B, S, ATTN_WIDTH)


def hgrn2(hq, hf, hi, hg, lb, norm_g):
    B, S, _ = hq.shape
    H, K, V, C = HG_HEADS, HG_DK, HG_DV, HG_CHUNK
    f = lb + (1.0 - lb) * jax.nn.sigmoid(hf.astype(F32))
    log_f = jnp.log(f)
    k = 1.0 - f
    q = hq.astype(F32)
    v = hi.astype(F32)
    n_c = S // C

    def to_chunks(t, w):
        return jnp.moveaxis(t.reshape(B, n_c, C, H, w), 1, 0).transpose(0, 1, 3, 2, 4)

    causal = jnp.tril(jnp.ones((C, C), bool))

    def step(state, inp):
        qc, kc, vc, lfc = inp
        b = jnp.cumsum(lfc, axis=2)
        diff = b[:, :, :, None, :] - b[:, :, None, :, :]
        dec = jnp.exp(jnp.where(causal[:, :, None], diff, -jnp.inf))
        a = jnp.einsum('bhtk,bhsk,bhtsk->bhts', qc, kc, dec)
        o = jnp.einsum('bhts,bhsv->bhtv', a, vc) + jnp.einsum('bhtk,bhkv->bhtv', qc * jnp.exp(b), state)
        b_last = b[:, :, -1:, :]
        new_state = jnp.exp(b_last[:, :, 0])[..., None] * state + jnp.einsum('bhsk,bhsv->bhkv', kc * jnp.exp(b_last - b), vc)
        return new_state, o

    s0 = jnp.zeros((B, H, K, V), F32)
    _, o = lax.scan(step, s0, (to_chunks(q, K), to_chunks(k, K), to_chunks(v, V), to_chunks(log_f, K)))
    o = o.transpose(1, 0, 3, 2, 4).reshape(B, S, H, V)
    o = o * lax.rsqrt(jnp.mean(jnp.square(o), -1, keepdims=True) + RMS_EPS) * norm_g.astype(F32).reshape(H, V)
    o = o * jax.nn.silu(hg.astype(F32).reshape(B, S, H, V))
    return o.reshape(B, S, HG_WIDTH).astype(hq.dtype)


def multiscale_pool(p, w_pool, scale):
    B, S, W = p.shape
    pf = p.astype(F32)
    csum = jnp.concatenate([jnp.zeros((B, 1, W), F32), jnp.cumsum(pf, axis=1)], axis=1)
    t = jnp.arange(S)
    outs = []
    for gi, w in enumerate(POOL_WINDOWS):
        sl = slice(gi * POOL_GROUP, (gi + 1) * POOL_GROUP)
        cg = csum[..., sl]
        lo = jnp.maximum(t + 1 - w, 0)
        cnt = (t + 1 - lo).astype(F32)
        mixed = (cg[:, t + 1] - cg[:, lo]) / cnt[None, :, None] - pf[..., sl]
        outs.append(jnp.einsum('bsc,cd->bsd', mixed.astype(p.dtype), w_pool[gi]))
    return jnp.concatenate(outs, -1) * scale


def moe(x, router_w, router_b, w1, w3, w2, ws1, ws3, ws2):
    B, S, D = x.shape
    xf = x.reshape(-1, D)
    scores = jax.nn.sigmoid(jnp.matmul(xf, router_w, preferred_element_type=F32))
    _, idx = lax.top_k(scores + router_b.astype(F32), TOP_K)
    w = jnp.take_along_axis(scores, idx, -1)
    w = w / jnp.sum(w, -1, keepdims=True) * ROUTE_SCALE
    gate = jnp.sum(jax.nn.one_hot(idx, N_EXPERTS, dtype=F32) * w[..., None], axis=1)
    hdn = jax.nn.silu(jnp.einsum('nd,edf->nef', xf, w1)) * jnp.einsum('nd,edf->nef', xf, w3)
    routed = jnp.einsum('nef,efd->nd', hdn * gate[..., None].astype(xf.dtype), w2)
    shared = (jax.nn.silu(xf @ ws1) * (xf @ ws3)) @ ws2
    return (routed + shared).reshape(B, S, D)


def setup_inputs(seed: int = 0) -> dict:
    key = jax.random.key(seed)
    ks = jax.random.split(key, 32)
    L, D, d = DEPTH, D_MODEL, HEAD_DIM
    nrm = lambda k, shape, s: jax.random.normal(k, shape, F32) * s
    return {
        "x": nrm(ks[0], (BATCH, SEQ, D), 1.0),
        "positions": jnp.broadcast_to(jnp.arange(SEQ, dtype=jnp.int32), (BATCH, SEQ)),
        "ln_in_g": 1.0 + nrm(ks[1], (D,), 0.02),
        "ln_in_b": nrm(ks[2], (D,), 0.02),
        "w_in": nrm(ks[3], (L, D, N_IN), D ** -0.5),
        "cmp_pos_k": nrm(ks[4], (L, CMP_BLOCK, d), 0.02),
        "cmp_pos_v": nrm(ks[5], (L, CMP_BLOCK, d), 0.02),
        "cmp_w1_k": nrm(ks[6], (L, CMP_BLOCK * d, d), (CMP_BLOCK * d) ** -0.5),
        "cmp_w2_k": nrm(ks[7], (L, d, d), d ** -0.5),
        "cmp_w1_v": nrm(ks[8], (L, CMP_BLOCK * d, d), (CMP_BLOCK * d) ** -0.5),
        "cmp_w2_v": nrm(ks[9], (L, d, d), d ** -0.5),
        "hg_lb_logits": nrm(ks[10], (L, HG_KWIDTH), 1.0),
        "hg_norm_g": 1.0 + nrm(ks[11], (L, HG_WIDTH), 0.02),
        "pool_w": nrm(ks[12], (L, len(POOL_WINDOWS), POOL_GROUP, POOL_GROUP), POOL_GROUP ** -0.5),
        "pool_scale": 1.0 + nrm(ks[13], (L, POOL_WIDTH), 0.02),
        "w_up_attn": nrm(ks[14], (L, ATTN_WIDTH, D), ATTN_WIDTH ** -0.5 * DN_BETA),
        "w_up_hg": nrm(ks[15], (L, HG_WIDTH, D), HG_WIDTH ** -0.5 * DN_BETA),
        "w_up_pool": nrm(ks[16], (L, POOL_WIDTH, D), POOL_WIDTH ** -0.5 * DN_BETA),
        "w_o": nrm(ks[17], (L, D, D), D ** -0.5 * DN_BETA),
        "ln1_g": 1.0 + nrm(ks[18], (L, D), 0.02),
        "ln1_b": nrm(ks[19], (L, D), 0.02),
        "router_w": nrm(ks[20], (L, D, N_EXPERTS), D ** -0.5),
        "router_b": nrm(ks[21], (L, N_EXPERTS), 0.01),
        "w1": nrm(ks[22], (L, N_EXPERTS, D, D_EXPERT), D ** -0.5),
        "w3": nrm(ks[23], (L, N_EXPERTS, D, D_EXPERT), D ** -0.5),
        "w2": nrm(ks[24], (L, N_EXPERTS, D_EXPERT, D), D_EXPERT ** -0.5 * DN_BETA),
        "ws1": nrm(ks[25], (L, D, D_SHARED), D ** -0.5),
        "ws3": nrm(ks[26], (L, D, D_SHARED), D ** -0.5),
        "ws2": nrm(ks[27], (L, D_SHARED, D), D_SHARED ** -0.5 * DN_BETA),
        "ln2_g": 1.0 + nrm(ks[28], (L, D), 0.02),
        "ln2_b": nrm(ks[29], (L, D), 0.02),
    }


def reference(x, positions, ln_in_g, ln_in_b, w_in, cmp_pos_k, cmp_pos_v, cmp_w1_k, cmp_w2_k, cmp_w1_v, cmp_w2_v,
              hg_lb_logits, hg_norm_g, pool_w, pool_scale, w_up_attn, w_up_hg, w_up_pool, w_o, ln1_g, ln1_b,
              router_w, router_b, w1, w3, w2, ws1, ws3, ws2, ln2_g, ln2_b):
    B, S, D = x.shape
    lbs = jnp.cumsum(jax.nn.softmax(hg_lb_logits.astype(F32), axis=0), axis=0)
    lbs = lbs - lbs[0:1]
    split_points = [int(v) for v in np.cumsum(SPLIT_SIZES)[:-1]]
    h = layer_norm(x, ln_in_g, ln_in_b)
    for l in range(DEPTH):
        proj = jnp.matmul(h, w_in[l])
        (q, kc, vc, ks_, vs_, kw, vw, a_gate, hq, hf, hi, hg, pin, ga, gb, gc) = jnp.split(proj, split_points, axis=-1)
        kv = lambda t: t.reshape(B, S, KV_GROUPS, HEAD_DIM)
        ya = nsa_attention(q.reshape(B, S, ATTN_HEADS, HEAD_DIM), kv(kc), kv(vc), kv(ks_), kv(vs_), kv(kw), kv(vw),
                           a_gate, positions, cmp_pos_k[l], cmp_pos_v[l], cmp_w1_k[l], cmp_w2_k[l],
                           cmp_w1_v[l], cmp_w2_v[l])
        yb = hgrn2(hq, hf, hi, hg, lbs[l], hg_norm_g[l])
        yc = multiscale_pool(pin, pool_w[l], pool_scale[l])
        merged = (jax.nn.sigmoid(ga) * (ya @ w_up_attn[l])
                  + jax.nn.sigmoid(gb) * (yb @ w_up_hg[l])
                  + jax.nn.sigmoid(gc) * (yc @ w_up_pool[l]))
        h = layer_norm(DN_ALPHA * h + merged @ w_o[l], ln1_g[l], ln1_b[l])
        ffn = moe(h, router_w[l], router_b[l], w1[l], w3[l], w2[l], ws1[l], ws3[l], ws2[l])
        h = layer_norm(DN_ALPHA * h + ffn, ln2_g[l], ln2_b[l])
    return h
```

```python
import functools

import numpy as np
import jax
import jax.numpy as jnp
from jax import lax
from jax.experimental import pallas as pl
from jax.experimental.pallas import tpu as pltpu

F32 = jnp.float32
BF16 = jnp.bfloat16

DEPTH = 2
HEAD_DIM = 128
ATTN_HEADS = 16
KV_GROUPS = 2
HEADS_PER_GROUP = ATTN_HEADS // KV_GROUPS
ATTN_WIDTH = ATTN_HEADS * HEAD_DIM
KV_WIDTH = KV_GROUPS * HEAD_DIM
ROPE_DIM = HEAD_DIM // 4
ROPE_THETA = 500000.0
CMP_BLOCK = 32
CMP_STRIDE = 16
SLC_BLOCK = 32
SLC_TOPN = 8
WINDOW = 512
HG_HEADS = 8
HG_DK = 128
HG_DV = 128
HG_KWIDTH = HG_HEADS * HG_DK
HG_WIDTH = HG_HEADS * HG_DV
POOL_WINDOWS = (2, 4, 8, 16)
POOL_GROUP = 256
POOL_WIDTH = POOL_GROUP * len(POOL_WINDOWS)
N_EXPERTS = 64
TOP_K = 8
D_EXPERT = 128
D_SHARED = 256
ROUTE_SCALE = 2.5
DN_ALPHA = (2.0 * DEPTH) ** 0.25
LN_EPS = 1e-5
RMS_EPS = 1e-6
NEG = -1e30

LANES = 128
SUBLANES = 8
VMEM_LIMIT = 56 * 1024 * 1024

COL_Q = 0
COL_KV = ATTN_WIDTH
COL_AGATE = COL_KV + 6 * KV_WIDTH
ATTN_SLAB = 4096
COL_HG = ATTN_SLAB
COL_POOL = COL_HG + 4 * HG_WIDTH
COL_MG = COL_POOL + POOL_WIDTH
N_PACK = COL_MG

TQ = 128
KC = 512
WSPAN = WINDOW + TQ
HC = 128
HG_LEVELS = (64, 32, 16, 8)


def _cparams(*sem):
    return pltpu.CompilerParams(dimension_semantics=sem, vmem_limit_bytes=VMEM_LIMIT)


def _ln_kernel(alpha, has_res, *refs):
    if has_res:
        x_ref, r_ref, g_ref, b_ref, o32_ref, o16_ref = refs
        x = alpha * r_ref[...] + x_ref[...]
    else:
        x_ref, g_ref, b_ref, o32_ref, o16_ref = refs
        x = x_ref[...]
    mu = jnp.mean(x, axis=-1, keepdims=True)
    xc = x - mu
    var = jnp.mean(xc * xc, axis=-1, keepdims=True)
    y = xc * lax.rsqrt(var + LN_EPS) * g_ref[...] + b_ref[...]
    o32_ref[...] = y
    o16_ref[...] = y.astype(BF16)


def layer_norm_rows(x, g, b, res=None, alpha=1.0, tm=256):
    n, d = x.shape
    row = pl.BlockSpec((tm, d), lambda i: (i, 0))
    vec = pl.BlockSpec((1, d), lambda i: (0, 0))
    ins = [x] + ([res] if res is not None else []) + [g.reshape(1, d), b.reshape(1, d)]
    specs = [row] + ([row] if res is not None else []) + [vec, vec]
    return pl.pallas_call(
        functools.partial(_ln_kernel, alpha, res is not None),
        grid=(n // tm,),
        in_specs=specs,
        out_specs=[row, row],
        out_shape=[jax.ShapeDtypeStruct((n, d), F32), jax.ShapeDtypeStruct((n, d), BF16)],
        compiler_params=_cparams("parallel"),
        name="layer_norm",
    )(*ins)


def _mm_kernel(x_ref, w_ref, o_ref):
    o_ref[...] = jnp.dot(x_ref[...], w_ref[...], preferred_element_type=F32).astype(o_ref.dtype)


def matmul(x, w, out_dtype, tm=1024, tn=1024):
    n, k = x.shape
    m = w.shape[1]
    tm, tn = min(tm, n), min(tn, m)
    return pl.pallas_call(
        _mm_kernel,
        grid=(n // tm, m // tn),
        in_specs=[pl.BlockSpec((tm, k), lambda i, j: (i, 0)), pl.BlockSpec((k, tn), lambda i, j: (0, j))],
        out_specs=pl.BlockSpec((tm, tn), lambda i, j: (i, j)),
        out_shape=jax.ShapeDtypeStruct((n, m), out_dtype),
        compiler_params=_cparams("parallel", "parallel"),
        name="matmul",
    )(x, w)


def _rope_table_kernel(pos_ref, inv_ref, c_ref, sa_ref, sb_ref):
    ang = pos_ref[...].astype(F32) * inv_ref[...]
    lane = lax.broadcasted_iota(jnp.int32, ang.shape, 1)
    sn = jnp.sin(ang)
    c_ref[...] = jnp.cos(ang)
    sa_ref[...] = jnp.where(lane < ROPE_DIM // 2, -sn, 0.0)
    sb_ref[...] = jnp.where((lane >= ROPE_DIM // 2) & (lane < ROPE_DIM), sn, 0.0)


def rope_tables(positions, tm=1024):
    n = positions.size
    half = ROPE_DIM // 2
    inv = ROPE_THETA ** (-np.arange(half, dtype=np.float32) * 2.0 / ROPE_DIM)
    inv_full = np.zeros((1, LANES), np.float32)
    inv_full[0, :half] = inv
    inv_full[0, half:ROPE_DIM] = inv
    tm = min(tm, n)
    out = jax.ShapeDtypeStruct((n, LANES), F32)
    spec = pl.BlockSpec((tm, LANES), lambda i: (i, 0))
    return pl.pallas_call(
        _rope_table_kernel,
        grid=(n // tm,),
        in_specs=[pl.BlockSpec((tm, 1), lambda i: (i, 0)), pl.BlockSpec((1, LANES), lambda i: (0, 0))],
        out_specs=[spec, spec, spec],
        out_shape=[out, out, out],
        compiler_params=_cparams("parallel"),
        name="rope_tables",
    )(positions.reshape(n, 1), jnp.asarray(inv_full))


def _rope(x, c, sa, sb):
    return x * c + pltpu.roll(x, LANES - ROPE_DIM // 2, 1) * sa + pltpu.roll(x, ROPE_DIM // 2, 1) * sb


def _gelu_tanh(x):
    return 0.5 * x * (1.0 + jnp.tanh(0.7978845608028654 * (x + 0.044715 * x * x * x)))


def _top_mask(val, lane, n_pick):
    sel = jnp.zeros(val.shape, F32)
    for _ in range(n_pick):
        m = jnp.max(val, axis=-1, keepdims=True)
        idx = jnp.min(jnp.where(val == m, lane, LANES), axis=-1, keepdims=True)
        pick = lane == idx
        sel = jnp.where(pick, 1.0, sel)
        val = jnp.where(pick, -jnp.inf, val)
    return sel


def _attn_kernel(q_ref, kc_ref, vc_ref, ks_ref, vs_ref, kw_ref, vw_ref, gate_ref,
                 cq_ref, saq_ref, sbq_ref, ck_ref, sak_ref, sbk_ref,
                 posk_ref, posv_ref, w1k_ref, w2k_ref, w1v_ref, w2v_ref, ov_ref, e_ref,
                 o_ref, kcmp_s, vcmp_s, ksr_s, kwr_s, tmp_s):
    seq = ks_ref.shape[0]
    nh = seq // CMP_STRIDE
    n_cmp = (seq - CMP_BLOCK) // CMP_STRIDE + 1
    hpg = HEADS_PER_GROUP
    scale = HEAD_DIM ** -0.5
    qt = pl.program_id(2)
    nt = (((1,), (1,)), ((), ()))

    @pl.when(qt == 0)
    def _per_sequence():
        def compress(t_ref, pos_ref, w1_ref, w2_ref, out_s):
            tmp_s[...] = t_ref[...].astype(F32)
            first = jnp.zeros((nh, HEAD_DIM), F32)
            second = jnp.zeros((nh, HEAD_DIM), F32)
            for j in range(CMP_BLOCK):
                x = tmp_s[pl.ds(j % CMP_STRIDE, nh, stride=CMP_STRIDE), :] + pos_ref[j:j + 1, :]
                p = jnp.dot(x.astype(BF16), w1_ref[j], preferred_element_type=F32)
                if j < CMP_STRIDE:
                    first = first + p
                else:
                    second = second + p
            pre = first + pltpu.roll(second, nh - 1, 0)
            hid = _gelu_tanh(pre).astype(BF16)
            out_s[...] = jnp.dot(hid, w2_ref[...], preferred_element_type=F32).astype(BF16)

        compress(kc_ref, posk_ref, w1k_ref, w2k_ref, kcmp_s)
        compress(vc_ref, posv_ref, w1v_ref, w2v_ref, vcmp_s)
        ck, sak, sbk = ck_ref[...], sak_ref[...], sbk_ref[...]
        ksr_s[...] = _rope(ks_ref[...].astype(F32), ck, sak, sbk).astype(BF16)
        kwr_s[...] = _rope(kw_ref[...].astype(F32), ck, sak, sbk).astype(BF16)

    t0 = qt * TQ
    t = t0 + lax.broadcasted_iota(jnp.int32, (TQ, 1), 0)
    lane = lax.broadcasted_iota(jnp.int32, (TQ, LANES), 1)
    q = q_ref[...]
    qf = q.astype(F32)
    cq, saq, sbq = cq_ref[...], saq_ref[...], sbq_ref[...]
    heads = [slice(h * HEAD_DIM, (h + 1) * HEAD_DIM) for h in range(hpg)]
    q_raw = jnp.concatenate([q[:, s] for s in heads], axis=0)
    q_rot = jnp.concatenate([_rope(qf[:, s], cq, saq, sbq).astype(BF16) for s in heads], axis=0)

    s = lax.dot_general(q_raw, kcmp_s[...], nt, preferred_element_type=F32) * scale
    s3 = s.reshape(hpg, TQ, nh)
    vis = (lane * CMP_STRIDE + (CMP_BLOCK - 1) <= t) & (lane < n_cmp)
    s3 = jnp.where(vis[None], s3, NEG)
    e3 = jnp.exp(s3 - jnp.max(s3, axis=-1, keepdims=True))
    p3 = e3 / jnp.sum(e3, axis=-1, keepdims=True) * vis[None].astype(F32)
    pb = p3.astype(BF16)
    o_c = jnp.dot(pb.reshape(hpg * TQ, nh), vcmp_s[...], preferred_element_type=F32).reshape(hpg, TQ, HEAD_DIM)

    psum = jnp.sum(pb.astype(F32), axis=0)
    imp = jnp.dot(psum, ov_ref[...], preferred_element_type=F32, precision=lax.Precision.HIGHEST)
    blk_t = t // SLC_BLOCK
    causal = lane <= blk_t
    forced = (lane == 0) | (lane == blk_t) | (lane == blk_t - 1)
    val = jnp.where(forced, jnp.inf, jnp.where(causal, imp, -jnp.inf))
    sel = jnp.where(causal, _top_mask(val, lane, SLC_TOPN), 0.0).astype(BF16)

    def sel_chunk(ci, carry):
        m, l, acc = carry
        k0 = pl.multiple_of(ci * KC, KC)
        kk = ksr_s[pl.ds(k0, KC), :]
        vv = vs_ref[pl.ds(k0, KC), :]
        sc = lax.dot_general(q_rot, kk, nt, preferred_element_type=F32) * scale
        sc = sc.reshape(hpg, TQ, KC)
        chosen = jnp.dot(sel, e_ref[:, pl.ds(k0, KC)], preferred_element_type=F32)
        kpos = k0 + lax.broadcasted_iota(jnp.int32, (TQ, KC), 1)
        valid = ((chosen > 0.5) & (kpos <= t))[None]
        sc = jnp.where(valid, sc, NEG)
        m_new = jnp.maximum(m, jnp.max(sc, axis=-1, keepdims=True))
        a = jnp.exp(m - m_new)
        p = jnp.where(valid, jnp.exp(sc - m_new), 0.0)
        l = a * l + jnp.sum(p, axis=-1, keepdims=True)
        pv = jnp.dot(p.astype(BF16).reshape(hpg * TQ, KC), vv, preferred_element_type=F32)
        return m_new, l, a * acc + pv.reshape(hpg, TQ, HEAD_DIM)

    init = (jnp.full((hpg, TQ, 1), NEG, F32), jnp.zeros((hpg, TQ, 1), F32), jnp.zeros((hpg, TQ, HEAD_DIM), F32))
    _, l_s, acc_s = lax.fori_loop(0, (t0 + TQ + KC - 1) // KC, sel_chunk, init)
    o_s = acc_s / l_s

    w0 = pl.multiple_of(jnp.maximum(t0 - WINDOW, 0), TQ)
    kk = kwr_s[pl.ds(w0, WSPAN), :]
    vv = vw_ref[pl.ds(w0, WSPAN), :]
    sw = lax.dot_general(q_rot, kk, nt, preferred_element_type=F32) * scale
    sw = sw.reshape(hpg, TQ, WSPAN)
    kpos = w0 + lax.broadcasted_iota(jnp.int32, (TQ, WSPAN), 1)
    valid = ((kpos <= t) & (t - kpos < WINDOW))[None]
    sw = jnp.where(valid, sw, NEG)
    pw = jnp.exp(sw - jnp.max(sw, axis=-1, keepdims=True))
    pw = pw / jnp.sum(pw, axis=-1, keepdims=True)
    o_w = jnp.dot(pw.astype(BF16).reshape(hpg * TQ, WSPAN), vv, preferred_element_type=F32)
    o_w = o_w.reshape(hpg, TQ, HEAD_DIM)

    g = jax.nn.sigmoid(gate_ref[...].astype(F32))
    for h in range(hpg):
        o = (g[:, h:h + 1] * o_c[h] + g[:, hpg + h:hpg + h + 1] * o_s[h]
             + g[:, 2 * hpg + h:2 * hpg + h + 1] * o_w[h])
        o_ref[:, heads[h]] = o.astype(BF16)


def _overlap_matrix(n_half, n_cmp, n_slc):
    c = np.arange(n_half)[:, None] * CMP_STRIDE
    s = np.arange(LANES)[None, :] * SLC_BLOCK
    ov = np.clip(np.minimum(c + CMP_BLOCK, s + SLC_BLOCK) - np.maximum(c, s), 0, None) / CMP_STRIDE
    ov[n_cmp:, :] = 0.0
    ov[:, n_slc:] = 0.0
    return ov.astype(np.float32)


def nsa_attention(proj, tabs, cmp_params, batch, seq):
    n = batch * seq
    nq = seq // TQ
    nh = seq // CMP_STRIDE
    n_cmp = (seq - CMP_BLOCK) // CMP_STRIDE + 1
    n_slc = seq // SLC_BLOCK
    assert nh == LANES and n_slc <= LANES and seq % KC == 0 and seq >= WSPAN
    c_tab, sa_tab, sb_tab = tabs
    posk, posv, w1k, w2k, w1v, w2v = cmp_params
    ov = jnp.asarray(_overlap_matrix(nh, n_cmp, n_slc))
    expand = (np.arange(seq)[None, :] // SLC_BLOCK == np.arange(LANES)[:, None])
    expand = jnp.asarray(expand.astype(np.float32), dtype=BF16)

    gw = HEADS_PER_GROUP * HEAD_DIM
    qspec = pl.BlockSpec((TQ, gw), lambda b, g, i: (b * nq + i, g))

    def kvspec(slab):
        return pl.BlockSpec((seq, HEAD_DIM), lambda b, g, i: (b, COL_KV // HEAD_DIM + slab * KV_GROUPS + g))

    gspec = pl.BlockSpec((TQ, LANES), lambda b, g, i: (b * nq + i, COL_AGATE // LANES + g))
    tq_spec = pl.BlockSpec((TQ, LANES), lambda b, g, i: (b * nq + i, 0))
    tk_spec = pl.BlockSpec((seq, LANES), lambda b, g, i: (b, 0))

    def full(a):
        return pl.BlockSpec(a.shape, lambda b, g, i: (0,) * a.ndim)

    consts = [posk, posv, w1k, w2k, w1v, w2v, ov, expand]
    return pl.pallas_call(
        _attn_kernel,
        grid=(batch, KV_GROUPS, nq),
        in_specs=[qspec] + [kvspec(s) for s in range(6)] + [gspec] + [tq_spec] * 3 + [tk_spec] * 3
                 + [full(a) for a in consts],
        out_specs=pl.BlockSpec((TQ, gw), lambda b, g, i: (b * nq + i, g)),
        out_shape=jax.ShapeDtypeStruct((n, ATTN_WIDTH), BF16),
        scratch_shapes=[pltpu.VMEM((nh, HEAD_DIM), BF16), pltpu.VMEM((nh, HEAD_DIM), BF16),
                        pltpu.VMEM((seq, HEAD_DIM), BF16), pltpu.VMEM((seq, HEAD_DIM), BF16),
                        pltpu.VMEM((seq, HEAD_DIM), F32)],
        compiler_params=_cparams("parallel", "parallel", "arbitrary"),
        name="nsa_attention",
    )(proj, proj, proj, proj, proj, proj, proj, proj, c_tab, sa_tab, sb_tab, c_tab, sa_tab, sb_tab, *consts)


def _hgrn_level_tables():
    t = np.arange(HC)[:, None]
    s = np.arange(HC)[None, :]
    lv = np.full((HC, HC), -1, np.int32)
    lv[(t // SUBLANES == s // SUBLANES)] = -1
    for i, m in enumerate(HG_LEVELS):
        ok = ((t // m) % 2 == 1) & (s // m == t // m - 1)
        lv[ok] = i
    tril = (s <= t).astype(np.float32)
    return lv, tril


def _hgrn_kernel(q_ref, f_ref, i_ref, g_ref, lb_ref, ng_ref, lv_ref, tril_ref, o_ref, st_s, b_s):
    c = pl.program_id(2)
    nt = (((1,), (1,)), ((), ()))
    tn = (((0,), (0,)), ((), ()))

    @pl.when(c == 0)
    def _():
        st_s[...] = jnp.zeros_like(st_s)

    lb = lb_ref[0]
    f = lb + (1.0 - lb) * jax.nn.sigmoid(f_ref[...].astype(F32))
    logf = jnp.log(f)
    kk = 1.0 - f
    q = q_ref[...].astype(F32)
    v = i_ref[...].astype(F32)
    vb = i_ref[...]
    b = jnp.dot(tril_ref[...], logf, preferred_element_type=F32, precision=lax.Precision.HIGHEST)
    b_s[...] = b
    row = lax.broadcasted_iota(jnp.int32, (HC, HG_DK), 0)

    st = st_s[...]
    o = lax.dot_general((q * jnp.exp(b)).astype(BF16), st.astype(BF16), nt, preferred_element_type=F32)

    lv = lv_ref[...]
    a = jnp.zeros((HC, HC), F32)
    for i, m in enumerate(HG_LEVELS):
        ref_rows = [jnp.broadcast_to(b_s[pl.ds((2 * j + 1) * m - 1, 1), :], (2 * m, HG_DK))
                    for j in range(HC // (2 * m))]
        d = b - jnp.concatenate(ref_rows, axis=0)
        odd = (row // m) % 2 == 1
        x = (jnp.where(odd, q, kk) * jnp.exp(-jnp.abs(d))).astype(BF16)
        am = lax.dot_general(x, x, nt, preferred_element_type=F32)
        a = jnp.where(lv == i, am, a)
    o = o + jnp.dot(a.astype(BF16), vb, preferred_element_type=F32)

    nb = HC // SUBLANES
    b3 = b.reshape(nb, SUBLANES, HG_DK)
    q3 = q.reshape(nb, SUBLANES, HG_DK)
    k3 = kk.reshape(nb, SUBLANES, HG_DK)
    v3 = v.reshape(nb, SUBLANES, HG_DV)
    r3 = lax.broadcasted_iota(jnp.int32, (nb, SUBLANES, HG_DK), 1)
    od = jnp.zeros((nb, SUBLANES, HG_DV), F32)
    for j in range(SUBLANES):
        bj = jnp.broadcast_to(b3[:, j:j + 1, :], b3.shape)
        kj = jnp.broadcast_to(k3[:, j:j + 1, :], b3.shape)
        vj = jnp.broadcast_to(v3[:, j:j + 1, :], b3.shape)
        dec = jnp.where(r3 >= j, jnp.exp(jnp.minimum(b3 - bj, 0.0)), 0.0)
        od = od + jnp.sum(q3 * kj * dec, axis=-1, keepdims=True) * vj
    o = o + od.reshape(HC, HG_DV)

    b_last = b_s[pl.ds(HC - 1, 1), :]
    kd = (kk * jnp.exp(b_last - b)).astype(BF16)
    st_s[...] = st * jnp.exp(b_last) + lax.dot_general(vb, kd, tn, preferred_element_type=F32)

    o = o * lax.rsqrt(jnp.mean(o * o, axis=-1, keepdims=True) + RMS_EPS) * ng_ref[0]
    gg = g_ref[...].astype(F32)
    o_ref[...] = (o * (gg * jax.nn.sigmoid(gg))).astype(BF16)


def hgrn2(proj, lb, norm_g, batch, seq):
    n = batch * seq
    nc = seq // HC
    lv, tril = _hgrn_level_tables()
    base = COL_HG // HG_DK

    def slab(k):
        return pl.BlockSpec((HC, HG_DK), lambda b, h, c: (b * nc + c, base + k * HG_HEADS + h))

    vec = pl.BlockSpec((1, 1, HG_DK), lambda b, h, c: (h, 0, 0))
    const = pl.BlockSpec((HC, HC), lambda b, h, c: (0, 0))
    return pl.pallas_call(
        _hgrn_kernel,
        grid=(batch, HG_HEADS, nc),
        in_specs=[slab(0), slab(1), slab(2), slab(3), vec, vec, const, const],
        out_specs=pl.BlockSpec((HC, HG_DV), lambda b, h, c: (b * nc + c, h)),
        out_shape=jax.ShapeDtypeStruct((n, HG_WIDTH), BF16),
        scratch_shapes=[pltpu.VMEM((HG_DV, HG_DK), F32), pltpu.VMEM((HC, HG_DK), F32)],
        compiler_params=_cparams("parallel", "parallel", "arbitrary"),
        name="hgrn2",
    )(proj, proj, proj, proj, lb.reshape(HG_HEADS, 1, HG_DK), norm_g.reshape(HG_HEADS, 1, HG_DV),
      jnp.asarray(lv), jnp.asarray(tril))


def _pool_kernel(p_ref, w_ref, sc_ref, o_ref):
    gi = pl.program_id(1)
    x = p_ref[...].astype(F32)
    t = lax.broadcasted_iota(jnp.int32, x.shape, 0)
    acc = x
    for k in range(len(POOL_WINDOWS)):
        sh = 1 << k
        nxt = acc + jnp.where(t >= sh, pltpu.roll(acc, sh, 0), 0.0)
        acc = jnp.where(k <= gi, nxt, acc)
    width = lax.shift_left(jnp.int32(2), gi)
    cnt = jnp.minimum(t + 1, width).astype(F32)
    mixed = acc / cnt - x
    y = jnp.dot(mixed.astype(BF16), w_ref[0], preferred_element_type=F32) * sc_ref[0]
    o_ref[...] = y.astype(BF16)


def multiscale_pool(proj, w_pool, scale, batch, seq):
    n = batch * seq
    ng = len(POOL_WINDOWS)
    base = COL_POOL // POOL_GROUP
    return pl.pallas_call(
        _pool_kernel,
        grid=(batch, ng),
        in_specs=[pl.BlockSpec((seq, POOL_GROUP), lambda b, g: (b, base + g)),
                  pl.BlockSpec((1, POOL_GROUP, POOL_GROUP), lambda b, g: (g, 0, 0)),
                  pl.BlockSpec((1, 1, POOL_GROUP), lambda b, g: (g, 0, 0))],
        out_specs=pl.BlockSpec((seq, POOL_GROUP), lambda b, g: (b, g)),
        out_shape=jax.ShapeDtypeStruct((n, POOL_WIDTH), BF16),
        compiler_params=_cparams("parallel", "parallel"),
        name="multiscale_pool",
    )(proj, w_pool, scale.reshape(ng, 1, POOL_GROUP))


def _merge_kernel(ya_ref, yb_ref, yc_ref, wa_ref, wb_ref, wc_ref, ga_ref, gb_ref, gc_ref, o_ref):
    def branch(y_ref, w_ref, g_ref):
        up = jnp.dot(y_ref[...], w_ref[...], preferred_element_type=F32)
        return jax.nn.sigmoid(g_ref[...].astype(F32)) * up

    o_ref[...] = (branch(ya_ref, wa_ref, ga_ref) + branch(yb_ref, wb_ref, gb_ref)
                  + branch(yc_ref, wc_ref, gc_ref)).astype(o_ref.dtype)


def merge_branches(ya, yb, yc, wa, wb, wc, proj, d_model, tm=1024, tn=512):
    n = ya.shape[0]
    tm = min(tm, n)
    gbase = COL_MG // tn
    step = d_model // tn

    def y_spec(a):
        return pl.BlockSpec((tm, a.shape[1]), lambda i, j: (i, 0))

    def w_spec(a):
        return pl.BlockSpec((a.shape[0], tn), lambda i, j: (0, j))

    def g_spec(k):
        return pl.BlockSpec((tm, tn), lambda i, j: (i, gbase + k * step + j))

    return pl.pallas_call(
        _merge_kernel,
        grid=(n // tm, d_model // tn),
        in_specs=[y_spec(ya), y_spec(yb), y_spec(yc), w_spec(wa), w_spec(wb), w_spec(wc),
                  g_spec(0), g_spec(1), g_spec(2)],
        out_specs=pl.BlockSpec((tm, tn), lambda i, j: (i, j)),
        out_shape=jax.ShapeDtypeStruct((n, d_model), BF16),
        compiler_params=_cparams("parallel", "parallel"),
        name="merge_branches",
    )(ya, yb, yc, wa, wb, wc, proj, proj, proj)


def _router_kernel(x_ref, w_ref, b_ref, o_ref):
    logits = jnp.dot(x_ref[...], w_ref[...], preferred_element_type=F32)
    scores = jax.nn.sigmoid(logits)
    lane = lax.broadcasted_iota(jnp.int32, scores.shape, 1)
    real = lane < N_EXPERTS
    sel = _top_mask(jnp.where(real, scores + b_ref[...], -jnp.inf), lane, TOP_K)
    w = jnp.where(real, sel * scores, 0.0)
    w = w / jnp.sum(w, axis=-1, keepdims=True) * ROUTE_SCALE
    o_ref[...] = jnp.where((lane >= N_EXPERTS) & (lane < N_EXPERTS + D_SHARED // D_EXPERT), 1.0, w)


def moe_router(x, router_w, router_b, tm=512):
    n, d = x.shape
    tm = min(tm, n)
    w = jnp.zeros((d, LANES), BF16).at[:, :N_EXPERTS].set(router_w.astype(BF16))
    b = jnp.zeros((1, LANES), F32).at[0, :N_EXPERTS].set(router_b.astype(F32))
    return pl.pallas_call(
        _router_kernel,
        grid=(n // tm,),
        in_specs=[pl.BlockSpec((tm, d), lambda i: (i, 0)), pl.BlockSpec((d, LANES), lambda i: (0, 0)),
                  pl.BlockSpec((1, LANES), lambda i: (0, 0))],
        out_specs=pl.BlockSpec((tm, LANES), lambda i: (i, 0)),
        out_shape=jax.ShapeDtypeStruct((n, LANES), F32),
        compiler_params=_cparams("parallel"),
        name="moe_router",
    )(x, w, b)


def _moe_kernel(x_ref, w1_ref, w3_ref, w2_ref, gate_ref, ex_ref, o_ref):
    j = pl.program_id(1)
    x = x_ref[...]
    h1 = jnp.dot(x, w1_ref[...], preferred_element_type=F32)
    h3 = jnp.dot(x, w3_ref[...], preferred_element_type=F32)
    g = gate_ref[...]
    g_hi = g.astype(BF16)
    g_lo = (g - g_hi.astype(F32)).astype(BF16)
    ex = ex_ref[...]
    gexp = jnp.dot(g_hi, ex, preferred_element_type=F32) + jnp.dot(g_lo, ex, preferred_element_type=F32)
    hd = (h1 * jax.nn.sigmoid(h1) * h3 * gexp).astype(BF16)
    y = jnp.dot(hd, w2_ref[...], preferred_element_type=F32)

    @pl.when(j == 0)
    def _():
        o_ref[...] = y

    @pl.when(j > 0)
    def _():
        o_ref[...] += y


def moe_experts(x, gate, w1c, w3c, w2c, tm=512, ec=3):
    n, d = x.shape
    tm = min(tm, n)
    hidden = w1c.shape[1]
    th = ec * D_EXPERT
    assert hidden % th == 0
    expand = (np.arange(hidden)[None, :] // D_EXPERT == np.arange(LANES)[:, None])
    expand = jnp.asarray(expand.astype(np.float32), dtype=BF16)
    return pl.pallas_call(
        _moe_kernel,
        grid=(n // tm, hidden // th),
        in_specs=[pl.BlockSpec((tm, d), lambda i, j: (i, 0)),
                  pl.BlockSpec((d, th), lambda i, j: (0, j)),
                  pl.BlockSpec((d, th), lambda i, j: (0, j)),
                  pl.BlockSpec((th, d), lambda i, j: (j, 0)),
                  pl.BlockSpec((tm, LANES), lambda i, j: (i, 0)),
                  pl.BlockSpec((LANES, th), lambda i, j: (0, j))],
        out_specs=pl.BlockSpec((tm, d), lambda i, j: (i, 0)),
        out_shape=jax.ShapeDtypeStruct((n, d), F32),
        compiler_params=_cparams("parallel", "arbitrary"),
        name="moe_experts",
    )(x, w1c, w3c, w2c, gate, expand)


def _pack_w_in(w, d_model):
    sizes = (ATTN_WIDTH, KV_WIDTH, KV_WIDTH, KV_WIDTH, KV_WIDTH, KV_WIDTH, KV_WIDTH, 3 * ATTN_HEADS,
             HG_KWIDTH, HG_KWIDTH, HG_WIDTH, HG_WIDTH, POOL_WIDTH, d_model, d_model, d_model)
    offs = np.concatenate([[0], np.cumsum(sizes)])
    seg = lambda i: w[:, offs[i]:offs[i + 1]]
    ag = seg(7).reshape(d_model, 3, KV_GROUPS, HEADS_PER_GROUP)
    zpad = lambda c: jnp.zeros((d_model, c), w.dtype)
    gates = []
    for g in range(KV_GROUPS):
        gates += [ag[:, :, g, :].reshape(d_model, 3 * HEADS_PER_GROUP), zpad(LANES - 3 * HEADS_PER_GROUP)]
    parts = ([seg(i) for i in range(7)] + gates + [zpad(ATTN_SLAB - COL_AGATE - KV_GROUPS * LANES)]
             + [seg(i) for i in range(8, 16)])
    return jnp.concatenate(parts, axis=1).astype(BF16)


def _hgrn_lower_bounds(logits):
    lbs = jnp.cumsum(jax.nn.softmax(logits.astype(F32), axis=0), axis=0)
    return lbs - lbs[0:1]


def kernel(x, positions, ln_in_g, ln_in_b, w_in, cmp_pos_k, cmp_pos_v, cmp_w1_k, cmp_w2_k, cmp_w1_v, cmp_w2_v,
           hg_lb_logits, hg_norm_g, pool_w, pool_scale, w_up_attn, w_up_hg, w_up_pool, w_o, ln1_g, ln1_b,
           router_w, router_b, w1, w3, w2, ws1, ws3, ws2, ln2_g, ln2_b):
    batch, seq, d = x.shape
    n = batch * seq
    depth = w_in.shape[0]
    lbs = _hgrn_lower_bounds(hg_lb_logits)
    tabs = rope_tables(positions)
    h32, h16 = layer_norm_rows(x.reshape(n, d), ln_in_g, ln_in_b)
    for l in range(depth):
        proj = matmul(h16, _pack_w_in(w_in[l], d), BF16)
        cmp_params = (cmp_pos_k[l], cmp_pos_v[l],
                      cmp_w1_k[l].reshape(CMP_BLOCK, HEAD_DIM, HEAD_DIM).astype(BF16), cmp_w2_k[l].astype(BF16),
                      cmp_w1_v[l].reshape(CMP_BLOCK, HEAD_DIM, HEAD_DIM).astype(BF16), cmp_w2_v[l].astype(BF16))
        ya = nsa_attention(proj, tabs, cmp_params, batch, seq)
        yb = hgrn2(proj, lbs[l], hg_norm_g[l], batch, seq)
        yc = multiscale_pool(proj, pool_w[l].astype(BF16), pool_scale[l], batch, seq)
        merged = merge_branches(ya, yb, yc, w_up_attn[l].astype(BF16), w_up_hg[l].astype(BF16),
                                w_up_pool[l].astype(BF16), proj, d)
        mix = matmul(merged, w_o[l].astype(BF16), F32)
        h32, h16 = layer_norm_rows(mix, ln1_g[l], ln1_b[l], res=h32, alpha=DN_ALPHA)
        gate = moe_router(h16, router_w[l], router_b[l])
        w1c = jnp.concatenate([jnp.moveaxis(w1[l], 0, 1).reshape(d, N_EXPERTS * D_EXPERT), ws1[l]], 1).astype(BF16)
        w3c = jnp.concatenate([jnp.moveaxis(w3[l], 0, 1).reshape(d, N_EXPERTS * D_EXPERT), ws3[l]], 1).astype(BF16)
        w2c = jnp.concatenate([w2[l].reshape(N_EXPERTS * D_EXPERT, d), ws2[l]], 0).astype(BF16)
        ffn = moe_experts(h16, gate, w1c, w3c, w2c)
        h32, h16 = layer_norm_rows(ffn, ln2_g[l], ln2_b[l], res=h32, alpha=DN_ALPHA)
    return h32.reshape(batch, seq, d)
```

```python
import functools

import numpy as np
import jax
import jax.numpy as jnp
from jax import lax
from jax.experimental import pallas as pl
from jax.experimental.pallas import tpu as pltpu

F32 = jnp.float32
BF16 = jnp.bfloat16

DEPTH = 2
HEAD_DIM = 128
ATTN_HEADS = 16
KV_GROUPS = 2
HEADS_PER_GROUP = ATTN_HEADS // KV_GROUPS
ATTN_WIDTH = ATTN_HEADS * HEAD_DIM
KV_WIDTH = KV_GROUPS * HEAD_DIM
ROPE_DIM = HEAD_DIM // 4
ROPE_THETA = 500000.0
CMP_BLOCK = 32
CMP_STRIDE = 16
SLC_BLOCK = 32
SLC_TOPN = 8
WINDOW = 512
HG_HEADS = 8
HG_DK = 128
HG_DV = 128
HG_KWIDTH = HG_HEADS * HG_DK
HG_WIDTH = HG_HEADS * HG_DV
POOL_WINDOWS = (2, 4, 8, 16)
POOL_GROUP = 256
POOL_WIDTH = POOL_GROUP * len(POOL_WINDOWS)
N_EXPERTS = 64
TOP_K = 8
D_EXPERT = 128
D_SHARED = 256
ROUTE_SCALE = 2.5
DN_ALPHA = (2.0 * DEPTH) ** 0.25
LN_EPS = 1e-5
RMS_EPS = 1e-6
NEG = -1e30

LANES = 128
SUBLANES = 8
VMEM_LIMIT = 56 * 1024 * 1024

COL_Q = 0
COL_KV = ATTN_WIDTH
COL_AGATE = COL_KV + 6 * KV_WIDTH
ATTN_SLAB = 4096
COL_HG = ATTN_SLAB
COL_POOL = COL_HG + 4 * HG_WIDTH
COL_MG = COL_POOL + POOL_WIDTH
N_PACK = COL_MG

TQ = 128
KC = 512
WSPAN = WINDOW + TQ
HC = 128
HG_LEVELS = (64, 32, 16, 8)
HG_STEP_HEADS = 8


def _cparams(*sem):
    return pltpu.CompilerParams(dimension_semantics=sem, vmem_limit_bytes=VMEM_LIMIT)


def _ln_kernel(alpha, has_res, *refs):
    if has_res:
        x_ref, r_ref, g_ref, b_ref, o32_ref, o16_ref = refs
        x = alpha * r_ref[...] + x_ref[...].astype(F32)
    else:
        x_ref, g_ref, b_ref, o32_ref, o16_ref = refs
        x = x_ref[...]
    mu = jnp.mean(x, axis=-1, keepdims=True)
    xc = x - mu
    var = jnp.mean(xc * xc, axis=-1, keepdims=True)
    y = xc * lax.rsqrt(var + LN_EPS) * g_ref[...] + b_ref[...]
    o32_ref[...] = y
    o16_ref[...] = y.astype(BF16)


def layer_norm_rows(x, g, b, res=None, alpha=1.0, tm=256):
    n, d = x.shape
    row = pl.BlockSpec((tm, d), lambda i: (i, 0))
    vec = pl.BlockSpec((1, d), lambda i: (0, 0))
    ins = [x] + ([res] if res is not None else []) + [g.reshape(1, d), b.reshape(1, d)]
    specs = [row] + ([row] if res is not None else []) + [vec, vec]
    return pl.pallas_call(
        functools.partial(_ln_kernel, alpha, res is not None),
        grid=(n // tm,),
        in_specs=specs,
        out_specs=[row, row],
        out_shape=[jax.ShapeDtypeStruct((n, d), F32), jax.ShapeDtypeStruct((n, d), BF16)],
        compiler_params=_cparams("parallel"),
        name="layer_norm",
    )(*ins)


def _mm_kernel(x_ref, w_ref, o_ref):
    o_ref[...] = jnp.dot(x_ref[...], w_ref[...], preferred_element_type=F32).astype(o_ref.dtype)


def matmul(x, w, out_dtype, tm=1024, tn=1024):
    n, k = x.shape
    m = w.shape[1]
    tm, tn = min(tm, n), min(tn, m)
    return pl.pallas_call(
        _mm_kernel,
        grid=(n // tm, m // tn),
        in_specs=[pl.BlockSpec((tm, k), lambda i, j: (i, 0)), pl.BlockSpec((k, tn), lambda i, j: (0, j))],
        out_specs=pl.BlockSpec((tm, tn), lambda i, j: (i, j)),
        out_shape=jax.ShapeDtypeStruct((n, m), out_dtype),
        compiler_params=_cparams("parallel", "parallel"),
        name="matmul",
    )(x, w)


def _rope_table_kernel(pos_ref, inv_ref, c_ref, sa_ref, sb_ref):
    ang = pos_ref[...].astype(F32) * inv_ref[...]
    lane = lax.broadcasted_iota(jnp.int32, ang.shape, 1)
    sn = jnp.sin(ang)
    c_ref[...] = jnp.cos(ang)
    sa_ref[...] = jnp.where(lane < ROPE_DIM // 2, -sn, 0.0)
    sb_ref[...] = jnp.where((lane >= ROPE_DIM // 2) & (lane < ROPE_DIM), sn, 0.0)


def rope_tables(positions, tm=1024):
    n = positions.size
    half = ROPE_DIM // 2
    inv = ROPE_THETA ** (-np.arange(half, dtype=np.float32) * 2.0 / ROPE_DIM)
    inv_full = np.zeros((1, LANES), np.float32)
    inv_full[0, :half] = inv
    inv_full[0, half:ROPE_DIM] = inv
    tm = min(tm, n)
    out = jax.ShapeDtypeStruct((n, LANES), F32)
    spec = pl.BlockSpec((tm, LANES), lambda i: (i, 0))
    return pl.pallas_call(
        _rope_table_kernel,
        grid=(n // tm,),
        in_specs=[pl.BlockSpec((tm, 1), lambda i: (i, 0)), pl.BlockSpec((1, LANES), lambda i: (0, 0))],
        out_specs=[spec, spec, spec],
        out_shape=[out, out, out],
        compiler_params=_cparams("parallel"),
        name="rope_tables",
    )(positions.reshape(n, 1), jnp.asarray(inv_full))


def _rope(x, c, sa, sb):
    return x * c + pltpu.roll(x, LANES - ROPE_DIM // 2, 1) * sa + pltpu.roll(x, ROPE_DIM // 2, 1) * sb


def _gelu_tanh(x):
    return 0.5 * x * (1.0 + jnp.tanh(0.7978845608028654 * (x + 0.044715 * x * x * x)))


def _top_mask(val, lane, n_pick):
    sel = jnp.zeros(val.shape, F32)
    for _ in range(n_pick):
        m = jnp.max(val, axis=-1, keepdims=True)
        idx = jnp.min(jnp.where(val == m, lane, LANES), axis=-1, keepdims=True)
        pick = lane == idx
        sel = jnp.where(pick, 1.0, sel)
        val = jnp.where(pick, -jnp.inf, val)
    return sel


def _attn_kernel(q_ref, kc_ref, vc_ref, ks_ref, vs_ref, kw_ref, vw_ref, gate_ref,
                 cq_ref, saq_ref, sbq_ref, ck_ref, sak_ref, sbk_ref,
                 posk_ref, posv_ref, w1k_ref, w2k_ref, w1v_ref, w2v_ref, ov_ref, e_ref,
                 o_ref, kcmp_s, vcmp_s, ksr_s, kwr_s, tmp_s, sc_s, bias_s, sw_s, wbias_s):
    seq = ks_ref.shape[0]
    nh = seq // CMP_STRIDE
    n_cmp = (seq - CMP_BLOCK) // CMP_STRIDE + 1
    hpg = HEADS_PER_GROUP
    scale = HEAD_DIM ** -0.5
    scale2 = scale * 1.4426950408889634
    qt = pl.program_id(2)
    nt = (((1,), (1,)), ((), ()))

    @pl.when(qt == 0)
    def _per_sequence():
        def compress(t_ref, pos_ref, w1_ref, w2_ref, out_s):
            tmp_s[...] = t_ref[...].astype(F32)
            first = jnp.zeros((nh, HEAD_DIM), F32)
            second = jnp.zeros((nh, HEAD_DIM), F32)
            for j in range(CMP_BLOCK):
                x = tmp_s[pl.ds(j % CMP_STRIDE, nh, stride=CMP_STRIDE), :] + pos_ref[j:j + 1, :]
                p = jnp.dot(x.astype(BF16), w1_ref[j], preferred_element_type=F32)
                if j < CMP_STRIDE:
                    first = first + p
                else:
                    second = second + p
            pre = first + pltpu.roll(second, nh - 1, 0)
            hid = _gelu_tanh(pre).astype(BF16)
            out_s[...] = jnp.dot(hid, w2_ref[...], preferred_element_type=F32).astype(BF16)

        compress(kc_ref, posk_ref, w1k_ref, w2k_ref, kcmp_s)
        compress(vc_ref, posv_ref, w1v_ref, w2v_ref, vcmp_s)
        ck, sak, sbk = ck_ref[...], sak_ref[...], sbk_ref[...]
        ksr_s[...] = _rope(ks_ref[...].astype(F32), ck, sak, sbk).astype(BF16)
        kwr_s[...] = _rope(kw_ref[...].astype(F32), ck, sak, sbk).astype(BF16)

    t0 = qt * TQ
    t = t0 + lax.broadcasted_iota(jnp.int32, (TQ, 1), 0)
    lane = lax.broadcasted_iota(jnp.int32, (TQ, LANES), 1)
    q = q_ref[...]
    qf = q.astype(F32)
    cq, saq, sbq = cq_ref[...], saq_ref[...], sbq_ref[...]
    heads = [slice(h * HEAD_DIM, (h + 1) * HEAD_DIM) for h in range(hpg)]
    q_raw = jnp.concatenate([q[:, s] for s in heads], axis=0)
    q_rot = jnp.concatenate([_rope(qf[:, s], cq, saq, sbq).astype(BF16) for s in heads], axis=0)

    s = lax.dot_general(q_raw, kcmp_s[...], nt, preferred_element_type=F32) * scale
    s3 = s.reshape(hpg, TQ, nh)
    vis = (lane * CMP_STRIDE + (CMP_BLOCK - 1) <= t) & (lane < n_cmp)
    s3 = jnp.where(vis[None], s3, NEG)
    e3 = jnp.exp(s3 - jnp.max(s3, axis=-1, keepdims=True))
    p3 = e3 / jnp.sum(e3, axis=-1, keepdims=True) * vis[None].astype(F32)
    pb = p3.astype(BF16)
    o_c = jnp.dot(pb.reshape(hpg * TQ, nh), vcmp_s[...], preferred_element_type=F32).reshape(hpg, TQ, HEAD_DIM)

    psum = jnp.sum(pb.astype(F32), axis=0)
    imp = jnp.dot(psum, ov_ref[...], preferred_element_type=F32, precision=lax.Precision.HIGHEST)
    blk_t = t // SLC_BLOCK
    causal = lane <= blk_t
    forced = (lane == 0) | (lane == blk_t) | (lane == blk_t - 1)
    val = jnp.where(forced, jnp.inf, jnp.where(causal, imp, -jnp.inf))
    sel = jnp.where(causal, _top_mask(val, lane, SLC_TOPN), 0.0).astype(BF16)

    w0 = pl.multiple_of(jnp.maximum(t0 - WINDOW, 0), TQ)
    kk = kwr_s[pl.ds(w0, WSPAN), :]
    vv = vw_ref[pl.ds(w0, WSPAN), :]
    sw_s[...] = lax.dot_general(q_rot, kk, nt, preferred_element_type=F32)
    kpos = w0 + lax.broadcasted_iota(jnp.int32, (TQ, WSPAN), 1)
    wbias_s[...] = jnp.where((kpos <= t) & (t - kpos < WINDOW), 0.0, NEG)
    m_w = jnp.max(sw_s[...].reshape(hpg, TQ, WSPAN) + wbias_s[...][None], axis=-1, keepdims=True)
    pw = jnp.exp2((sw_s[...].reshape(hpg, TQ, WSPAN) + wbias_s[...][None] - m_w) * scale2)
    l_w = jnp.sum(pw, axis=-1, keepdims=True)
    o_w = jnp.dot(pw.astype(BF16).reshape(hpg * TQ, WSPAN), vv, preferred_element_type=F32)
    o_w = o_w.reshape(hpg, TQ, HEAD_DIM) / l_w

    g = jax.nn.sigmoid(gate_ref[...].astype(F32))
    o_cw = [g[:, h:h + 1] * o_c[h] + g[:, 2 * hpg + h:2 * hpg + h + 1] * o_w[h] for h in range(hpg)]

    def sel_chunk(ci, carry):
        m, l, acc = carry
        k0 = pl.multiple_of(ci * KC, KC)
        kk = ksr_s[pl.ds(k0, KC), :]
        vv = vs_ref[pl.ds(k0, KC), :]
        sc_s[...] = lax.dot_general(q_rot, kk, nt, preferred_element_type=F32)
        chosen = jnp.dot(sel, e_ref[:, pl.ds(k0, KC)], preferred_element_type=F32)
        kpos = k0 + lax.broadcasted_iota(jnp.int32, (TQ, KC), 1)
        bias_s[...] = jnp.where((chosen > 0.5) & (kpos <= t), 0.0, NEG)
        m_new = jnp.maximum(m, jnp.max(sc_s[...].reshape(hpg, TQ, KC) + bias_s[...][None], axis=-1, keepdims=True))
        a = jnp.exp2((m - m_new) * scale2)
        p = jnp.exp2((sc_s[...].reshape(hpg, TQ, KC) + bias_s[...][None] - m_new) * scale2)
        l = a * l + jnp.sum(p, axis=-1, keepdims=True)
        pv = jnp.dot(p.astype(BF16).reshape(hpg * TQ, KC), vv, preferred_element_type=F32)
        return m_new, l, a * acc + pv.reshape(hpg, TQ, HEAD_DIM)

    init = (jnp.full((hpg, TQ, 1), NEG, F32), jnp.zeros((hpg, TQ, 1), F32), jnp.zeros((hpg, TQ, HEAD_DIM), F32))
    _, l_s, acc_s = lax.fori_loop(0, (t0 + TQ + KC - 1) // KC, sel_chunk, init)
    o_s = acc_s / l_s
    for h in range(hpg):
        o_ref[:, heads[h]] = (o_cw[h] + g[:, hpg + h:hpg + h + 1] * o_s[h]).astype(BF16)


def _overlap_matrix(n_half, n_cmp, n_slc):
    c = np.arange(n_half)[:, None] * CMP_STRIDE
    s = np.arange(LANES)[None, :] * SLC_BLOCK
    ov = np.clip(np.minimum(c + CMP_BLOCK, s + SLC_BLOCK) - np.maximum(c, s), 0, None) / CMP_STRIDE
    ov[n_cmp:, :] = 0.0
    ov[:, n_slc:] = 0.0
    return ov.astype(np.float32)


def nsa_attention(proj, tabs, cmp_params, batch, seq):
    n = batch * seq
    nq = seq // TQ
    nh = seq // CMP_STRIDE
    n_cmp = (seq - CMP_BLOCK) // CMP_STRIDE + 1
    n_slc = seq // SLC_BLOCK
    assert nh == LANES and n_slc <= LANES and seq % KC == 0 and seq >= WSPAN
    c_tab, sa_tab, sb_tab = tabs
    posk, posv, w1k, w2k, w1v, w2v = cmp_params
    ov = jnp.asarray(_overlap_matrix(nh, n_cmp, n_slc))
    expand = (np.arange(seq)[None, :] // SLC_BLOCK == np.arange(LANES)[:, None])
    expand = jnp.asarray(expand.astype(np.float32), dtype=BF16)

    gw = HEADS_PER_GROUP * HEAD_DIM
    qspec = pl.BlockSpec((TQ, gw), lambda b, g, i: (b * nq + i, g))

    def kvspec(slab):
        return pl.BlockSpec((seq, HEAD_DIM), lambda b, g, i: (b, COL_KV // HEAD_DIM + slab * KV_GROUPS + g))

    gspec = pl.BlockSpec((TQ, LANES), lambda b, g, i: (b * nq + i, COL_AGATE // LANES + g))
    tq_spec = pl.BlockSpec((TQ, LANES), lambda b, g, i: (b * nq + i, 0))
    tk_spec = pl.BlockSpec((seq, LANES), lambda b, g, i: (b, 0))

    def full(a):
        return pl.BlockSpec(a.shape, lambda b, g, i: (0,) * a.ndim)

    consts = [posk, posv, w1k, w2k, w1v, w2v, ov, expand]
    return pl.pallas_call(
        _attn_kernel,
        grid=(batch, KV_GROUPS, nq),
        in_specs=[qspec] + [kvspec(s) for s in range(6)] + [gspec] + [tq_spec] * 3 + [tk_spec] * 3
                 + [full(a) for a in consts],
        out_specs=pl.BlockSpec((TQ, gw), lambda b, g, i: (b * nq + i, g)),
        out_shape=jax.ShapeDtypeStruct((n, ATTN_WIDTH), BF16),
        scratch_shapes=[pltpu.VMEM((nh, HEAD_DIM), BF16), pltpu.VMEM((nh, HEAD_DIM), BF16),
                        pltpu.VMEM((seq, HEAD_DIM), BF16), pltpu.VMEM((seq, HEAD_DIM), BF16),
                        pltpu.VMEM((seq, HEAD_DIM), F32),
                        pltpu.VMEM((HEADS_PER_GROUP * TQ, KC), F32), pltpu.VMEM((TQ, KC), F32),
                        pltpu.VMEM((HEADS_PER_GROUP * TQ, WSPAN), F32), pltpu.VMEM((TQ, WSPAN), F32)],
        compiler_params=_cparams("parallel", "parallel", "arbitrary"),
        name="nsa_attention",
    )(proj, proj, proj, proj, proj, proj, proj, proj, c_tab, sa_tab, sb_tab, c_tab, sa_tab, sb_tab, *consts)


def _hgrn_level_tables():
    t = np.arange(HC)[:, None]
    s = np.arange(HC)[None, :]
    lv = np.full((HC, HC), -1, np.int32)
    lv[(t // SUBLANES == s // SUBLANES)] = -1
    for i, m in enumerate(HG_LEVELS):
        ok = ((t // m) % 2 == 1) & (s // m == t // m - 1)
        lv[ok] = i
    tril = (s <= t).astype(np.float32)
    return lv, tril


def _hgrn_head(q_b, f_b, i_b, g_b, lb, ng, lv, tril, st_ref, b_ref):
    nt = (((1,), (1,)), ((), ()))
    tn = (((0,), (0,)), ((), ()))
    f = lb + (1.0 - lb) * jax.nn.sigmoid(f_b.astype(F32))
    logf = jnp.log(f)
    kk = 1.0 - f
    q = q_b.astype(F32)
    v = i_b.astype(F32)
    vb = i_b
    b = jnp.dot(tril, logf, preferred_element_type=F32, precision=lax.Precision.HIGHEST)
    b_ref[...] = b
    row = lax.broadcasted_iota(jnp.int32, (HC, HG_DK), 0)

    st = st_ref[...]
    o = lax.dot_general((q * jnp.exp(b)).astype(BF16), st.astype(BF16), nt, preferred_element_type=F32)

    a = jnp.zeros((HC, HC), F32)
    for i, m in enumerate(HG_LEVELS):
        ref_rows = [jnp.broadcast_to(b_ref[pl.ds((2 * j + 1) * m - 1, 1), :], (2 * m, HG_DK))
                    for j in range(HC // (2 * m))]
        d = b - jnp.concatenate(ref_rows, axis=0)
        odd = (row // m) % 2 == 1
        x = (jnp.where(odd, q, kk) * jnp.exp(-jnp.abs(d))).astype(BF16)
        am = lax.dot_general(x, x, nt, preferred_element_type=F32)
        a = jnp.where(lv == i, am, a)
    o = o + jnp.dot(a.astype(BF16), vb, preferred_element_type=F32)

    nb = HC // SUBLANES
    b3 = b.reshape(nb, SUBLANES, HG_DK)
    q3 = q.reshape(nb, SUBLANES, HG_DK)
    k3 = kk.reshape(nb, SUBLANES, HG_DK)
    v3 = v.reshape(nb, SUBLANES, HG_DV)
    r3 = lax.broadcasted_iota(jnp.int32, (nb, SUBLANES, HG_DK), 1)
    od = jnp.zeros((nb, SUBLANES, HG_DV), F32)
    for j in range(SUBLANES):
        bj = jnp.broadcast_to(b3[:, j:j + 1, :], b3.shape)
        kj = jnp.broadcast_to(k3[:, j:j + 1, :], b3.shape)
        vj = jnp.broadcast_to(v3[:, j:j + 1, :], b3.shape)
        dec = jnp.where(r3 >= j, jnp.exp(jnp.minimum(b3 - bj, 0.0)), 0.0)
        od = od + jnp.sum(q3 * kj * dec, axis=-1, keepdims=True) * vj
    o = o + od.reshape(HC, HG_DV)

    b_last = b_ref[pl.ds(HC - 1, 1), :]
    kd = (kk * jnp.exp(b_last - b)).astype(BF16)
    st_ref[...] = st * jnp.exp(b_last) + lax.dot_general(vb, kd, tn, preferred_element_type=F32)

    o = o * lax.rsqrt(jnp.mean(o * o, axis=-1, keepdims=True) + RMS_EPS) * ng
    gg = g_b.astype(F32)
    return (o * (gg * jax.nn.sigmoid(gg))).astype(BF16)


def _hgrn_kernel(q_ref, f_ref, i_ref, g_ref, lb_ref, ng_ref, lv_ref, tril_ref, o_ref, st_s, b_s):
    @pl.when(pl.program_id(2) == 0)
    def _():
        st_s[...] = jnp.zeros_like(st_s)

    lv, tril = lv_ref[...], tril_ref[...]
    lb, ng = lb_ref[0], ng_ref[0]
    for h in range(HG_STEP_HEADS):
        s = slice(h * HG_DK, (h + 1) * HG_DK)
        o_ref[:, s] = _hgrn_head(q_ref[:, s], f_ref[:, s], i_ref[:, s], g_ref[:, s], lb[:, s], ng[:, s],
                                 lv, tril, st_s.at[h], b_s.at[h])


def hgrn2(proj, lb, norm_g, batch, seq):
    n = batch * seq
    nc = seq // HC
    hs = HG_STEP_HEADS
    wide = hs * HG_DK
    lv, tril = _hgrn_level_tables()
    base = COL_HG // wide

    def slab(k):
        return pl.BlockSpec((HC, wide), lambda b, h, c: (b * nc + c, base + k * (HG_HEADS // hs) + h))

    vec = pl.BlockSpec((1, 1, wide), lambda b, h, c: (h, 0, 0))
    const = pl.BlockSpec((HC, HC), lambda b, h, c: (0, 0))
    return pl.pallas_call(
        _hgrn_kernel,
        grid=(batch, HG_HEADS // hs, nc),
        in_specs=[slab(0), slab(1), slab(2), slab(3), vec, vec, const, const],
        out_specs=pl.BlockSpec((HC, wide), lambda b, h, c: (b * nc + c, h)),
        out_shape=jax.ShapeDtypeStruct((n, HG_WIDTH), BF16),
        scratch_shapes=[pltpu.VMEM((hs, HG_DV, HG_DK), F32), pltpu.VMEM((hs, HC, HG_DK), F32)],
        compiler_params=_cparams("parallel", "parallel", "arbitrary"),
        name="hgrn2",
    )(proj, proj, proj, proj, lb.reshape(HG_HEADS // hs, 1, wide), norm_g.reshape(HG_HEADS // hs, 1, wide),
      jnp.asarray(lv), jnp.asarray(tril))


def _pool_kernel(p_ref, w_ref, sc_ref, o_ref):
    gi = pl.program_id(1)
    x = p_ref[...].astype(F32)
    t = lax.broadcasted_iota(jnp.int32, x.shape, 0)
    acc = x
    for k in range(len(POOL_WINDOWS)):
        sh = 1 << k
        nxt = acc + jnp.where(t >= sh, pltpu.roll(acc, sh, 0), 0.0)
        acc = jnp.where(k <= gi, nxt, acc)
    width = lax.shift_left(jnp.int32(2), gi)
    cnt = jnp.minimum(t + 1, width).astype(F32)
    mixed = acc / cnt - x
    y = jnp.dot(mixed.astype(BF16), w_ref[0], preferred_element_type=F32) * sc_ref[0]
    o_ref[...] = y.astype(BF16)


def multiscale_pool(proj, w_pool, scale, batch, seq):
    n = batch * seq
    ng = len(POOL_WINDOWS)
    base = COL_POOL // POOL_GROUP
    return pl.pallas_call(
        _pool_kernel,
        grid=(batch, ng),
        in_specs=[pl.BlockSpec((seq, POOL_GROUP), lambda b, g: (b, base + g)),
                  pl.BlockSpec((1, POOL_GROUP, POOL_GROUP), lambda b, g: (g, 0, 0)),
                  pl.BlockSpec((1, 1, POOL_GROUP), lambda b, g: (g, 0, 0))],
        out_specs=pl.BlockSpec((seq, POOL_GROUP), lambda b, g: (b, g)),
        out_shape=jax.ShapeDtypeStruct((n, POOL_WIDTH), BF16),
        compiler_params=_cparams("parallel", "parallel"),
        name="multiscale_pool",
    )(proj, w_pool, scale.reshape(ng, 1, POOL_GROUP))


def _merge_kernel(ya_ref, yb_ref, yc_ref, wa_ref, wb_ref, wc_ref, ga_ref, gb_ref, gc_ref, o_ref):
    def branch(y_ref, w_ref, g_ref):
        up = jnp.dot(y_ref[...], w_ref[...], preferred_element_type=F32)
        return jax.nn.sigmoid(g_ref[...].astype(F32)) * up

    o_ref[...] = (branch(ya_ref, wa_ref, ga_ref) + branch(yb_ref, wb_ref, gb_ref)
                  + branch(yc_ref, wc_ref, gc_ref)).astype(o_ref.dtype)


def merge_branches(ya, yb, yc, wa, wb, wc, proj, d_model, tm=1024, tn=512):
    n = ya.shape[0]
    tm = min(tm, n)
    gbase = COL_MG // tn
    step = d_model // tn

    def y_spec(a):
        return pl.BlockSpec((tm, a.shape[1]), lambda i, j: (i, 0))

    def w_spec(a):
        return pl.BlockSpec((a.shape[0], tn), lambda i, j: (0, j))

    def g_spec(k):
        return pl.BlockSpec((tm, tn), lambda i, j: (i, gbase + k * step + j))

    return pl.pallas_call(
        _merge_kernel,
        grid=(n // tm, d_model // tn),
        in_specs=[y_spec(ya), y_spec(yb), y_spec(yc), w_spec(wa), w_spec(wb), w_spec(wc),
                  g_spec(0), g_spec(1), g_spec(2)],
        out_specs=pl.BlockSpec((tm, tn), lambda i, j: (i, j)),
        out_shape=jax.ShapeDtypeStruct((n, d_model), BF16),
        compiler_params=_cparams("parallel", "parallel"),
        name="merge_branches",
    )(ya, yb, yc, wa, wb, wc, proj, proj, proj)


def _router_kernel(x_ref, w_ref, b_ref, o_ref):
    logits = jnp.dot(x_ref[...], w_ref[...], preferred_element_type=F32)
    scores = jax.nn.sigmoid(logits)
    lane = lax.broadcasted_iota(jnp.int32, scores.shape, 1)
    real = lane < N_EXPERTS
    sel = _top_mask(jnp.where(real, scores + b_ref[...], -jnp.inf), lane, TOP_K)
    w = jnp.where(real, sel * scores, 0.0)
    o_ref[...] = w / jnp.sum(w, axis=-1, keepdims=True) * ROUTE_SCALE


def moe_router(x, router_w, router_b, tm=512):
    n, d = x.shape
    tm = min(tm, n)
    w = jnp.zeros((d, LANES), BF16).at[:, :N_EXPERTS].set(router_w.astype(BF16))
    b = jnp.zeros((1, LANES), F32).at[0, :N_EXPERTS].set(router_b.astype(F32))
    return pl.pallas_call(
        _router_kernel,
        grid=(n // tm,),
        in_specs=[pl.BlockSpec((tm, d), lambda i: (i, 0)), pl.BlockSpec((d, LANES), lambda i: (0, 0)),
                  pl.BlockSpec((1, LANES), lambda i: (0, 0))],
        out_specs=pl.BlockSpec((tm, LANES), lambda i: (i, 0)),
        out_shape=jax.ShapeDtypeStruct((n, LANES), F32),
        compiler_params=_cparams("parallel"),
        name="moe_router",
    )(x, w, b)


def _moe_kernel(ec, x_ref, w1_ref, w3_ref, w2_ref, s1_ref, s3_ref, s2_ref, gate_ref, ex_ref, o_ref, acc_s):
    j = pl.program_id(1)
    last = pl.num_programs(1) - 1
    x = x_ref[...]
    d = x.shape[1]

    def ffn(w1, w3, w2, gexp):
        h1 = jnp.dot(x, w1, preferred_element_type=F32)
        h3 = jnp.dot(x, w3, preferred_element_type=F32)
        hd = h1 * jax.nn.sigmoid(h1) * h3
        if gexp is not None:
            hd = hd * gexp
        return jnp.dot(hd.astype(BF16), w2, preferred_element_type=F32)

    @pl.when(j == 0)
    def _shared():
        acc_s[...] = ffn(s1_ref[...], s3_ref[...], s2_ref[...], None)

    @pl.when(j > 0)
    def _routed():
        w1 = jnp.concatenate([w1_ref[e] for e in range(ec)], axis=1)
        w3 = jnp.concatenate([w3_ref[e] for e in range(ec)], axis=1)
        w2 = w2_ref[...].reshape(ec * D_EXPERT, d)
        g = gate_ref[...]
        g_hi = g.astype(BF16)
        g_lo = (g - g_hi.astype(F32)).astype(BF16)
        ex = ex_ref[...]
        gexp = jnp.dot(g_hi, ex, preferred_element_type=F32) + jnp.dot(g_lo, ex, preferred_element_type=F32)
        acc_s[...] += ffn(w1, w3, w2, gexp)

    @pl.when(j == last)
    def _():
        o_ref[...] = acc_s[...].astype(o_ref.dtype)


def moe_experts(x, gate, w1, w3, w2, ws1, ws3, ws2, layer, tm=512, ec=2):
    n, d = x.shape
    tm = min(tm, n)
    th = ec * D_EXPERT
    steps = N_EXPERTS // ec
    expand = (np.arange(N_EXPERTS * D_EXPERT)[None, :] // D_EXPERT == np.arange(LANES)[:, None])
    expand = jnp.asarray(expand.astype(np.float32), dtype=BF16)
    once = pl.Buffered(1)
    routed = lambda i, j: (layer, jnp.maximum(j - 1, 0), 0, 0)
    return pl.pallas_call(
        functools.partial(_moe_kernel, ec),
        grid=(n // tm, steps + 1),
        in_specs=[pl.BlockSpec((tm, d), lambda i, j: (i, 0), pipeline_mode=once),
                  pl.BlockSpec((None, ec, d, D_EXPERT), routed),
                  pl.BlockSpec((None, ec, d, D_EXPERT), routed),
                  pl.BlockSpec((None, ec, D_EXPERT, d), routed),
                  pl.BlockSpec((d, D_SHARED), lambda i, j: (0, 0), pipeline_mode=once),
                  pl.BlockSpec((d, D_SHARED), lambda i, j: (0, 0), pipeline_mode=once),
                  pl.BlockSpec((D_SHARED, d), lambda i, j: (0, 0), pipeline_mode=once),
                  pl.BlockSpec((tm, LANES), lambda i, j: (i, 0)),
                  pl.BlockSpec((LANES, th), lambda i, j: (0, jnp.maximum(j - 1, 0)))],
        out_specs=pl.BlockSpec((tm, d), lambda i, j: (i, 0)),
        out_shape=jax.ShapeDtypeStruct((n, d), BF16),
        scratch_shapes=[pltpu.VMEM((tm, d), F32)],
        compiler_params=_cparams("parallel", "arbitrary"),
        name="moe_experts",
    )(x, w1, w3, w2, ws1, ws3, ws2, gate, expand)


def _pack_w_in(w, d_model):
    sizes = (ATTN_WIDTH, KV_WIDTH, KV_WIDTH, KV_WIDTH, KV_WIDTH, KV_WIDTH, KV_WIDTH, 3 * ATTN_HEADS,
             HG_KWIDTH, HG_KWIDTH, HG_WIDTH, HG_WIDTH, POOL_WIDTH, d_model, d_model, d_model)
    offs = np.concatenate([[0], np.cumsum(sizes)])
    seg = lambda i: w[:, offs[i]:offs[i + 1]]
    ag = seg(7).reshape(d_model, 3, KV_GROUPS, HEADS_PER_GROUP)
    zpad = lambda c: jnp.zeros((d_model, c), w.dtype)
    gates = []
    for g in range(KV_GROUPS):
        gates += [ag[:, :, g, :].reshape(d_model, 3 * HEADS_PER_GROUP), zpad(LANES - 3 * HEADS_PER_GROUP)]
    parts = ([seg(i) for i in range(7)] + gates + [zpad(ATTN_SLAB - COL_AGATE - KV_GROUPS * LANES)]
             + [seg(i) for i in range(8, 16)])
    return jnp.concatenate(parts, axis=1).astype(BF16)


def _hgrn_lower_bounds(logits):
    lbs = jnp.cumsum(jax.nn.softmax(logits.astype(F32), axis=0), axis=0)
    return lbs - lbs[0:1]


def kernel(x, positions, ln_in_g, ln_in_b, w_in, cmp_pos_k, cmp_pos_v, cmp_w1_k, cmp_w2_k, cmp_w1_v, cmp_w2_v,
           hg_lb_logits, hg_norm_g, pool_w, pool_scale, w_up_attn, w_up_hg, w_up_pool, w_o, ln1_g, ln1_b,
           router_w, router_b, w1, w3, w2, ws1, ws3, ws2, ln2_g, ln2_b):
    batch, seq, d = x.shape
    n = batch * seq
    depth = w_in.shape[0]
    lbs = _hgrn_lower_bounds(hg_lb_logits)
    tabs = rope_tables(positions)
    w1b, w3b, w2b = w1.astype(BF16), w3.astype(BF16), w2.astype(BF16)
    h32, h16 = layer_norm_rows(x.reshape(n, d), ln_in_g, ln_in_b)
    for l in range(depth):
        proj = matmul(h16, _pack_w_in(w_in[l], d), BF16)
        cmp_params = (cmp_pos_k[l], cmp_pos_v[l],
                      cmp_w1_k[l].reshape(CMP_BLOCK, HEAD_DIM, HEAD_DIM).astype(BF16), cmp_w2_k[l].astype(BF16),
                      cmp_w1_v[l].reshape(CMP_BLOCK, HEAD_DIM, HEAD_DIM).astype(BF16), cmp_w2_v[l].astype(BF16))
        ya = nsa_attention(proj, tabs, cmp_params, batch, seq)
        yb = hgrn2(proj, lbs[l], hg_norm_g[l], batch, seq)
        yc = multiscale_pool(proj, pool_w[l].astype(BF16), pool_scale[l], batch, seq)
        merged = merge_branches(ya, yb, yc, w_up_attn[l].astype(BF16), w_up_hg[l].astype(BF16),
                                w_up_pool[l].astype(BF16), proj, d)
        mix = matmul(merged, w_o[l].astype(BF16), BF16)
        h32, h16 = layer_norm_rows(mix, ln1_g[l], ln1_b[l], res=h32, alpha=DN_ALPHA)
        gate = moe_router(h16, router_w[l], router_b[l])
        ffn = moe_experts(h16, gate, w1b, w3b, w2b, ws1[l].astype(BF16), ws3[l].astype(BF16),
                          ws2[l].astype(BF16), l)
        h32, h16 = layer_norm_rows(ffn, ln2_g[l], ln2_b[l], res=h32, alpha=DN_ALPHA)
    return h32.reshape(batch, seq, d)
```

```python
import functools

import numpy as np
import jax
import jax.numpy as jnp
from jax import lax
from jax.experimental import pallas as pl
from jax.experimental.pallas import tpu as pltpu

F32 = jnp.float32
BF16 = jnp.bfloat16

DEPTH = 2
HEAD_DIM = 128
ATTN_HEADS = 16
KV_GROUPS = 2
HEADS_PER_GROUP = ATTN_HEADS // KV_GROUPS
ATTN_WIDTH = ATTN_HEADS * HEAD_DIM
KV_WIDTH = KV_GROUPS * HEAD_DIM
ROPE_DIM = HEAD_DIM // 4
ROPE_THETA = 500000.0
CMP_BLOCK = 32
CMP_STRIDE = 16
SLC_BLOCK = 32
SLC_TOPN = 8
WINDOW = 512
HG_HEADS = 8
HG_DK = 128
HG_DV = 128
HG_KWIDTH = HG_HEADS * HG_DK
HG_WIDTH = HG_HEADS * HG_DV
POOL_WINDOWS = (2, 4, 8, 16)
POOL_GROUP = 256
POOL_WIDTH = POOL_GROUP * len(POOL_WINDOWS)
N_EXPERTS = 64
TOP_K = 8
D_EXPERT = 128
D_SHARED = 256
ROUTE_SCALE = 2.5
DN_ALPHA = (2.0 * DEPTH) ** 0.25
LN_EPS = 1e-5
RMS_EPS = 1e-6
NEG = -1e30

LANES = 128
SUBLANES = 8
VMEM_LIMIT = 56 * 1024 * 1024

COL_KV = ATTN_WIDTH
ATTN_COLS = ATTN_WIDTH + 6 * KV_WIDTH
COL_HG = 0
COL_POOL = COL_HG + 4 * HG_WIDTH
COL_MG = COL_POOL + POOL_WIDTH

TQ = 128
KC = 512
WSPAN = WINDOW + TQ
HC = 128
HG_LEVELS = (64, 32, 16, 8)
HG_STEP_HEADS = 8


def _cparams(*sem):
    return pltpu.CompilerParams(dimension_semantics=sem, vmem_limit_bytes=VMEM_LIMIT)


def _ln_kernel(alpha, has_res, *refs):
    if has_res:
        x_ref, r_ref, g_ref, b_ref, o32_ref, o16_ref = refs
        x = alpha * r_ref[...] + x_ref[...].astype(F32)
    else:
        x_ref, g_ref, b_ref, o32_ref, o16_ref = refs
        x = x_ref[...]
    mu = jnp.mean(x, axis=-1, keepdims=True)
    xc = x - mu
    var = jnp.mean(xc * xc, axis=-1, keepdims=True)
    y = xc * lax.rsqrt(var + LN_EPS) * g_ref[...] + b_ref[...]
    o32_ref[...] = y
    o16_ref[...] = y.astype(BF16)


def layer_norm_rows(x, g, b, res=None, alpha=1.0, tm=256):
    n, d = x.shape
    row = pl.BlockSpec((tm, d), lambda i: (i, 0))
    vec = pl.BlockSpec((1, d), lambda i: (0, 0))
    ins = [x] + ([res] if res is not None else []) + [g.reshape(1, d), b.reshape(1, d)]
    specs = [row] + ([row] if res is not None else []) + [vec, vec]
    return pl.pallas_call(
        functools.partial(_ln_kernel, alpha, res is not None),
        grid=(n // tm,),
        in_specs=specs,
        out_specs=[row, row],
        out_shape=[jax.ShapeDtypeStruct((n, d), F32), jax.ShapeDtypeStruct((n, d), BF16)],
        compiler_params=_cparams("parallel"),
        name="layer_norm",
    )(*ins)


def _mm_kernel(x_ref, w_ref, o_ref):
    o_ref[...] = jnp.dot(x_ref[...], w_ref[...], preferred_element_type=F32).astype(o_ref.dtype)


def matmul(x, w, out_dtype, tm=1024, tn=1024, layer=None, cols=None):
    n, k = x.shape
    m = cols if cols is not None else w.shape[-1]
    tm, tn = min(tm, n), min(tn, m)
    assert m % tn == 0
    if layer is None:
        w_spec = pl.BlockSpec((k, tn), lambda i, j: (0, j))
    else:
        w_spec = pl.BlockSpec((None, k, tn), lambda i, j: (layer, 0, j))
    return pl.pallas_call(
        _mm_kernel,
        grid=(n // tm, m // tn),
        in_specs=[pl.BlockSpec((tm, k), lambda i, j: (i, 0)), w_spec],
        out_specs=pl.BlockSpec((tm, tn), lambda i, j: (i, j)),
        out_shape=jax.ShapeDtypeStruct((n, m), out_dtype),
        compiler_params=_cparams("parallel", "parallel"),
        name="matmul",
    )(x, w)


def _rope_table_kernel(pos_ref, inv_ref, c_ref, sa_ref, sb_ref):
    ang = pos_ref[...].astype(F32) * inv_ref[...]
    lane = lax.broadcasted_iota(jnp.int32, ang.shape, 1)
    sn = jnp.sin(ang)
    c_ref[...] = jnp.cos(ang)
    sa_ref[...] = jnp.where(lane < ROPE_DIM // 2, -sn, 0.0)
    sb_ref[...] = jnp.where((lane >= ROPE_DIM // 2) & (lane < ROPE_DIM), sn, 0.0)


def rope_tables(positions, tm=1024):
    n = positions.size
    half = ROPE_DIM // 2
    inv = ROPE_THETA ** (-np.arange(half, dtype=np.float32) * 2.0 / ROPE_DIM)
    inv_full = np.zeros((1, LANES), np.float32)
    inv_full[0, :half] = inv
    inv_full[0, half:ROPE_DIM] = inv
    tm = min(tm, n)
    out = jax.ShapeDtypeStruct((n, LANES), F32)
    spec = pl.BlockSpec((tm, LANES), lambda i: (i, 0))
    return pl.pallas_call(
        _rope_table_kernel,
        grid=(n // tm,),
        in_specs=[pl.BlockSpec((tm, 1), lambda i: (i, 0)), pl.BlockSpec((1, LANES), lambda i: (0, 0))],
        out_specs=[spec, spec, spec],
        out_shape=[out, out, out],
        compiler_params=_cparams("parallel"),
        name="rope_tables",
    )(positions.reshape(n, 1), jnp.asarray(inv_full))


def _rope(x, c, sa, sb):
    return x * c + pltpu.roll(x, LANES - ROPE_DIM // 2, 1) * sa + pltpu.roll(x, ROPE_DIM // 2, 1) * sb


def _gelu_tanh(x):
    return 0.5 * x * (1.0 + jnp.tanh(0.7978845608028654 * (x + 0.044715 * x * x * x)))


def _top_mask(val, lane, n_pick, axis=-1):
    sel = jnp.zeros(val.shape, F32)
    for _ in range(n_pick):
        m = jnp.max(val, axis=axis, keepdims=True)
        idx = jnp.min(jnp.where(val == m, lane, LANES), axis=axis, keepdims=True)
        pick = lane == idx
        sel = jnp.where(pick, 1.0, sel)
        val = jnp.where(pick, -jnp.inf, val)
    return sel


def _attn_kernel(q_ref, kc_ref, vc_ref, ks_ref, vs_ref, kw_ref, vw_ref, gate_ref,
                 cq_ref, saq_ref, sbq_ref, ck_ref, sak_ref, sbk_ref,
                 posk_ref, posv_ref, w1k_ref, w2k_ref, w1v_ref, w2v_ref, ov_ref, e_ref,
                 o_ref, kcmp_s, vcmp_s, ksr_s, kwr_s, tmp_s, sc_s, bias_s, sw_s, wbias_s):
    seq = ks_ref.shape[0]
    nh = seq // CMP_STRIDE
    n_cmp = (seq - CMP_BLOCK) // CMP_STRIDE + 1
    hpg = HEADS_PER_GROUP
    scale = HEAD_DIM ** -0.5
    scale2 = scale * 1.4426950408889634
    qt = pl.program_id(2)
    nt = (((1,), (1,)), ((), ()))

    @pl.when(qt == 0)
    def _per_sequence():
        def compress(t_ref, pos_ref, w1_ref, w2_ref, out_s):
            tmp_s[...] = t_ref[...].astype(F32)
            first = jnp.zeros((nh, HEAD_DIM), F32)
            second = jnp.zeros((nh, HEAD_DIM), F32)
            for j in range(CMP_BLOCK):
                x = tmp_s[pl.ds(j % CMP_STRIDE, nh, stride=CMP_STRIDE), :] + pos_ref[j:j + 1, :]
                p = jnp.dot(x.astype(BF16), w1_ref[j], preferred_element_type=F32)
                if j < CMP_STRIDE:
                    first = first + p
                else:
                    second = second + p
            pre = first + pltpu.roll(second, nh - 1, 0)
            hid = _gelu_tanh(pre).astype(BF16)
            out_s[...] = jnp.dot(hid, w2_ref[...], preferred_element_type=F32).astype(BF16)

        compress(kc_ref, posk_ref, w1k_ref, w2k_ref, kcmp_s)
        compress(vc_ref, posv_ref, w1v_ref, w2v_ref, vcmp_s)
        ck, sak, sbk = ck_ref[...], sak_ref[...], sbk_ref[...]
        ksr_s[...] = _rope(ks_ref[...].astype(F32), ck, sak, sbk).astype(BF16)
        kwr_s[...] = _rope(kw_ref[...].astype(F32), ck, sak, sbk).astype(BF16)

    t0 = qt * TQ
    t = t0 + lax.broadcasted_iota(jnp.int32, (TQ, 1), 0)
    lane = lax.broadcasted_iota(jnp.int32, (TQ, LANES), 1)
    q = q_ref[...]
    qf = q.astype(F32)
    cq, saq, sbq = cq_ref[...], saq_ref[...], sbq_ref[...]
    heads = [slice(h * HEAD_DIM, (h + 1) * HEAD_DIM) for h in range(hpg)]
    q_raw = jnp.concatenate([q[:, s] for s in heads], axis=0)
    q_rot = jnp.concatenate([_rope(qf[:, s], cq, saq, sbq).astype(BF16) for s in heads], axis=0)

    s = lax.dot_general(q_raw, kcmp_s[...], nt, preferred_element_type=F32) * scale
    s3 = s.reshape(hpg, TQ, nh)
    vis = (lane * CMP_STRIDE + (CMP_BLOCK - 1) <= t) & (lane < n_cmp)
    s3 = jnp.where(vis[None], s3, NEG)
    e3 = jnp.exp(s3 - jnp.max(s3, axis=-1, keepdims=True))
    p3 = e3 / jnp.sum(e3, axis=-1, keepdims=True) * vis[None].astype(F32)
    pb = p3.astype(BF16)
    o_c = jnp.dot(pb.reshape(hpg * TQ, nh), vcmp_s[...], preferred_element_type=F32).reshape(hpg, TQ, HEAD_DIM)

    psum = jnp.sum(pb.astype(F32), axis=0)
    imp = lax.dot_general(ov_ref[...], psum, nt, preferred_element_type=F32, precision=lax.Precision.HIGHEST)
    n_slc = imp.shape[0]
    blk = lax.broadcasted_iota(jnp.int32, (n_slc, TQ), 0)
    blk_t = (t0 + lax.broadcasted_iota(jnp.int32, (n_slc, TQ), 1)) // SLC_BLOCK
    causal = blk <= blk_t
    forced = (blk == 0) | (blk == blk_t) | (blk == blk_t - 1)
    val = jnp.where(forced, jnp.inf, jnp.where(causal, imp, -jnp.inf))
    sel = jnp.where(causal, _top_mask(val, blk, SLC_TOPN, axis=0), 0.0).astype(BF16)

    w0 = pl.multiple_of(jnp.maximum(t0 - WINDOW, 0), TQ)
    kk = kwr_s[pl.ds(w0, WSPAN), :]
    vv = vw_ref[pl.ds(w0, WSPAN), :]
    sw_s[...] = lax.dot_general(q_rot, kk, nt, preferred_element_type=F32)
    kpos = w0 + lax.broadcasted_iota(jnp.int32, (TQ, WSPAN), 1)
    wbias_s[...] = jnp.where((kpos <= t) & (t - kpos < WINDOW), 0.0, NEG)
    m_w = jnp.max(sw_s[...].reshape(hpg, TQ, WSPAN) + wbias_s[...][None], axis=-1, keepdims=True)
    pw = jnp.exp2((sw_s[...].reshape(hpg, TQ, WSPAN) + wbias_s[...][None] - m_w) * scale2)
    l_w = jnp.sum(pw, axis=-1, keepdims=True)
    o_w = jnp.dot(pw.astype(BF16).reshape(hpg * TQ, WSPAN), vv, preferred_element_type=F32)
    o_w = o_w.reshape(hpg, TQ, HEAD_DIM) / l_w

    g = jax.nn.sigmoid(gate_ref[...].astype(F32))
    o_cw = [g[:, h:h + 1] * o_c[h] + g[:, 2 * hpg + h:2 * hpg + h + 1] * o_w[h] for h in range(hpg)]

    def sel_chunk(ci, carry):
        m, l, acc = carry
        k0 = pl.multiple_of(ci * KC, KC)
        kk = ksr_s[pl.ds(k0, KC), :]
        vv = vs_ref[pl.ds(k0, KC), :]
        sc_s[...] = lax.dot_general(q_rot, kk, nt, preferred_element_type=F32)
        chosen = lax.dot_general(sel, e_ref[:, pl.ds(k0, KC)], (((0,), (0,)), ((), ())),
                                 preferred_element_type=F32)
        kpos = k0 + lax.broadcasted_iota(jnp.int32, (TQ, KC), 1)
        bias_s[...] = jnp.where((chosen > 0.5) & (kpos <= t), 0.0, NEG)
        m_new = jnp.maximum(m, jnp.max(sc_s[...].reshape(hpg, TQ, KC) + bias_s[...][None], axis=-1, keepdims=True))
        a = jnp.exp2((m - m_new) * scale2)
        p = jnp.exp2((sc_s[...].reshape(hpg, TQ, KC) + bias_s[...][None] - m_new) * scale2)
        l = a * l + jnp.sum(p, axis=-1, keepdims=True)
        pv = jnp.dot(p.astype(BF16).reshape(hpg * TQ, KC), vv, preferred_element_type=F32)
        return m_new, l, a * acc + pv.reshape(hpg, TQ, HEAD_DIM)

    init = (jnp.full((hpg, TQ, 1), NEG, F32), jnp.zeros((hpg, TQ, 1), F32), jnp.zeros((hpg, TQ, HEAD_DIM), F32))
    _, l_s, acc_s = lax.fori_loop(0, (t0 + TQ + KC - 1) // KC, sel_chunk, init)
    o_s = acc_s / l_s
    for h in range(hpg):
        o_ref[:, heads[h]] = (o_cw[h] + g[:, hpg + h:hpg + h + 1] * o_s[h]).astype(BF16)


def _overlap_matrix(n_half, n_cmp, n_slc):
    c = np.arange(n_half)[None, :] * CMP_STRIDE
    s = np.arange(n_slc)[:, None] * SLC_BLOCK
    ov = np.clip(np.minimum(c + CMP_BLOCK, s + SLC_BLOCK) - np.maximum(c, s), 0, None) / CMP_STRIDE
    ov[:, n_cmp:] = 0.0
    return ov.astype(np.float32)


def nsa_attention(proj, proj_g, tabs, cmp_params, batch, seq):
    n = batch * seq
    nq = seq // TQ
    nh = seq // CMP_STRIDE
    n_cmp = (seq - CMP_BLOCK) // CMP_STRIDE + 1
    n_slc = seq // SLC_BLOCK
    assert nh == LANES and n_slc <= LANES and seq % KC == 0 and seq >= WSPAN
    c_tab, sa_tab, sb_tab = tabs
    posk, posv, w1k, w2k, w1v, w2v = cmp_params
    ov = jnp.asarray(_overlap_matrix(nh, n_cmp, n_slc))
    expand = (np.arange(seq)[None, :] // SLC_BLOCK == np.arange(n_slc)[:, None])
    expand = jnp.asarray(expand.astype(np.float32), dtype=BF16)

    gw = HEADS_PER_GROUP * HEAD_DIM
    qspec = pl.BlockSpec((TQ, gw), lambda b, g, i: (b * nq + i, g))

    def kvspec(slab):
        return pl.BlockSpec((seq, HEAD_DIM), lambda b, g, i: (b, COL_KV // HEAD_DIM + slab * KV_GROUPS + g))

    gspec = pl.BlockSpec((TQ, LANES), lambda b, g, i: (b * nq + i, g))
    tq_spec = pl.BlockSpec((TQ, LANES), lambda b, g, i: (b * nq + i, 0))
    tk_spec = pl.BlockSpec((seq, LANES), lambda b, g, i: (b, 0))

    def full(a):
        return pl.BlockSpec(a.shape, lambda b, g, i: (0,) * a.ndim)

    consts = [posk, posv, w1k, w2k, w1v, w2v, ov, expand]
    return pl.pallas_call(
        _attn_kernel,
        grid=(batch, KV_GROUPS, nq),
        in_specs=[qspec] + [kvspec(s) for s in range(6)] + [gspec] + [tq_spec] * 3 + [tk_spec] * 3
                 + [full(a) for a in consts],
        out_specs=pl.BlockSpec((TQ, gw), lambda b, g, i: (b * nq + i, g)),
        out_shape=jax.ShapeDtypeStruct((n, ATTN_WIDTH), BF16),
        scratch_shapes=[pltpu.VMEM((nh, HEAD_DIM), BF16), pltpu.VMEM((nh, HEAD_DIM), BF16),
                        pltpu.VMEM((seq, HEAD_DIM), BF16), pltpu.VMEM((seq, HEAD_DIM), BF16),
                        pltpu.VMEM((seq, HEAD_DIM), F32),
                        pltpu.VMEM((HEADS_PER_GROUP * TQ, KC), F32), pltpu.VMEM((TQ, KC), F32),
                        pltpu.VMEM((HEADS_PER_GROUP * TQ, WSPAN), F32), pltpu.VMEM((TQ, WSPAN), F32)],
        compiler_params=_cparams("parallel", "parallel", "arbitrary"),
        name="nsa_attention",
    )(proj, proj, proj, proj, proj, proj, proj, proj_g, c_tab, sa_tab, sb_tab, c_tab, sa_tab, sb_tab, *consts)


def _hgrn_level_tables():
    t = np.arange(HC)[:, None]
    s = np.arange(HC)[None, :]
    lv = np.full((HC, HC), -1, np.int32)
    lv[(t // SUBLANES == s // SUBLANES)] = -1
    for i, m in enumerate(HG_LEVELS):
        ok = ((t // m) % 2 == 1) & (s // m == t // m - 1)
        lv[ok] = i
    tril = (s <= t).astype(np.float32)
    return lv, tril


def _hgrn_head(q_b, f_b, i_b, g_b, lb, ng, lv, tril, st_ref, b_ref):
    nt = (((1,), (1,)), ((), ()))
    tn = (((0,), (0,)), ((), ()))
    f = lb + (1.0 - lb) * jax.nn.sigmoid(f_b.astype(F32))
    logf = jnp.log(f)
    kk = 1.0 - f
    q = q_b.astype(F32)
    v = i_b.astype(F32)
    vb = i_b
    b = jnp.dot(tril, logf, preferred_element_type=F32, precision=lax.Precision.HIGHEST)
    b_ref[...] = b
    row = lax.broadcasted_iota(jnp.int32, (HC, HG_DK), 0)

    st = st_ref[...]
    o = lax.dot_general((q * jnp.exp(b)).astype(BF16), st.astype(BF16), nt, preferred_element_type=F32)

    a = jnp.zeros((HC, HC), F32)
    for i, m in enumerate(HG_LEVELS):
        ref_rows = [jnp.broadcast_to(b_ref[pl.ds((2 * j + 1) * m - 1, 1), :], (2 * m, HG_DK))
                    for j in range(HC // (2 * m))]
        d = b - jnp.concatenate(ref_rows, axis=0)
        odd = (row // m) % 2 == 1
        x = (jnp.where(odd, q, kk) * jnp.exp(-jnp.abs(d))).astype(BF16)
        am = lax.dot_general(x, x, nt, preferred_element_type=F32)
        a = jnp.where(lv == i, am, a)
    o = o + jnp.dot(a.astype(BF16), vb, preferred_element_type=F32)

    nb = HC // SUBLANES
    b3 = b.reshape(nb, SUBLANES, HG_DK)
    q3 = q.reshape(nb, SUBLANES, HG_DK)
    k3 = kk.reshape(nb, SUBLANES, HG_DK)
    v3 = v.reshape(nb, SUBLANES, HG_DV)
    r3 = lax.broadcasted_iota(jnp.int32, (nb, SUBLANES, HG_DK), 1)
    od = jnp.zeros((nb, SUBLANES, HG_DV), F32)
    for j in range(SUBLANES):
        bj = jnp.broadcast_to(b3[:, j:j + 1, :], b3.shape)
        kj = jnp.broadcast_to(k3[:, j:j + 1, :], b3.shape)
        vj = jnp.broadcast_to(v3[:, j:j + 1, :], b3.shape)
        dec = jnp.where(r3 >= j, jnp.exp(jnp.minimum(b3 - bj, 0.0)), 0.0)
        od = od + jnp.sum(q3 * kj * dec, axis=-1, keepdims=True) * vj
    o = o + od.reshape(HC, HG_DV)

    b_last = b_ref[pl.ds(HC - 1, 1), :]
    kd = (kk * jnp.exp(b_last - b)).astype(BF16)
    st_ref[...] = st * jnp.exp(b_last) + lax.dot_general(vb, kd, tn, preferred_element_type=F32)

    o = o * lax.rsqrt(jnp.mean(o * o, axis=-1, keepdims=True) + RMS_EPS) * ng
    gg = g_b.astype(F32)
    return (o * (gg * jax.nn.sigmoid(gg))).astype(BF16)


def _hgrn_kernel(q_ref, f_ref, i_ref, g_ref, lb_ref, ng_ref, lv_ref, tril_ref, o_ref, st_s, b_s):
    @pl.when(pl.program_id(2) == 0)
    def _():
        st_s[...] = jnp.zeros_like(st_s)

    lv, tril = lv_ref[...], tril_ref[...]
    lb, ng = lb_ref[0], ng_ref[0]
    for h in range(HG_STEP_HEADS):
        s = slice(h * HG_DK, (h + 1) * HG_DK)
        o_ref[:, s] = _hgrn_head(q_ref[:, s], f_ref[:, s], i_ref[:, s], g_ref[:, s], lb[:, s], ng[:, s],
                                 lv, tril, st_s.at[h], b_s.at[h])


def hgrn2(proj, lb, norm_g, batch, seq):
    n = batch * seq
    nc = seq // HC
    hs = HG_STEP_HEADS
    wide = hs * HG_DK
    lv, tril = _hgrn_level_tables()
    base = COL_HG // wide

    def slab(k):
        return pl.BlockSpec((HC, wide), lambda b, h, c: (b * nc + c, base + k * (HG_HEADS // hs) + h))

    vec = pl.BlockSpec((1, 1, wide), lambda b, h, c: (h, 0, 0))
    const = pl.BlockSpec((HC, HC), lambda b, h, c: (0, 0))
    return pl.pallas_call(
        _hgrn_kernel,
        grid=(batch, HG_HEADS // hs, nc),
        in_specs=[slab(0), slab(1), slab(2), slab(3), vec, vec, const, const],
        out_specs=pl.BlockSpec((HC, wide), lambda b, h, c: (b * nc + c, h)),
        out_shape=jax.ShapeDtypeStruct((n, HG_WIDTH), BF16),
        scratch_shapes=[pltpu.VMEM((hs, HG_DV, HG_DK), F32), pltpu.VMEM((hs, HC, HG_DK), F32)],
        compiler_params=_cparams("parallel", "parallel", "arbitrary"),
        name="hgrn2",
    )(proj, proj, proj, proj, lb.reshape(HG_HEADS // hs, 1, wide), norm_g.reshape(HG_HEADS // hs, 1, wide),
      jnp.asarray(lv), jnp.asarray(tril))


def _pool_kernel(p_ref, w_ref, sc_ref, o_ref):
    gi = pl.program_id(1)
    x = p_ref[...].astype(F32)
    t = lax.broadcasted_iota(jnp.int32, x.shape, 0)
    acc = x
    for k in range(len(POOL_WINDOWS)):
        sh = 1 << k
        nxt = acc + jnp.where(t >= sh, pltpu.roll(acc, sh, 0), 0.0)
        acc = jnp.where(k <= gi, nxt, acc)
    width = lax.shift_left(jnp.int32(2), gi)
    cnt = jnp.minimum(t + 1, width).astype(F32)
    mixed = acc / cnt - x
    y = jnp.dot(mixed.astype(BF16), w_ref[0], preferred_element_type=F32) * sc_ref[0]
    o_ref[...] = y.astype(BF16)


def multiscale_pool(proj, w_pool, scale, batch, seq):
    n = batch * seq
    ng = len(POOL_WINDOWS)
    base = COL_POOL // POOL_GROUP
    return pl.pallas_call(
        _pool_kernel,
        grid=(batch, ng),
        in_specs=[pl.BlockSpec((seq, POOL_GROUP), lambda b, g: (b, base + g)),
                  pl.BlockSpec((1, POOL_GROUP, POOL_GROUP), lambda b, g: (g, 0, 0)),
                  pl.BlockSpec((1, 1, POOL_GROUP), lambda b, g: (g, 0, 0))],
        out_specs=pl.BlockSpec((seq, POOL_GROUP), lambda b, g: (b, g)),
        out_shape=jax.ShapeDtypeStruct((n, POOL_WIDTH), BF16),
        compiler_params=_cparams("parallel", "parallel"),
        name="multiscale_pool",
    )(proj, w_pool, scale.reshape(ng, 1, POOL_GROUP))


def _merge_kernel(ya_ref, yb_ref, yc_ref, wa_ref, wb_ref, wc_ref, ga_ref, gb_ref, gc_ref, o_ref):
    def branch(y_ref, w_ref, g_ref):
        up = jnp.dot(y_ref[...], w_ref[...], preferred_element_type=F32)
        return jax.nn.sigmoid(g_ref[...].astype(F32)) * up

    o_ref[...] = (branch(ya_ref, wa_ref, ga_ref) + branch(yb_ref, wb_ref, gb_ref)
                  + branch(yc_ref, wc_ref, gc_ref)).astype(o_ref.dtype)


def merge_branches(ya, yb, yc, wa, wb, wc, proj, d_model, layer, tm=1024, tn=512):
    n = ya.shape[0]
    tm = min(tm, n)
    gbase = COL_MG // tn
    step = d_model // tn

    def y_spec(a):
        return pl.BlockSpec((tm, a.shape[1]), lambda i, j: (i, 0))

    def w_spec(a):
        return pl.BlockSpec((None, a.shape[1], tn), lambda i, j: (layer, 0, j))

    def g_spec(k):
        return pl.BlockSpec((tm, tn), lambda i, j: (i, gbase + k * step + j))

    return pl.pallas_call(
        _merge_kernel,
        grid=(n // tm, d_model // tn),
        in_specs=[y_spec(ya), y_spec(yb), y_spec(yc), w_spec(wa), w_spec(wb), w_spec(wc),
                  g_spec(0), g_spec(1), g_spec(2)],
        out_specs=pl.BlockSpec((tm, tn), lambda i, j: (i, j)),
        out_shape=jax.ShapeDtypeStruct((n, d_model), BF16),
        compiler_params=_cparams("parallel", "parallel"),
        name="merge_branches",
    )(ya, yb, yc, wa, wb, wc, proj, proj, proj)


def _router_kernel(x_ref, w_ref, b_ref, o_ref):
    logits = jnp.dot(x_ref[...], w_ref[...], preferred_element_type=F32)
    scores = jax.nn.sigmoid(logits)
    lane = lax.broadcasted_iota(jnp.int32, scores.shape, 1)
    real = lane < N_EXPERTS
    sel = _top_mask(jnp.where(real, scores + b_ref[...], -jnp.inf), lane, TOP_K)
    w = jnp.where(real, sel * scores, 0.0)
    o_ref[...] = w / jnp.sum(w, axis=-1, keepdims=True) * ROUTE_SCALE


def moe_router(x, router_w, router_b, tm=512):
    n, d = x.shape
    tm = min(tm, n)
    w = jnp.zeros((d, LANES), BF16).at[:, :N_EXPERTS].set(router_w.astype(BF16))
    b = jnp.zeros((1, LANES), F32).at[0, :N_EXPERTS].set(router_b.astype(F32))
    return pl.pallas_call(
        _router_kernel,
        grid=(n // tm,),
        in_specs=[pl.BlockSpec((tm, d), lambda i: (i, 0)), pl.BlockSpec((d, LANES), lambda i: (0, 0)),
                  pl.BlockSpec((1, LANES), lambda i: (0, 0))],
        out_specs=pl.BlockSpec((tm, LANES), lambda i: (i, 0)),
        out_shape=jax.ShapeDtypeStruct((n, LANES), F32),
        compiler_params=_cparams("parallel"),
        name="moe_router",
    )(x, w, b)


def _moe_up_kernel(ec, x_ref, w1_ref, w3_ref, gate_ref, ex_ref, o_ref):
    x = x_ref[...]
    w1 = jnp.concatenate([w1_ref[e] for e in range(ec)], axis=1)
    w3 = jnp.concatenate([w3_ref[e] for e in range(ec)], axis=1)
    h1 = jnp.dot(x, w1, preferred_element_type=F32)
    h3 = jnp.dot(x, w3, preferred_element_type=F32)
    g = gate_ref[...]
    g_hi = g.astype(BF16)
    g_lo = (g - g_hi.astype(F32)).astype(BF16)
    ex = ex_ref[...]
    gexp = jnp.dot(g_hi, ex, preferred_element_type=F32) + jnp.dot(g_lo, ex, preferred_element_type=F32)
    o_ref[...] = (h1 * jax.nn.sigmoid(h1) * h3 * gexp).astype(BF16)


def moe_hidden(x, gate, w1, w3, layer, tm=1024, ec=4):
    n, d = x.shape
    tm = min(tm, n)
    th = ec * D_EXPERT
    expand = (np.arange(N_EXPERTS * D_EXPERT)[None, :] // D_EXPERT == np.arange(LANES)[:, None])
    expand = jnp.asarray(expand.astype(np.float32), dtype=BF16)
    w_spec = pl.BlockSpec((None, ec, d, D_EXPERT), lambda i, j: (layer, j, 0, 0))
    return pl.pallas_call(
        functools.partial(_moe_up_kernel, ec),
        grid=(n // tm, N_EXPERTS // ec),
        in_specs=[pl.BlockSpec((tm, d), lambda i, j: (i, 0)), w_spec, w_spec,
                  pl.BlockSpec((tm, LANES), lambda i, j: (i, 0)),
                  pl.BlockSpec((LANES, th), lambda i, j: (0, j))],
        out_specs=pl.BlockSpec((tm, th), lambda i, j: (i, j)),
        out_shape=jax.ShapeDtypeStruct((n, N_EXPERTS * D_EXPERT), BF16),
        compiler_params=_cparams("parallel", "parallel"),
        name="moe_hidden",
    )(x, w1, w3, gate, expand)


def _glu_kernel(x_ref, w1_ref, w3_ref, o_ref):
    x = x_ref[...]
    h1 = jnp.dot(x, w1_ref[...], preferred_element_type=F32)
    h3 = jnp.dot(x, w3_ref[...], preferred_element_type=F32)
    o_ref[...] = (h1 * jax.nn.sigmoid(h1) * h3).astype(BF16)


def shared_hidden(x, ws1, ws3, layer, tm=1024):
    n, d = x.shape
    tm = min(tm, n)
    w_spec = pl.BlockSpec((None, d, D_SHARED), lambda i: (layer, 0, 0))
    return pl.pallas_call(
        _glu_kernel,
        grid=(n // tm,),
        in_specs=[pl.BlockSpec((tm, d), lambda i: (i, 0)), w_spec, w_spec],
        out_specs=pl.BlockSpec((tm, D_SHARED), lambda i: (i, 0)),
        out_shape=jax.ShapeDtypeStruct((n, D_SHARED), BF16),
        compiler_params=_cparams("parallel"),
        name="shared_hidden",
    )(x, ws1, ws3)


def _moe_down_kernel(hr_ref, hs_ref, w2_ref, ws2_ref, o_ref):
    y = jnp.dot(hr_ref[...], w2_ref[...], preferred_element_type=F32)
    y = y + jnp.dot(hs_ref[...], ws2_ref[...], preferred_element_type=F32)
    o_ref[...] = y.astype(o_ref.dtype)


def moe_down(hid_r, hid_s, w2, ws2, layer, tm=1024, tn=512):
    n, kr = hid_r.shape
    ks = hid_s.shape[1]
    d = w2.shape[-1]
    tm = min(tm, n)
    return pl.pallas_call(
        _moe_down_kernel,
        grid=(n // tm, d // tn),
        in_specs=[pl.BlockSpec((tm, kr), lambda i, j: (i, 0), pipeline_mode=pl.Buffered(1)),
                  pl.BlockSpec((tm, ks), lambda i, j: (i, 0)),
                  pl.BlockSpec((None, kr, tn), lambda i, j: (layer, 0, j)),
                  pl.BlockSpec((None, ks, tn), lambda i, j: (layer, 0, j))],
        out_specs=pl.BlockSpec((tm, tn), lambda i, j: (i, j)),
        out_shape=jax.ShapeDtypeStruct((n, d), BF16),
        compiler_params=_cparams("parallel", "parallel"),
        name="moe_down",
    )(hid_r, hid_s, w2, ws2)


def _branch_gate_weights(w_l, d_model):
    ag = w_l[:, ATTN_COLS:ATTN_COLS + 3 * ATTN_HEADS].reshape(d_model, 3, KV_GROUPS, HEADS_PER_GROUP)
    pad = jnp.zeros((d_model, LANES - 3 * HEADS_PER_GROUP), w_l.dtype)
    parts = []
    for g in range(KV_GROUPS):
        parts += [ag[:, :, g, :].reshape(d_model, 3 * HEADS_PER_GROUP), pad]
    return jnp.concatenate(parts, axis=1)


def _hgrn_lower_bounds(logits):
    lbs = jnp.cumsum(jax.nn.softmax(logits.astype(F32), axis=0), axis=0)
    return lbs - lbs[0:1]


def kernel(x, positions, ln_in_g, ln_in_b, w_in, cmp_pos_k, cmp_pos_v, cmp_w1_k, cmp_w2_k, cmp_w1_v, cmp_w2_v,
           hg_lb_logits, hg_norm_g, pool_w, pool_scale, w_up_attn, w_up_hg, w_up_pool, w_o, ln1_g, ln1_b,
           router_w, router_b, w1, w3, w2, ws1, ws3, ws2, ln2_g, ln2_b):
    batch, seq, d = x.shape
    n = batch * seq
    depth = w_in.shape[0]
    lbs = _hgrn_lower_bounds(hg_lb_logits)
    tabs = rope_tables(positions)
    w_in_b = w_in.astype(BF16)
    w1b, w3b = w1.astype(BF16), w3.astype(BF16)
    w2b = w2.astype(BF16).reshape(depth, N_EXPERTS * D_EXPERT, d)
    ws1b, ws3b, ws2b = ws1.astype(BF16), ws3.astype(BF16), ws2.astype(BF16)
    wab, whb, wpb, wob = w_up_attn.astype(BF16), w_up_hg.astype(BF16), w_up_pool.astype(BF16), w_o.astype(BF16)
    h32, h16 = layer_norm_rows(x.reshape(n, d), ln_in_g, ln_in_b)
    for l in range(depth):
        proj_a = matmul(h16, w_in_b, BF16, tm=512, tn=ATTN_COLS // 2, layer=l, cols=ATTN_COLS)
        proj_g = matmul(h16, _branch_gate_weights(w_in[l], d).astype(BF16), BF16)
        proj_b = matmul(h16, w_in_b[l][:, ATTN_COLS + 3 * ATTN_HEADS:], BF16)
        cmp_params = (cmp_pos_k[l], cmp_pos_v[l],
                      cmp_w1_k[l].reshape(CMP_BLOCK, HEAD_DIM, HEAD_DIM).astype(BF16), cmp_w2_k[l].astype(BF16),
                      cmp_w1_v[l].reshape(CMP_BLOCK, HEAD_DIM, HEAD_DIM).astype(BF16), cmp_w2_v[l].astype(BF16))
        ya = nsa_attention(proj_a, proj_g, tabs, cmp_params, batch, seq)
        yb = hgrn2(proj_b, lbs[l], hg_norm_g[l], batch, seq)
        yc = multiscale_pool(proj_b, pool_w[l].astype(BF16), pool_scale[l], batch, seq)
        merged = merge_branches(ya, yb, yc, wab, whb, wpb, proj_b, d, l)
        mix = matmul(merged, wob, BF16, layer=l)
        h32, h16 = layer_norm_rows(mix, ln1_g[l], ln1_b[l], res=h32, alpha=DN_ALPHA)
        gate = moe_router(h16, router_w[l], router_b[l])
        hid_r = moe_hidden(h16, gate, w1b, w3b, l)
        hid_s = shared_hidden(h16, ws1b, ws3b, l)
        ffn = moe_down(hid_r, hid_s, w2b, ws2b, l)
        h32, h16 = layer_norm_rows(ffn, ln2_g[l], ln2_b[l], res=h32, alpha=DN_ALPHA)
    return h32.reshape(batch, seq, d)
```

```python
import functools

import numpy as np
import jax
import jax.numpy as jnp
from jax import lax
from jax.experimental import pallas as pl
from jax.experimental.pallas import tpu as pltpu

F32 = jnp.float32
BF16 = jnp.bfloat16

DEPTH = 2
HEAD_DIM = 128
ATTN_HEADS = 16
KV_GROUPS = 2
HEADS_PER_GROUP = ATTN_HEADS // KV_GROUPS
ATTN_WIDTH = ATTN_HEADS * HEAD_DIM
KV_WIDTH = KV_GROUPS * HEAD_DIM
ROPE_DIM = HEAD_DIM // 4
ROPE_THETA = 500000.0
CMP_BLOCK = 32
CMP_STRIDE = 16
SLC_BLOCK = 32
SLC_TOPN = 8
WINDOW = 512
HG_HEADS = 8
HG_DK = 128
HG_DV = 128
HG_KWIDTH = HG_HEADS * HG_DK
HG_WIDTH = HG_HEADS * HG_DV
POOL_WINDOWS = (2, 4, 8, 16)
POOL_GROUP = 256
POOL_WIDTH = POOL_GROUP * len(POOL_WINDOWS)
N_EXPERTS = 64
TOP_K = 8
D_EXPERT = 128
D_SHARED = 256
ROUTE_SCALE = 2.5
DN_ALPHA = (2.0 * DEPTH) ** 0.25
LN_EPS = 1e-5
RMS_EPS = 1e-6
NEG = -1e30

LANES = 128
SUBLANES = 8
VMEM_LIMIT = 56 * 1024 * 1024

COL_KV = ATTN_WIDTH
ATTN_COLS = ATTN_WIDTH + 6 * KV_WIDTH
COL_HG = 0
COL_POOL = COL_HG + 4 * HG_WIDTH
COL_MG = COL_POOL + POOL_WIDTH

TQ = 128
KC = 512
WSPAN = WINDOW + TQ
HC = 128
HG_LEVELS = (64, 32, 16, 8)
HG_STEP_HEADS = 8


def _cparams(*sem):
    return pltpu.CompilerParams(dimension_semantics=sem, vmem_limit_bytes=VMEM_LIMIT)


def _ln_kernel(alpha, has_res, *refs):
    if has_res:
        x_ref, r_ref, g_ref, b_ref, o32_ref, o16_ref = refs
        x = alpha * r_ref[...] + x_ref[...].astype(F32)
    else:
        x_ref, g_ref, b_ref, o32_ref, o16_ref = refs
        x = x_ref[...]
    mu = jnp.mean(x, axis=-1, keepdims=True)
    xc = x - mu
    var = jnp.mean(xc * xc, axis=-1, keepdims=True)
    y = xc * lax.rsqrt(var + LN_EPS) * g_ref[...] + b_ref[...]
    o32_ref[...] = y
    o16_ref[...] = y.astype(BF16)


def layer_norm_rows(x, g, b, res=None, alpha=1.0, tm=256):
    n, d = x.shape
    row = pl.BlockSpec((tm, d), lambda i: (i, 0))
    vec = pl.BlockSpec((1, d), lambda i: (0, 0))
    ins = [x] + ([res] if res is not None else []) + [g.reshape(1, d), b.reshape(1, d)]
    specs = [row] + ([row] if res is not None else []) + [vec, vec]
    return pl.pallas_call(
        functools.partial(_ln_kernel, alpha, res is not None),
        grid=(n // tm,),
        in_specs=specs,
        out_specs=[row, row],
        out_shape=[jax.ShapeDtypeStruct((n, d), F32), jax.ShapeDtypeStruct((n, d), BF16)],
        compiler_params=_cparams("parallel"),
        name="layer_norm",
    )(*ins)


def _mm_kernel(x_ref, w_ref, o_ref):
    o_ref[...] = jnp.dot(x_ref[...], w_ref[...], preferred_element_type=F32).astype(o_ref.dtype)


def matmul(x, w, out_dtype, tm=1024, tn=1024, layer=None, cols=None):
    n, k = x.shape
    m = cols if cols is not None else w.shape[-1]
    tm, tn = min(tm, n), min(tn, m)
    assert m % tn == 0
    if layer is None:
        w_spec = pl.BlockSpec((k, tn), lambda i, j: (0, j))
    else:
        w_spec = pl.BlockSpec((None, k, tn), lambda i, j: (layer, 0, j))
    return pl.pallas_call(
        _mm_kernel,
        grid=(n // tm, m // tn),
        in_specs=[pl.BlockSpec((tm, k), lambda i, j: (i, 0)), w_spec],
        out_specs=pl.BlockSpec((tm, tn), lambda i, j: (i, j)),
        out_shape=jax.ShapeDtypeStruct((n, m), out_dtype),
        compiler_params=_cparams("parallel", "parallel"),
        name="matmul",
    )(x, w)


def _rope_table_kernel(pos_ref, inv_ref, c_ref, sa_ref, sb_ref):
    ang = pos_ref[...].astype(F32) * inv_ref[...]
    lane = lax.broadcasted_iota(jnp.int32, ang.shape, 1)
    sn = jnp.sin(ang)
    c_ref[...] = jnp.cos(ang)
    sa_ref[...] = jnp.where(lane < ROPE_DIM // 2, -sn, 0.0)
    sb_ref[...] = jnp.where((lane >= ROPE_DIM // 2) & (lane < ROPE_DIM), sn, 0.0)


def rope_tables(positions, tm=1024):
    n = positions.size
    half = ROPE_DIM // 2
    inv = ROPE_THETA ** (-np.arange(half, dtype=np.float32) * 2.0 / ROPE_DIM)
    inv_full = np.zeros((1, LANES), np.float32)
    inv_full[0, :half] = inv
    inv_full[0, half:ROPE_DIM] = inv
    tm = min(tm, n)
    out = jax.ShapeDtypeStruct((n, LANES), F32)
    spec = pl.BlockSpec((tm, LANES), lambda i: (i, 0))
    return pl.pallas_call(
        _rope_table_kernel,
        grid=(n // tm,),
        in_specs=[pl.BlockSpec((tm, 1), lambda i: (i, 0)), pl.BlockSpec((1, LANES), lambda i: (0, 0))],
        out_specs=[spec, spec, spec],
        out_shape=[out, out, out],
        compiler_params=_cparams("parallel"),
        name="rope_tables",
    )(positions.reshape(n, 1), jnp.asarray(inv_full))


def _rope(x, c, sa, sb):
    return x * c + pltpu.roll(x, LANES - ROPE_DIM // 2, 1) * sa + pltpu.roll(x, ROPE_DIM // 2, 1) * sb


def _gelu_tanh(x):
    return 0.5 * x * (1.0 + jnp.tanh(0.7978845608028654 * (x + 0.044715 * x * x * x)))


def _top_mask(val, lane, n_pick, axis=-1):
    sel = jnp.zeros(val.shape, F32)
    for _ in range(n_pick):
        m = jnp.max(val, axis=axis, keepdims=True)
        idx = jnp.min(jnp.where(val == m, lane, LANES), axis=axis, keepdims=True)
        pick = lane == idx
        sel = jnp.where(pick, 1.0, sel)
        val = jnp.where(pick, -jnp.inf, val)
    return sel


def _attn_kernel(q_ref, kc_ref, vc_ref, ks_ref, vs_ref, kw_ref, vw_ref, gate_ref,
                 cq_ref, saq_ref, sbq_ref, ck_ref, sak_ref, sbk_ref,
                 posk_ref, posv_ref, w1k_ref, w2k_ref, w1v_ref, w2v_ref, ov_ref, nege_ref,
                 o_ref, kcmp_s, vcmp_s, kx_s, kwr_s, vsx_s, vwx_s, tmp_s, sca_s, scb_s, m_s, acc_s, sw_s, wbias_s):
    seq = ks_ref.shape[0]
    nh = seq // CMP_STRIDE
    n_cmp = (seq - CMP_BLOCK) // CMP_STRIDE + 1
    hpg = HEADS_PER_GROUP
    scale = HEAD_DIM ** -0.5
    scale2 = scale * 1.4426950408889634
    qt = pl.program_id(2)
    nt = (((1,), (1,)), ((), ()))

    @pl.when(qt == 0)
    def _per_sequence():
        def compress(t_ref, pos_ref, w1_ref, w2_ref, out_s):
            tmp_s[...] = t_ref[...].astype(F32)
            first = jnp.zeros((nh, HEAD_DIM), F32)
            second = jnp.zeros((nh, HEAD_DIM), F32)
            for j in range(CMP_BLOCK):
                x = tmp_s[pl.ds(j % CMP_STRIDE, nh, stride=CMP_STRIDE), :] + pos_ref[j:j + 1, :]
                p = jnp.dot(x.astype(BF16), w1_ref[j], preferred_element_type=F32)
                if j < CMP_STRIDE:
                    first = first + p
                else:
                    second = second + p
            pre = first + pltpu.roll(second, nh - 1, 0)
            hid = _gelu_tanh(pre).astype(BF16)
            out_s[...] = jnp.dot(hid, w2_ref[...], preferred_element_type=F32).astype(BF16)

        compress(kc_ref, posk_ref, w1k_ref, w2k_ref, kcmp_s)
        compress(vc_ref, posv_ref, w1v_ref, w2v_ref, vcmp_s)
        ck, sak, sbk = ck_ref[...], sak_ref[...], sbk_ref[...]
        kx_s[:, :HEAD_DIM] = _rope(ks_ref[...].astype(F32), ck, sak, sbk).astype(BF16)
        kx_s[:, HEAD_DIM:] = nege_ref[...]
        ones = jnp.ones((seq, HEAD_DIM), BF16)
        vsx_s[:, :HEAD_DIM] = vs_ref[...]
        vsx_s[:, HEAD_DIM:] = ones
        vwx_s[:, :HEAD_DIM] = vw_ref[...]
        vwx_s[:, HEAD_DIM:] = ones
        kwr_s[...] = _rope(kw_ref[...].astype(F32), ck, sak, sbk).astype(BF16)

    t0 = qt * TQ
    t = t0 + lax.broadcasted_iota(jnp.int32, (TQ, 1), 0)
    lane = lax.broadcasted_iota(jnp.int32, (TQ, LANES), 1)
    q = q_ref[...]
    qf = q.astype(F32)
    cq, saq, sbq = cq_ref[...], saq_ref[...], sbq_ref[...]
    heads = [slice(h * HEAD_DIM, (h + 1) * HEAD_DIM) for h in range(hpg)]
    q_raw = jnp.concatenate([q[:, s] for s in heads], axis=0)
    q_rot = jnp.concatenate([_rope(qf[:, s], cq, saq, sbq).astype(BF16) for s in heads], axis=0)

    s = lax.dot_general(q_raw, kcmp_s[...], nt, preferred_element_type=F32) * scale
    s3 = s.reshape(hpg, TQ, nh)
    vis = (lane * CMP_STRIDE + (CMP_BLOCK - 1) <= t) & (lane < n_cmp)
    s3 = jnp.where(vis[None], s3, NEG)
    e3 = jnp.exp(s3 - jnp.max(s3, axis=-1, keepdims=True))
    p3 = e3 / jnp.sum(e3, axis=-1, keepdims=True) * vis[None].astype(F32)
    pb = p3.astype(BF16)
    o_c = jnp.dot(pb.reshape(hpg * TQ, nh), vcmp_s[...], preferred_element_type=F32).reshape(hpg, TQ, HEAD_DIM)

    psum = jnp.sum(pb.astype(F32), axis=0)
    imp = lax.dot_general(ov_ref[...], psum, nt, preferred_element_type=F32, precision=lax.Precision.HIGHEST)
    n_slc = imp.shape[0]
    blk = lax.broadcasted_iota(jnp.int32, (n_slc, TQ), 0)
    blk_t = (t0 + lax.broadcasted_iota(jnp.int32, (n_slc, TQ), 1)) // SLC_BLOCK
    causal = blk <= blk_t
    forced = (blk == 0) | (blk == blk_t) | (blk == blk_t - 1)
    val = jnp.where(forced, jnp.inf, jnp.where(causal, imp, -jnp.inf))
    not_sel = jnp.where(causal, 1.0 - _top_mask(val, blk, SLC_TOPN, axis=0), 1.0)
    not_sel = jnp.concatenate([not_sel.T, jnp.zeros((TQ, HEAD_DIM - n_slc), F32)], axis=1).astype(BF16)
    q_ext = jnp.concatenate([q_rot, jnp.concatenate([not_sel] * hpg, axis=0)], axis=1)

    w0 = pl.multiple_of(jnp.maximum(t0 - WINDOW, 0), TQ)
    kk = kwr_s[pl.ds(w0, WSPAN), :]
    vv = vwx_s[pl.ds(w0, WSPAN), :]
    sw_s[...] = lax.dot_general(q_rot, kk, nt, preferred_element_type=F32)
    kpos = w0 + lax.broadcasted_iota(jnp.int32, (TQ, WSPAN), 1)
    wbias_s[...] = jnp.where((kpos <= t) & (t - kpos < WINDOW), 0.0, NEG)
    m_w = jnp.max(sw_s[...].reshape(hpg, TQ, WSPAN) + wbias_s[...][None], axis=-1, keepdims=True)
    pw = jnp.exp2((sw_s[...].reshape(hpg, TQ, WSPAN) + wbias_s[...][None] - m_w) * scale2)
    o_w = jnp.dot(pw.astype(BF16).reshape(hpg * TQ, WSPAN), vv, preferred_element_type=F32)
    o_w = (o_w[:, :HEAD_DIM] / o_w[:, HEAD_DIM:]).reshape(hpg, TQ, HEAD_DIM)

    g = jax.nn.sigmoid(gate_ref[...].astype(F32))
    o_cw = [g[:, h:h + 1] * o_c[h] + g[:, 2 * hpg + h:2 * hpg + h + 1] * o_w[h] for h in range(hpg)]

    n_chunks = (t0 + TQ + KC - 1) // KC
    last = n_chunks - 1
    tri_bias = jnp.where(lane <= lax.broadcasted_iota(jnp.int32, (TQ, TQ), 0), 0.0, NEG)

    def scores(c):
        kk = kx_s[pl.ds(pl.multiple_of(c * KC, KC), KC), :]
        return lax.dot_general(q_ext, kk, nt, preferred_element_type=F32)

    def softmax_pv(sc_ref, c, first):
        vv = vsx_s[pl.ds(pl.multiple_of(c * KC, KC), KC), :]
        m_new = jnp.max(sc_ref[...].reshape(hpg, TQ, KC), axis=-1, keepdims=True)
        if not first:
            m_old = m_s[...].reshape(hpg, TQ, 1)
            m_new = jnp.maximum(m_old, m_new)
        p = jnp.exp2((sc_ref[...].reshape(hpg, TQ, KC) - m_new) * scale2)
        pv = jnp.dot(p.astype(BF16).reshape(hpg * TQ, KC), vv, preferred_element_type=F32)
        if first:
            acc_s[...] = pv
        else:
            a = jnp.exp2((m_old - m_new) * scale2).reshape(hpg * TQ, 1)
            acc_s[...] = a * acc_s[...] + pv
        m_s[...] = m_new.reshape(hpg * TQ, 1)

    sca_s[...] = scores(last)
    own = pl.ds(pl.multiple_of(t0 - last * KC, TQ), TQ)
    for h in range(hpg):
        sca_s[h * TQ:(h + 1) * TQ, own] += tri_bias
    scb_s[...] = scores(0)
    softmax_pv(sca_s, last, True)

    def chunk_pair(i, carry):
        c0 = 2 * i
        sca_s[...] = scores(jnp.minimum(c0 + 1, last))
        softmax_pv(scb_s, c0, False)

        @pl.when(c0 + 1 < last)
        def _():
            scb_s[...] = scores(jnp.minimum(c0 + 2, last))
            softmax_pv(sca_s, c0 + 1, False)

        return carry

    lax.fori_loop(0, (last + 1) // 2, chunk_pair, 0)
    acc = acc_s[...]
    o_s = (acc[:, :HEAD_DIM] / acc[:, HEAD_DIM:]).reshape(hpg, TQ, HEAD_DIM)
    for h in range(hpg):
        o_ref[:, heads[h]] = (o_cw[h] + g[:, hpg + h:hpg + h + 1] * o_s[h]).astype(BF16)


def _overlap_matrix(n_half, n_cmp, n_slc):
    c = np.arange(n_half)[None, :] * CMP_STRIDE
    s = np.arange(n_slc)[:, None] * SLC_BLOCK
    ov = np.clip(np.minimum(c + CMP_BLOCK, s + SLC_BLOCK) - np.maximum(c, s), 0, None) / CMP_STRIDE
    ov[:, n_cmp:] = 0.0
    return ov.astype(np.float32)


def nsa_attention(proj, proj_g, tabs, cmp_params, batch, seq):
    n = batch * seq
    nq = seq // TQ
    nh = seq // CMP_STRIDE
    n_cmp = (seq - CMP_BLOCK) // CMP_STRIDE + 1
    n_slc = seq // SLC_BLOCK
    assert nh == LANES and n_slc <= LANES and seq % KC == 0 and seq >= WSPAN
    c_tab, sa_tab, sb_tab = tabs
    posk, posv, w1k, w2k, w1v, w2v = cmp_params
    ov = jnp.asarray(_overlap_matrix(nh, n_cmp, n_slc))
    neg_onehot = np.where(np.arange(seq)[:, None] // SLC_BLOCK == np.arange(HEAD_DIM)[None, :], NEG, 0.0)
    expand = jnp.asarray(neg_onehot.astype(np.float32), dtype=BF16)

    gw = HEADS_PER_GROUP * HEAD_DIM
    qspec = pl.BlockSpec((TQ, gw), lambda b, g, i: (b * nq + i, g))

    def kvspec(slab):
        return pl.BlockSpec((seq, HEAD_DIM), lambda b, g, i: (b, COL_KV // HEAD_DIM + slab * KV_GROUPS + g))

    gspec = pl.BlockSpec((TQ, LANES), lambda b, g, i: (b * nq + i, g))
    tq_spec = pl.BlockSpec((TQ, LANES), lambda b, g, i: (b * nq + i, 0))
    tk_spec = pl.BlockSpec((seq, LANES), lambda b, g, i: (b, 0))

    def full(a):
        return pl.BlockSpec(a.shape, lambda b, g, i: (0,) * a.ndim)

    consts = [posk, posv, w1k, w2k, w1v, w2v, ov, expand]
    return pl.pallas_call(
        _attn_kernel,
        grid=(batch, KV_GROUPS, nq),
        in_specs=[qspec] + [kvspec(s) for s in range(6)] + [gspec] + [tq_spec] * 3 + [tk_spec] * 3
                 + [full(a) for a in consts],
        out_specs=pl.BlockSpec((TQ, gw), lambda b, g, i: (b * nq + i, g)),
        out_shape=jax.ShapeDtypeStruct((n, ATTN_WIDTH), BF16),
        scratch_shapes=[pltpu.VMEM((nh, HEAD_DIM), BF16), pltpu.VMEM((nh, HEAD_DIM), BF16),
                        pltpu.VMEM((seq, 2 * HEAD_DIM), BF16), pltpu.VMEM((seq, HEAD_DIM), BF16),
                        pltpu.VMEM((seq, 2 * HEAD_DIM), BF16), pltpu.VMEM((seq, 2 * HEAD_DIM), BF16),
                        pltpu.VMEM((seq, HEAD_DIM), F32),
                        pltpu.VMEM((HEADS_PER_GROUP * TQ, KC), F32), pltpu.VMEM((HEADS_PER_GROUP * TQ, KC), F32),
                        pltpu.VMEM((HEADS_PER_GROUP * TQ, 1), F32), pltpu.VMEM((HEADS_PER_GROUP * TQ, 2 * HEAD_DIM), F32),
                        pltpu.VMEM((HEADS_PER_GROUP * TQ, WSPAN), F32), pltpu.VMEM((TQ, WSPAN), F32)],
        compiler_params=_cparams("parallel", "parallel", "arbitrary"),
        name="nsa_attention",
    )(proj, proj, proj, proj, proj, proj, proj, proj_g, c_tab, sa_tab, sb_tab, c_tab, sa_tab, sb_tab, *consts)


def _hgrn_level_tables():
    t = np.arange(HC)[:, None]
    s = np.arange(HC)[None, :]
    lv = np.full((HC, HC), -1, np.int32)
    lv[(t // SUBLANES == s // SUBLANES)] = -1
    for i, m in enumerate(HG_LEVELS):
        ok = ((t // m) % 2 == 1) & (s // m == t // m - 1)
        lv[ok] = i
    tril = (s <= t).astype(np.float32)
    return lv, tril


def _hgrn_head(q_b, f_b, i_b, g_b, lb, ng, lv, tril, st_ref, b_ref):
    nt = (((1,), (1,)), ((), ()))
    tn = (((0,), (0,)), ((), ()))
    f = lb + (1.0 - lb) * jax.nn.sigmoid(f_b.astype(F32))
    logf = jnp.log(f)
    kk = 1.0 - f
    q = q_b.astype(F32)
    v = i_b.astype(F32)
    vb = i_b
    b = jnp.dot(tril, logf, preferred_element_type=F32, precision=lax.Precision.HIGHEST)
    b_ref[...] = b
    row = lax.broadcasted_iota(jnp.int32, (HC, HG_DK), 0)

    st = st_ref[...]
    o = lax.dot_general((q * jnp.exp(b)).astype(BF16), st.astype(BF16), nt, preferred_element_type=F32)

    a = jnp.zeros((HC, HC), F32)
    for i, m in enumerate(HG_LEVELS):
        ref_rows = [jnp.broadcast_to(b_ref[pl.ds((2 * j + 1) * m - 1, 1), :], (2 * m, HG_DK))
                    for j in range(HC // (2 * m))]
        d = b - jnp.concatenate(ref_rows, axis=0)
        odd = (row // m) % 2 == 1
        x = (jnp.where(odd, q, kk) * jnp.exp(-jnp.abs(d))).astype(BF16)
        am = lax.dot_general(x, x, nt, preferred_element_type=F32)
        a = jnp.where(lv == i, am, a)
    o = o + jnp.dot(a.astype(BF16), vb, preferred_element_type=F32)

    nb = HC // SUBLANES
    b3 = b.reshape(nb, SUBLANES, HG_DK)
    q3 = q.reshape(nb, SUBLANES, HG_DK)
    k3 = kk.reshape(nb, SUBLANES, HG_DK)
    v3 = v.reshape(nb, SUBLANES, HG_DV)
    r3 = lax.broadcasted_iota(jnp.int32, (nb, SUBLANES, HG_DK), 1)
    od = jnp.zeros((nb, SUBLANES, HG_DV), F32)
    for j in range(SUBLANES):
        bj = jnp.broadcast_to(b3[:, j:j + 1, :], b3.shape)
        kj = jnp.broadcast_to(k3[:, j:j + 1, :], b3.shape)
        vj = jnp.broadcast_to(v3[:, j:j + 1, :], b3.shape)
        dec = jnp.where(r3 >= j, jnp.exp(jnp.minimum(b3 - bj, 0.0)), 0.0)
        od = od + jnp.sum(q3 * kj * dec, axis=-1, keepdims=True) * vj
    o = o + od.reshape(HC, HG_DV)

    b_last = b_ref[pl.ds(HC - 1, 1), :]
    kd = (kk * jnp.exp(b_last - b)).astype(BF16)
    st_ref[...] = st * jnp.exp(b_last) + lax.dot_general(vb, kd, tn, preferred_element_type=F32)

    o = o * lax.rsqrt(jnp.mean(o * o, axis=-1, keepdims=True) + RMS_EPS) * ng
    gg = g_b.astype(F32)
    return (o * (gg * jax.nn.sigmoid(gg))).astype(BF16)


def _hgrn_kernel(q_ref, f_ref, i_ref, g_ref, lb_ref, ng_ref, lv_ref, tril_ref, o_ref, st_s, b_s):
    @pl.when(pl.program_id(2) == 0)
    def _():
        st_s[...] = jnp.zeros_like(st_s)

    lv, tril = lv_ref[...], tril_ref[...]
    lb, ng = lb_ref[0], ng_ref[0]
    for h in range(HG_STEP_HEADS):
        s = slice(h * HG_DK, (h + 1) * HG_DK)
        o_ref[:, s] = _hgrn_head(q_ref[:, s], f_ref[:, s], i_ref[:, s], g_ref[:, s], lb[:, s], ng[:, s],
                                 lv, tril, st_s.at[h], b_s.at[h])


def hgrn2(proj, lb, norm_g, batch, seq):
    n = batch * seq
    nc = seq // HC
    hs = HG_STEP_HEADS
    wide = hs * HG_DK
    lv, tril = _hgrn_level_tables()
    base = COL_HG // wide

    def slab(k):
        return pl.BlockSpec((HC, wide), lambda b, h, c: (b * nc + c, base + k * (HG_HEADS // hs) + h))

    vec = pl.BlockSpec((1, 1, wide), lambda b, h, c: (h, 0, 0))
    const = pl.BlockSpec((HC, HC), lambda b, h, c: (0, 0))
    return pl.pallas_call(
        _hgrn_kernel,
        grid=(batch, HG_HEADS // hs, nc),
        in_specs=[slab(0), slab(1), slab(2), slab(3), vec, vec, const, const],
        out_specs=pl.BlockSpec((HC, wide), lambda b, h, c: (b * nc + c, h)),
        out_shape=jax.ShapeDtypeStruct((n, HG_WIDTH), BF16),
        scratch_shapes=[pltpu.VMEM((hs, HG_DV, HG_DK), F32), pltpu.VMEM((hs, HC, HG_DK), F32)],
        compiler_params=_cparams("parallel", "parallel", "arbitrary"),
        name="hgrn2",
    )(proj, proj, proj, proj, lb.reshape(HG_HEADS // hs, 1, wide), norm_g.reshape(HG_HEADS // hs, 1, wide),
      jnp.asarray(lv), jnp.asarray(tril))


def _pool_kernel(p_ref, w_ref, sc_ref, o_ref):
    gi = pl.program_id(1)
    x = p_ref[...].astype(F32)
    t = lax.broadcasted_iota(jnp.int32, x.shape, 0)
    acc = x
    for k in range(len(POOL_WINDOWS)):
        sh = 1 << k
        nxt = acc + jnp.where(t >= sh, pltpu.roll(acc, sh, 0), 0.0)
        acc = jnp.where(k <= gi, nxt, acc)
    width = lax.shift_left(jnp.int32(2), gi)
    cnt = jnp.minimum(t + 1, width).astype(F32)
    mixed = acc / cnt - x
    y = jnp.dot(mixed.astype(BF16), w_ref[0], preferred_element_type=F32) * sc_ref[0]
    o_ref[...] = y.astype(BF16)


def multiscale_pool(proj, w_pool, scale, batch, seq):
    n = batch * seq
    ng = len(POOL_WINDOWS)
    base = COL_POOL // POOL_GROUP
    return pl.pallas_call(
        _pool_kernel,
        grid=(batch, ng),
        in_specs=[pl.BlockSpec((seq, POOL_GROUP), lambda b, g: (b, base + g)),
                  pl.BlockSpec((1, POOL_GROUP, POOL_GROUP), lambda b, g: (g, 0, 0)),
                  pl.BlockSpec((1, 1, POOL_GROUP), lambda b, g: (g, 0, 0))],
        out_specs=pl.BlockSpec((seq, POOL_GROUP), lambda b, g: (b, g)),
        out_shape=jax.ShapeDtypeStruct((n, POOL_WIDTH), BF16),
        compiler_params=_cparams("parallel", "parallel"),
        name="multiscale_pool",
    )(proj, w_pool, scale.reshape(ng, 1, POOL_GROUP))


def _merge_kernel(ya_ref, yb_ref, yc_ref, wa_ref, wb_ref, wc_ref, ga_ref, gb_ref, gc_ref, o_ref):
    def branch(y_ref, w_ref, g_ref):
        up = jnp.dot(y_ref[...], w_ref[...], preferred_element_type=F32)
        return jax.nn.sigmoid(g_ref[...].astype(F32)) * up

    o_ref[...] = (branch(ya_ref, wa_ref, ga_ref) + branch(yb_ref, wb_ref, gb_ref)
                  + branch(yc_ref, wc_ref, gc_ref)).astype(o_ref.dtype)


def merge_branches(ya, yb, yc, wa, wb, wc, proj, d_model, layer, tm=1024, tn=512):
    n = ya.shape[0]
    tm = min(tm, n)
    gbase = COL_MG // tn
    step = d_model // tn

    def y_spec(a):
        return pl.BlockSpec((tm, a.shape[1]), lambda i, j: (i, 0))

    def w_spec(a):
        return pl.BlockSpec((None, a.shape[1], tn), lambda i, j: (layer, 0, j))

    def g_spec(k):
        return pl.BlockSpec((tm, tn), lambda i, j: (i, gbase + k * step + j))

    return pl.pallas_call(
        _merge_kernel,
        grid=(n // tm, d_model // tn),
        in_specs=[y_spec(ya), y_spec(yb), y_spec(yc), w_spec(wa), w_spec(wb), w_spec(wc),
                  g_spec(0), g_spec(1), g_spec(2)],
        out_specs=pl.BlockSpec((tm, tn), lambda i, j: (i, j)),
        out_shape=jax.ShapeDtypeStruct((n, d_model), BF16),
        compiler_params=_cparams("parallel", "parallel"),
        name="merge_branches",
    )(ya, yb, yc, wa, wb, wc, proj, proj, proj)


def _router_kernel(x_ref, w_ref, b_ref, o_ref):
    logits = jnp.dot(x_ref[...], w_ref[...], preferred_element_type=F32)
    scores = jax.nn.sigmoid(logits)
    lane = lax.broadcasted_iota(jnp.int32, scores.shape, 1)
    real = lane < N_EXPERTS
    sel = _top_mask(jnp.where(real, scores + b_ref[...], -jnp.inf), lane, TOP_K)
    w = jnp.where(real, sel * scores, 0.0)
    o_ref[...] = w / jnp.sum(w, axis=-1, keepdims=True) * ROUTE_SCALE


def moe_router(x, router_w, router_b, tm=512):
    n, d = x.shape
    tm = min(tm, n)
    w = jnp.zeros((d, LANES), BF16).at[:, :N_EXPERTS].set(router_w.astype(BF16))
    b = jnp.zeros((1, LANES), F32).at[0, :N_EXPERTS].set(router_b.astype(F32))
    return pl.pallas_call(
        _router_kernel,
        grid=(n // tm,),
        in_specs=[pl.BlockSpec((tm, d), lambda i: (i, 0)), pl.BlockSpec((d, LANES), lambda i: (0, 0)),
                  pl.BlockSpec((1, LANES), lambda i: (0, 0))],
        out_specs=pl.BlockSpec((tm, LANES), lambda i: (i, 0)),
        out_shape=jax.ShapeDtypeStruct((n, LANES), F32),
        compiler_params=_cparams("parallel"),
        name="moe_router",
    )(x, w, b)


def _moe_up_kernel(ec, x_ref, w1_ref, w3_ref, gate_ref, ex_ref, o_ref):
    x = x_ref[...]
    w1 = jnp.concatenate([w1_ref[e] for e in range(ec)], axis=1)
    w3 = jnp.concatenate([w3_ref[e] for e in range(ec)], axis=1)
    h1 = jnp.dot(x, w1, preferred_element_type=F32)
    h3 = jnp.dot(x, w3, preferred_element_type=F32)
    g = gate_ref[...]
    g_hi = g.astype(BF16)
    g_lo = (g - g_hi.astype(F32)).astype(BF16)
    ex = ex_ref[...]
    gexp = jnp.dot(g_hi, ex, preferred_element_type=F32) + jnp.dot(g_lo, ex, preferred_element_type=F32)
    o_ref[...] = (h1 * jax.nn.sigmoid(h1) * h3 * gexp).astype(BF16)


def moe_hidden(x, gate, w1, w3, layer, tm=1024, ec=4):
    n, d = x.shape
    tm = min(tm, n)
    th = ec * D_EXPERT
    expand = (np.arange(N_EXPERTS * D_EXPERT)[None, :] // D_EXPERT == np.arange(LANES)[:, None])
    expand = jnp.asarray(expand.astype(np.float32), dtype=BF16)
    w_spec = pl.BlockSpec((None, ec, d, D_EXPERT), lambda i, j: (layer, j, 0, 0))
    return pl.pallas_call(
        functools.partial(_moe_up_kernel, ec),
        grid=(n // tm, N_EXPERTS // ec),
        in_specs=[pl.BlockSpec((tm, d), lambda i, j: (i, 0)), w_spec, w_spec,
                  pl.BlockSpec((tm, LANES), lambda i, j: (i, 0)),
                  pl.BlockSpec((LANES, th), lambda i, j: (0, j))],
        out_specs=pl.BlockSpec((tm, th), lambda i, j: (i, j)),
        out_shape=jax.ShapeDtypeStruct((n, N_EXPERTS * D_EXPERT), BF16),
        compiler_params=_cparams("parallel", "parallel"),
        name="moe_hidden",
    )(x, w1, w3, gate, expand)


def _glu_kernel(x_ref, w1_ref, w3_ref, o_ref):
    x = x_ref[...]
    h1 = jnp.dot(x, w1_ref[...], preferred_element_type=F32)
    h3 = jnp.dot(x, w3_ref[...], preferred_element_type=F32)
    o_ref[...] = (h1 * jax.nn.sigmoid(h1) * h3).astype(BF16)


def shared_hidden(x, ws1, ws3, layer, tm=1024):
    n, d = x.shape
    tm = min(tm, n)
    w_spec = pl.BlockSpec((None, d, D_SHARED), lambda i: (layer, 0, 0))
    return pl.pallas_call(
        _glu_kernel,
        grid=(n // tm,),
        in_specs=[pl.BlockSpec((tm, d), lambda i: (i, 0)), w_spec, w_spec],
        out_specs=pl.BlockSpec((tm, D_SHARED), lambda i: (i, 0)),
        out_shape=jax.ShapeDtypeStruct((n, D_SHARED), BF16),
        compiler_params=_cparams("parallel"),
        name="shared_hidden",
    )(x, ws1, ws3)


def _moe_down_kernel(hr_ref, hs_ref, w2_ref, ws2_ref, o_ref):
    y = jnp.dot(hr_ref[...], w2_ref[...], preferred_element_type=F32)
    y = y + jnp.dot(hs_ref[...], ws2_ref[...], preferred_element_type=F32)
    o_ref[...] = y.astype(o_ref.dtype)


def moe_down(hid_r, hid_s, w2, ws2, layer, tm=1024, tn=512):
    n, kr = hid_r.shape
    ks = hid_s.shape[1]
    d = w2.shape[-1]
    tm = min(tm, n)
    return pl.pallas_call(
        _moe_down_kernel,
        grid=(n // tm, d // tn),
        in_specs=[pl.BlockSpec((tm, kr), lambda i, j: (i, 0), pipeline_mode=pl.Buffered(1)),
                  pl.BlockSpec((tm, ks), lambda i, j: (i, 0)),
                  pl.BlockSpec((None, kr, tn), lambda i, j: (layer, 0, j)),
                  pl.BlockSpec((None, ks, tn), lambda i, j: (layer, 0, j))],
        out_specs=pl.BlockSpec((tm, tn), lambda i, j: (i, j)),
        out_shape=jax.ShapeDtypeStruct((n, d), BF16),
        compiler_params=_cparams("parallel", "parallel"),
        name="moe_down",
    )(hid_r, hid_s, w2, ws2)


def _branch_gate_weights(w_l, d_model):
    ag = w_l[:, ATTN_COLS:ATTN_COLS + 3 * ATTN_HEADS].reshape(d_model, 3, KV_GROUPS, HEADS_PER_GROUP)
    pad = jnp.zeros((d_model, LANES - 3 * HEADS_PER_GROUP), w_l.dtype)
    parts = []
    for g in range(KV_GROUPS):
        parts += [ag[:, :, g, :].reshape(d_model, 3 * HEADS_PER_GROUP), pad]
    return jnp.concatenate(parts, axis=1)


def _hgrn_lower_bounds(logits):
    lbs = jnp.cumsum(jax.nn.softmax(logits.astype(F32), axis=0), axis=0)
    return lbs - lbs[0:1]


def kernel(x, positions, ln_in_g, ln_in_b, w_in, cmp_pos_k, cmp_pos_v, cmp_w1_k, cmp_w2_k, cmp_w1_v, cmp_w2_v,
           hg_lb_logits, hg_norm_g, pool_w, pool_scale, w_up_attn, w_up_hg, w_up_pool, w_o, ln1_g, ln1_b,
           router_w, router_b, w1, w3, w2, ws1, ws3, ws2, ln2_g, ln2_b):
    batch, seq, d = x.shape
    n = batch * seq
    depth = w_in.shape[0]
    lbs = _hgrn_lower_bounds(hg_lb_logits)
    tabs = rope_tables(positions)
    w_in_b = w_in.astype(BF16)
    w_in_tail = w_in_b[:, :, ATTN_COLS + 3 * ATTN_HEADS:]
    w1b, w3b = w1.astype(BF16), w3.astype(BF16)
    w2b = w2.astype(BF16).reshape(depth, N_EXPERTS * D_EXPERT, d)
    ws1b, ws3b, ws2b = ws1.astype(BF16), ws3.astype(BF16), ws2.astype(BF16)
    wab, whb, wpb, wob = w_up_attn.astype(BF16), w_up_hg.astype(BF16), w_up_pool.astype(BF16), w_o.astype(BF16)
    h32, h16 = layer_norm_rows(x.reshape(n, d), ln_in_g, ln_in_b)
    for l in range(depth):
        proj_a = matmul(h16, w_in_b, BF16, tm=512, tn=ATTN_COLS // 2, layer=l, cols=ATTN_COLS)
        proj_g = matmul(h16, _branch_gate_weights(w_in_b[l], d), BF16)
        proj_b = matmul(h16, w_in_tail, BF16, layer=l)
        cmp_params = (cmp_pos_k[l], cmp_pos_v[l],
                      cmp_w1_k[l].reshape(CMP_BLOCK, HEAD_DIM, HEAD_DIM).astype(BF16), cmp_w2_k[l].astype(BF16),
                      cmp_w1_v[l].reshape(CMP_BLOCK, HEAD_DIM, HEAD_DIM).astype(BF16), cmp_w2_v[l].astype(BF16))
        ya = nsa_attention(proj_a, proj_g, tabs, cmp_params, batch, seq)
        yb = hgrn2(proj_b, lbs[l], hg_norm_g[l], batch, seq)
        yc = multiscale_pool(proj_b, pool_w[l].astype(BF16), pool_scale[l], batch, seq)
        merged = merge_branches(ya, yb, yc, wab, whb, wpb, proj_b, d, l)
        mix = matmul(merged, wob, BF16, layer=l)
        h32, h16 = layer_norm_rows(mix, ln1_g[l], ln1_b[l], res=h32, alpha=DN_ALPHA)
        gate = moe_router(h16, router_w[l], router_b[l])
        hid_r = moe_hidden(h16, gate, w1b, w3b, l)
        hid_s = shared_hidden(h16, ws1b, ws3b, l)
        ffn = moe_down(hid_r, hid_s, w2b, ws2b, l)
        h32, h16 = layer_norm_rows(ffn, ln2_g[l], ln2_b[l], res=h32, alpha=DN_ALPHA)
    return h32.reshape(batch, seq, d)
```

```python
import functools

import numpy as np
import jax
import jax.numpy as jnp
from jax import lax
from jax.experimental import pallas as pl
from jax.experimental.pallas import tpu as pltpu

F32 = jnp.float32
BF16 = jnp.bfloat16

DEPTH = 2
HEAD_DIM = 128
ATTN_HEADS = 16
KV_GROUPS = 2
HEADS_PER_GROUP = ATTN_HEADS // KV_GROUPS
ATTN_WIDTH = ATTN_HEADS * HEAD_DIM
KV_WIDTH = KV_GROUPS * HEAD_DIM
ROPE_DIM = HEAD_DIM // 4
ROPE_THETA = 500000.0
CMP_BLOCK = 32
CMP_STRIDE = 16
SLC_BLOCK = 32
SLC_TOPN = 8
WINDOW = 512
HG_HEADS = 8
HG_DK = 128
HG_DV = 128
HG_KWIDTH = HG_HEADS * HG_DK
HG_WIDTH = HG_HEADS * HG_DV
POOL_WINDOWS = (2, 4, 8, 16)
POOL_GROUP = 256
POOL_WIDTH = POOL_GROUP * len(POOL_WINDOWS)
N_EXPERTS = 64
TOP_K = 8
D_EXPERT = 128
D_SHARED = 256
ROUTE_SCALE = 2.5
DN_ALPHA = (2.0 * DEPTH) ** 0.25
LN_EPS = 1e-5
RMS_EPS = 1e-6
NEG = -1e30

LANES = 128
SUBLANES = 8
VMEM_LIMIT = 56 * 1024 * 1024

COL_KV = ATTN_WIDTH
ATTN_COLS = ATTN_WIDTH + 6 * KV_WIDTH
COL_HG = 0
COL_POOL = COL_HG + 4 * HG_WIDTH
COL_MG = COL_POOL + POOL_WIDTH

TQ = 128
KC = 512
WSPAN = WINDOW + TQ
HC = 128
HG_LEVELS = (64, 32, 16, 8)
HG_STEP_HEADS = 8


def _cparams(*sem):
    return pltpu.CompilerParams(dimension_semantics=sem, vmem_limit_bytes=VMEM_LIMIT)


def _ln_kernel(alpha, has_res, *refs):
    if has_res:
        x_ref, r_ref, g_ref, b_ref, o32_ref, o16_ref = refs
        x = alpha * r_ref[...] + x_ref[...].astype(F32)
    else:
        x_ref, g_ref, b_ref, o32_ref, o16_ref = refs
        x = x_ref[...]
    mu = jnp.mean(x, axis=-1, keepdims=True)
    xc = x - mu
    var = jnp.mean(xc * xc, axis=-1, keepdims=True)
    y = xc * lax.rsqrt(var + LN_EPS) * g_ref[...] + b_ref[...]
    o32_ref[...] = y
    o16_ref[...] = y.astype(BF16)


def layer_norm_rows(x, g, b, res=None, alpha=1.0, tm=256):
    n, d = x.shape
    row = pl.BlockSpec((tm, d), lambda i: (i, 0))
    vec = pl.BlockSpec((1, d), lambda i: (0, 0))
    ins = [x] + ([res] if res is not None else []) + [g.reshape(1, d), b.reshape(1, d)]
    specs = [row] + ([row] if res is not None else []) + [vec, vec]
    return pl.pallas_call(
        functools.partial(_ln_kernel, alpha, res is not None),
        grid=(n // tm,),
        in_specs=specs,
        out_specs=[row, row],
        out_shape=[jax.ShapeDtypeStruct((n, d), F32), jax.ShapeDtypeStruct((n, d), BF16)],
        compiler_params=_cparams("parallel"),
        name="layer_norm",
    )(*ins)


def _mm_kernel(x_ref, w_ref, o_ref):
    o_ref[...] = jnp.dot(x_ref[...], w_ref[...], preferred_element_type=F32).astype(o_ref.dtype)


def matmul(x, w, out_dtype, tm=1024, tn=1024, layer=None, cols=None):
    n, k = x.shape
    m = cols if cols is not None else w.shape[-1]
    tm, tn = min(tm, n), min(tn, m)
    assert m % tn == 0
    if layer is None:
        w_spec = pl.BlockSpec((k, tn), lambda i, j: (0, j))
    else:
        w_spec = pl.BlockSpec((None, k, tn), lambda i, j: (layer, 0, j))
    return pl.pallas_call(
        _mm_kernel,
        grid=(n // tm, m // tn),
        in_specs=[pl.BlockSpec((tm, k), lambda i, j: (i, 0)), w_spec],
        out_specs=pl.BlockSpec((tm, tn), lambda i, j: (i, j)),
        out_shape=jax.ShapeDtypeStruct((n, m), out_dtype),
        compiler_params=_cparams("parallel", "parallel"),
        name="matmul",
    )(x, w)


def _mm_nt_kernel(x_ref, w_ref, o_ref):
    o_ref[...] = lax.dot_general(x_ref[...], w_ref[...], (((1,), (1,)), ((), ())),
                                 preferred_element_type=F32).astype(o_ref.dtype)


def matmul_nt(x, wt, out_dtype, layer, tm=1024, tn=1024, rows=None):
    n, k = x.shape
    m = rows if rows is not None else wt.shape[1]
    tm, tn = min(tm, n), min(tn, m)
    assert m % tn == 0
    return pl.pallas_call(
        _mm_nt_kernel,
        grid=(n // tm, m // tn),
        in_specs=[pl.BlockSpec((tm, k), lambda i, j: (i, 0)),
                  pl.BlockSpec((None, tn, k), lambda i, j: (layer, j, 0))],
        out_specs=pl.BlockSpec((tm, tn), lambda i, j: (i, j)),
        out_shape=jax.ShapeDtypeStruct((n, m), out_dtype),
        compiler_params=_cparams("parallel", "parallel"),
        name="matmul_nt",
    )(x, wt)


def _rope_table_kernel(pos_ref, inv_ref, c_ref, sa_ref, sb_ref):
    ang = pos_ref[...].astype(F32) * inv_ref[...]
    lane = lax.broadcasted_iota(jnp.int32, ang.shape, 1)
    sn = jnp.sin(ang)
    c_ref[...] = jnp.cos(ang)
    sa_ref[...] = jnp.where(lane < ROPE_DIM // 2, -sn, 0.0)
    sb_ref[...] = jnp.where((lane >= ROPE_DIM // 2) & (lane < ROPE_DIM), sn, 0.0)


def rope_tables(positions, tm=1024):
    n = positions.size
    half = ROPE_DIM // 2
    inv = ROPE_THETA ** (-np.arange(half, dtype=np.float32) * 2.0 / ROPE_DIM)
    inv_full = np.zeros((1, LANES), np.float32)
    inv_full[0, :half] = inv
    inv_full[0, half:ROPE_DIM] = inv
    tm = min(tm, n)
    out = jax.ShapeDtypeStruct((n, LANES), F32)
    spec = pl.BlockSpec((tm, LANES), lambda i: (i, 0))
    return pl.pallas_call(
        _rope_table_kernel,
        grid=(n // tm,),
        in_specs=[pl.BlockSpec((tm, 1), lambda i: (i, 0)), pl.BlockSpec((1, LANES), lambda i: (0, 0))],
        out_specs=[spec, spec, spec],
        out_shape=[out, out, out],
        compiler_params=_cparams("parallel"),
        name="rope_tables",
    )(positions.reshape(n, 1), jnp.asarray(inv_full))


def _rope(x, c, sa, sb):
    return x * c + pltpu.roll(x, LANES - ROPE_DIM // 2, 1) * sa + pltpu.roll(x, ROPE_DIM // 2, 1) * sb


def _gelu_tanh(x):
    return 0.5 * x * (1.0 + jnp.tanh(0.7978845608028654 * (x + 0.044715 * x * x * x)))


def _top_mask(val, lane, n_pick, axis=-1):
    sel = jnp.zeros(val.shape, F32)
    for _ in range(n_pick):
        m = jnp.max(val, axis=axis, keepdims=True)
        idx = jnp.min(jnp.where(val == m, lane, LANES), axis=axis, keepdims=True)
        pick = lane == idx
        sel = jnp.where(pick, 1.0, sel)
        val = jnp.where(pick, -jnp.inf, val)
    return sel


def _attn_kernel(q_ref, kc_ref, vc_ref, ks_ref, vs_ref, kw_ref, vw_ref, gate_ref,
                 cq_ref, saq_ref, sbq_ref, ck_ref, sak_ref, sbk_ref,
                 posk_ref, posv_ref, w1k_ref, w2k_ref, w1v_ref, w2v_ref, ov_ref, nege_ref,
                 o_ref, kcmp_s, vcmp_s, kx_s, kwr_s, vsx_s, vwx_s, tmp_s, sca_s, scb_s, m_s, acc_s, sw_s, wbias_s):
    seq = ks_ref.shape[0]
    nh = seq // CMP_STRIDE
    n_cmp = (seq - CMP_BLOCK) // CMP_STRIDE + 1
    hpg = HEADS_PER_GROUP
    scale = HEAD_DIM ** -0.5
    scale2 = scale * 1.4426950408889634
    qt = pl.program_id(2)
    nt = (((1,), (1,)), ((), ()))

    @pl.when(qt == 0)
    def _per_sequence():
        def compress(t_ref, pos_ref, w1_ref, w2_ref, out_s):
            tmp_s[...] = t_ref[...].astype(F32)
            first = jnp.zeros((nh, HEAD_DIM), F32)
            second = jnp.zeros((nh, HEAD_DIM), F32)
            for j in range(CMP_BLOCK):
                x = tmp_s[pl.ds(j % CMP_STRIDE, nh, stride=CMP_STRIDE), :] + pos_ref[j:j + 1, :]
                p = jnp.dot(x.astype(BF16), w1_ref[j], preferred_element_type=F32)
                if j < CMP_STRIDE:
                    first = first + p
                else:
                    second = second + p
            pre = first + pltpu.roll(second, nh - 1, 0)
            hid = _gelu_tanh(pre).astype(BF16)
            out_s[...] = jnp.dot(hid, w2_ref[...], preferred_element_type=F32).astype(BF16)

        compress(kc_ref, posk_ref, w1k_ref, w2k_ref, kcmp_s)
        compress(vc_ref, posv_ref, w1v_ref, w2v_ref, vcmp_s)
        ck, sak, sbk = ck_ref[...], sak_ref[...], sbk_ref[...]
        kx_s[:, :HEAD_DIM] = _rope(ks_ref[...].astype(F32), ck, sak, sbk).astype(BF16)
        kx_s[:, HEAD_DIM:] = nege_ref[...]
        ones = jnp.ones((seq, HEAD_DIM), BF16)
        vsx_s[:, :HEAD_DIM] = vs_ref[...]
        vsx_s[:, HEAD_DIM:] = ones
        vwx_s[:, :HEAD_DIM] = vw_ref[...]
        vwx_s[:, HEAD_DIM:] = ones
        kwr_s[...] = _rope(kw_ref[...].astype(F32), ck, sak, sbk).astype(BF16)

    t0 = qt * TQ
    t = t0 + lax.broadcasted_iota(jnp.int32, (TQ, 1), 0)
    lane = lax.broadcasted_iota(jnp.int32, (TQ, LANES), 1)
    q = q_ref[...]
    qf = q.astype(F32)
    cq, saq, sbq = cq_ref[...], saq_ref[...], sbq_ref[...]
    heads = [slice(h * HEAD_DIM, (h + 1) * HEAD_DIM) for h in range(hpg)]
    q_raw = jnp.concatenate([q[:, s] for s in heads], axis=0)
    q_rot = jnp.concatenate([_rope(qf[:, s], cq, saq, sbq).astype(BF16) for s in heads], axis=0)

    s = lax.dot_general(q_raw, kcmp_s[...], nt, preferred_element_type=F32) * scale
    s3 = s.reshape(hpg, TQ, nh)
    vis = (lane * CMP_STRIDE + (CMP_BLOCK - 1) <= t) & (lane < n_cmp)
    s3 = jnp.where(vis[None], s3, NEG)
    e3 = jnp.exp(s3 - jnp.max(s3, axis=-1, keepdims=True))
    p3 = e3 / jnp.sum(e3, axis=-1, keepdims=True) * vis[None].astype(F32)
    pb = p3.astype(BF16)
    o_c = jnp.dot(pb.reshape(hpg * TQ, nh), vcmp_s[...], preferred_element_type=F32).reshape(hpg, TQ, HEAD_DIM)

    psum = jnp.sum(pb.astype(F32), axis=0)
    imp = lax.dot_general(ov_ref[...], psum, nt, preferred_element_type=F32, precision=lax.Precision.HIGHEST)
    n_slc = imp.shape[0]
    blk = lax.broadcasted_iota(jnp.int32, (n_slc, TQ), 0)
    blk_t = (t0 + lax.broadcasted_iota(jnp.int32, (n_slc, TQ), 1)) // SLC_BLOCK
    causal = blk <= blk_t
    forced = (blk == 0) | (blk == blk_t) | (blk == blk_t - 1)
    val = jnp.where(forced, jnp.inf, jnp.where(causal, imp, -jnp.inf))
    not_sel = jnp.where(causal, 1.0 - _top_mask(val, blk, SLC_TOPN, axis=0), 1.0)
    not_sel = jnp.concatenate([not_sel.T, jnp.zeros((TQ, HEAD_DIM - n_slc), F32)], axis=1).astype(BF16)
    q_ext = jnp.concatenate([q_rot, jnp.concatenate([not_sel] * hpg, axis=0)], axis=1)

    w0 = pl.multiple_of(jnp.maximum(t0 - WINDOW, 0), TQ)
    kk = kwr_s[pl.ds(w0, WSPAN), :]
    vv = vwx_s[pl.ds(w0, WSPAN), :]
    sw_s[...] = lax.dot_general(q_rot, kk, nt, preferred_element_type=F32)
    kpos = w0 + lax.broadcasted_iota(jnp.int32, (TQ, WSPAN), 1)
    wbias_s[...] = jnp.where((kpos <= t) & (t - kpos < WINDOW), 0.0, NEG)
    m_w = jnp.max(sw_s[...].reshape(hpg, TQ, WSPAN) + wbias_s[...][None], axis=-1, keepdims=True)
    pw = jnp.exp2((sw_s[...].reshape(hpg, TQ, WSPAN) + wbias_s[...][None] - m_w) * scale2)
    o_w = jnp.dot(pw.astype(BF16).reshape(hpg * TQ, WSPAN), vv, preferred_element_type=F32)
    o_w = (o_w[:, :HEAD_DIM] / o_w[:, HEAD_DIM:]).reshape(hpg, TQ, HEAD_DIM)

    g = jax.nn.sigmoid(gate_ref[...].astype(F32))
    o_cw = [g[:, h:h + 1] * o_c[h] + g[:, 2 * hpg + h:2 * hpg + h + 1] * o_w[h] for h in range(hpg)]

    n_chunks = (t0 + TQ + KC - 1) // KC
    last = n_chunks - 1
    tri_bias = jnp.where(lane <= lax.broadcasted_iota(jnp.int32, (TQ, TQ), 0), 0.0, NEG)

    def scores(c):
        kk = kx_s[pl.ds(pl.multiple_of(c * KC, KC), KC), :]
        return lax.dot_general(q_ext, kk, nt, preferred_element_type=F32)

    def softmax_pv(sc_ref, c, first):
        vv = vsx_s[pl.ds(pl.multiple_of(c * KC, KC), KC), :]
        m_new = jnp.max(sc_ref[...].reshape(hpg, TQ, KC), axis=-1, keepdims=True)
        if not first:
            m_old = m_s[...].reshape(hpg, TQ, 1)
            m_new = jnp.maximum(m_old, m_new)
        p = jnp.exp2((sc_ref[...].reshape(hpg, TQ, KC) - m_new) * scale2)
        pv = jnp.dot(p.astype(BF16).reshape(hpg * TQ, KC), vv, preferred_element_type=F32)
        if first:
            acc_s[...] = pv
        else:
            a = jnp.exp2((m_old - m_new) * scale2).reshape(hpg * TQ, 1)
            acc_s[...] = a * acc_s[...] + pv
        m_s[...] = m_new.reshape(hpg * TQ, 1)

    sca_s[...] = scores(last)
    own = pl.ds(pl.multiple_of(t0 - last * KC, TQ), TQ)
    for h in range(hpg):
        sca_s[h * TQ:(h + 1) * TQ, own] += tri_bias
    scb_s[...] = scores(0)
    softmax_pv(sca_s, last, True)

    def chunk_pair(i, carry):
        c0 = 2 * i
        sca_s[...] = scores(jnp.minimum(c0 + 1, last))
        softmax_pv(scb_s, c0, False)

        @pl.when(c0 + 1 < last)
        def _():
            scb_s[...] = scores(jnp.minimum(c0 + 2, last))
            softmax_pv(sca_s, c0 + 1, False)

        return carry

    lax.fori_loop(0, (last + 1) // 2, chunk_pair, 0)
    acc = acc_s[...]
    o_s = (acc[:, :HEAD_DIM] / acc[:, HEAD_DIM:]).reshape(hpg, TQ, HEAD_DIM)
    for h in range(hpg):
        o_ref[:, heads[h]] = (o_cw[h] + g[:, hpg + h:hpg + h + 1] * o_s[h]).astype(BF16)


def _overlap_matrix(n_half, n_cmp, n_slc):
    c = np.arange(n_half)[None, :] * CMP_STRIDE
    s = np.arange(n_slc)[:, None] * SLC_BLOCK
    ov = np.clip(np.minimum(c + CMP_BLOCK, s + SLC_BLOCK) - np.maximum(c, s), 0, None) / CMP_STRIDE
    ov[:, n_cmp:] = 0.0
    return ov.astype(np.float32)


def nsa_attention(proj, proj_g, tabs, cmp_params, batch, seq):
    n = batch * seq
    nq = seq // TQ
    nh = seq // CMP_STRIDE
    n_cmp = (seq - CMP_BLOCK) // CMP_STRIDE + 1
    n_slc = seq // SLC_BLOCK
    assert nh == LANES and n_slc <= LANES and seq % KC == 0 and seq >= WSPAN
    c_tab, sa_tab, sb_tab = tabs
    posk, posv, w1k, w2k, w1v, w2v = cmp_params
    ov = jnp.asarray(_overlap_matrix(nh, n_cmp, n_slc))
    neg_onehot = np.where(np.arange(seq)[:, None] // SLC_BLOCK == np.arange(HEAD_DIM)[None, :], NEG, 0.0)
    expand = jnp.asarray(neg_onehot.astype(np.float32), dtype=BF16)

    gw = HEADS_PER_GROUP * HEAD_DIM
    qspec = pl.BlockSpec((TQ, gw), lambda b, g, i: (b * nq + i, g))

    def kvspec(slab):
        return pl.BlockSpec((seq, HEAD_DIM), lambda b, g, i: (b, COL_KV // HEAD_DIM + slab * KV_GROUPS + g))

    gspec = pl.BlockSpec((TQ, LANES), lambda b, g, i: (b * nq + i, g))
    tq_spec = pl.BlockSpec((TQ, LANES), lambda b, g, i: (b * nq + i, 0))
    tk_spec = pl.BlockSpec((seq, LANES), lambda b, g, i: (b, 0))

    def full(a):
        return pl.BlockSpec(a.shape, lambda b, g, i: (0,) * a.ndim)

    consts = [posk, posv, w1k, w2k, w1v, w2v, ov, expand]
    return pl.pallas_call(
        _attn_kernel,
        grid=(batch, KV_GROUPS, nq),
        in_specs=[qspec] + [kvspec(s) for s in range(6)] + [gspec] + [tq_spec] * 3 + [tk_spec] * 3
                 + [full(a) for a in consts],
        out_specs=pl.BlockSpec((TQ, gw), lambda b, g, i: (b * nq + i, g)),
        out_shape=jax.ShapeDtypeStruct((n, ATTN_WIDTH), BF16),
        scratch_shapes=[pltpu.VMEM((nh, HEAD_DIM), BF16), pltpu.VMEM((nh, HEAD_DIM), BF16),
                        pltpu.VMEM((seq, 2 * HEAD_DIM), BF16), pltpu.VMEM((seq, HEAD_DIM), BF16),
                        pltpu.VMEM((seq, 2 * HEAD_DIM), BF16), pltpu.VMEM((seq, 2 * HEAD_DIM), BF16),
                        pltpu.VMEM((seq, HEAD_DIM), F32),
                        pltpu.VMEM((HEADS_PER_GROUP * TQ, KC), F32), pltpu.VMEM((HEADS_PER_GROUP * TQ, KC), F32),
                        pltpu.VMEM((HEADS_PER_GROUP * TQ, 1), F32), pltpu.VMEM((HEADS_PER_GROUP * TQ, 2 * HEAD_DIM), F32),
                        pltpu.VMEM((HEADS_PER_GROUP * TQ, WSPAN), F32), pltpu.VMEM((TQ, WSPAN), F32)],
        compiler_params=_cparams("parallel", "parallel", "arbitrary"),
        name="nsa_attention",
    )(proj, proj, proj, proj, proj, proj, proj, proj_g, c_tab, sa_tab, sb_tab, c_tab, sa_tab, sb_tab, *consts)


def _hgrn_level_tables():
    t = np.arange(HC)[:, None]
    s = np.arange(HC)[None, :]
    lv = np.full((HC, HC), -1, np.int32)
    lv[(t // SUBLANES == s // SUBLANES)] = -1
    for i, m in enumerate(HG_LEVELS):
        ok = ((t // m) % 2 == 1) & (s // m == t // m - 1)
        lv[ok] = i
    tril = (s <= t).astype(np.float32)
    return lv, tril


def _hgrn_head(q_b, f_b, i_b, g_b, lb, ng, lv, tril, st_ref, b_ref):
    nt = (((1,), (1,)), ((), ()))
    tn = (((0,), (0,)), ((), ()))
    f = lb + (1.0 - lb) * jax.nn.sigmoid(f_b.astype(F32))
    logf = jnp.log(f)
    kk = 1.0 - f
    q = q_b.astype(F32)
    v = i_b.astype(F32)
    vb = i_b
    b = jnp.dot(tril, logf, preferred_element_type=F32, precision=lax.Precision.HIGHEST)
    b_ref[...] = b
    row = lax.broadcasted_iota(jnp.int32, (HC, HG_DK), 0)

    st = st_ref[...]
    o = lax.dot_general((q * jnp.exp(b)).astype(BF16), st.astype(BF16), nt, preferred_element_type=F32)

    a = jnp.zeros((HC, HC), F32)
    for i, m in enumerate(HG_LEVELS):
        ref_rows = [jnp.broadcast_to(b_ref[pl.ds((2 * j + 1) * m - 1, 1), :], (2 * m, HG_DK))
                    for j in range(HC // (2 * m))]
        d = b - jnp.concatenate(ref_rows, axis=0)
        odd = (row // m) % 2 == 1
        x = (jnp.where(odd, q, kk) * jnp.exp(-jnp.abs(d))).astype(BF16)
        am = lax.dot_general(x, x, nt, preferred_element_type=F32)
        a = jnp.where(lv == i, am, a)
    o = o + jnp.dot(a.astype(BF16), vb, preferred_element_type=F32)

    nb = HC // SUBLANES
    b3 = b.reshape(nb, SUBLANES, HG_DK)
    q3 = q.reshape(nb, SUBLANES, HG_DK)
    k3 = kk.reshape(nb, SUBLANES, HG_DK)
    v3 = v.reshape(nb, SUBLANES, HG_DV)
    r3 = lax.broadcasted_iota(jnp.int32, (nb, SUBLANES, HG_DK), 1)
    od = jnp.zeros((nb, SUBLANES, HG_DV), F32)
    for j in range(SUBLANES):
        bj = jnp.broadcast_to(b3[:, j:j + 1, :], b3.shape)
        kj = jnp.broadcast_to(k3[:, j:j + 1, :], b3.shape)
        vj = jnp.broadcast_to(v3[:, j:j + 1, :], b3.shape)
        dec = jnp.where(r3 >= j, jnp.exp(jnp.minimum(b3 - bj, 0.0)), 0.0)
        od = od + jnp.sum(q3 * kj * dec, axis=-1, keepdims=True) * vj
    o = o + od.reshape(HC, HG_DV)

    b_last = b_ref[pl.ds(HC - 1, 1), :]
    kd = (kk * jnp.exp(b_last - b)).astype(BF16)
    st_ref[...] = st * jnp.exp(b_last) + lax.dot_general(vb, kd, tn, preferred_element_type=F32)

    o = o * lax.rsqrt(jnp.mean(o * o, axis=-1, keepdims=True) + RMS_EPS) * ng
    gg = g_b.astype(F32)
    return (o * (gg * jax.nn.sigmoid(gg))).astype(BF16)


def _hgrn_kernel(q_ref, f_ref, i_ref, g_ref, lb_ref, ng_ref, lv_ref, tril_ref, o_ref, st_s, b_s):
    @pl.when(pl.program_id(2) == 0)
    def _():
        st_s[...] = jnp.zeros_like(st_s)

    lv, tril = lv_ref[...], tril_ref[...]
    lb, ng = lb_ref[0], ng_ref[0]
    for h in range(HG_STEP_HEADS):
        s = slice(h * HG_DK, (h + 1) * HG_DK)
        o_ref[:, s] = _hgrn_head(q_ref[:, s], f_ref[:, s], i_ref[:, s], g_ref[:, s], lb[:, s], ng[:, s],
                                 lv, tril, st_s.at[h], b_s.at[h])


def hgrn2(proj, lb, norm_g, batch, seq):
    n = batch * seq
    nc = seq // HC
    hs = HG_STEP_HEADS
    wide = hs * HG_DK
    lv, tril = _hgrn_level_tables()
    base = COL_HG // wide

    def slab(k):
        return pl.BlockSpec((HC, wide), lambda b, h, c: (b * nc + c, base + k * (HG_HEADS // hs) + h))

    vec = pl.BlockSpec((1, 1, wide), lambda b, h, c: (h, 0, 0))
    const = pl.BlockSpec((HC, HC), lambda b, h, c: (0, 0))
    return pl.pallas_call(
        _hgrn_kernel,
        grid=(batch, HG_HEADS // hs, nc),
        in_specs=[slab(0), slab(1), slab(2), slab(3), vec, vec, const, const],
        out_specs=pl.BlockSpec((HC, wide), lambda b, h, c: (b * nc + c, h)),
        out_shape=jax.ShapeDtypeStruct((n, HG_WIDTH), BF16),
        scratch_shapes=[pltpu.VMEM((hs, HG_DV, HG_DK), F32), pltpu.VMEM((hs, HC, HG_DK), F32)],
        compiler_params=_cparams("parallel", "parallel", "arbitrary"),
        name="hgrn2",
    )(proj, proj, proj, proj, lb.reshape(HG_HEADS // hs, 1, wide), norm_g.reshape(HG_HEADS // hs, 1, wide),
      jnp.asarray(lv), jnp.asarray(tril))


def _pool_kernel(p_ref, w_ref, sc_ref, o_ref):
    gi = pl.program_id(1)
    x = p_ref[...].astype(F32)
    t = lax.broadcasted_iota(jnp.int32, x.shape, 0)
    acc = x
    for k in range(len(POOL_WINDOWS)):
        sh = 1 << k
        nxt = acc + jnp.where(t >= sh, pltpu.roll(acc, sh, 0), 0.0)
        acc = jnp.where(k <= gi, nxt, acc)
    width = lax.shift_left(jnp.int32(2), gi)
    cnt = jnp.minimum(t + 1, width).astype(F32)
    mixed = acc / cnt - x
    y = jnp.dot(mixed.astype(BF16), w_ref[0], preferred_element_type=F32) * sc_ref[0]
    o_ref[...] = y.astype(BF16)


def multiscale_pool(proj, w_pool, scale, batch, seq):
    n = batch * seq
    ng = len(POOL_WINDOWS)
    base = COL_POOL // POOL_GROUP
    return pl.pallas_call(
        _pool_kernel,
        grid=(batch, ng),
        in_specs=[pl.BlockSpec((seq, POOL_GROUP), lambda b, g: (b, base + g)),
                  pl.BlockSpec((1, POOL_GROUP, POOL_GROUP), lambda b, g: (g, 0, 0)),
                  pl.BlockSpec((1, 1, POOL_GROUP), lambda b, g: (g, 0, 0))],
        out_specs=pl.BlockSpec((seq, POOL_GROUP), lambda b, g: (b, g)),
        out_shape=jax.ShapeDtypeStruct((n, POOL_WIDTH), BF16),
        compiler_params=_cparams("parallel", "parallel"),
        name="multiscale_pool",
    )(proj, w_pool, scale.reshape(ng, 1, POOL_GROUP))


def _merge_kernel(ya_ref, yb_ref, yc_ref, wa_ref, wb_ref, wc_ref, ga_ref, gb_ref, gc_ref, o_ref):
    def branch(y_ref, w_ref, g_ref):
        up = jnp.dot(y_ref[...], w_ref[...], preferred_element_type=F32)
        return jax.nn.sigmoid(g_ref[...].astype(F32)) * up

    o_ref[...] = (branch(ya_ref, wa_ref, ga_ref) + branch(yb_ref, wb_ref, gb_ref)
                  + branch(yc_ref, wc_ref, gc_ref)).astype(o_ref.dtype)


def merge_branches(ya, yb, yc, wa, wb, wc, proj, d_model, layer, tm=1024, tn=512):
    n = ya.shape[0]
    tm = min(tm, n)
    gbase = COL_MG // tn
    step = d_model // tn

    def y_spec(a):
        return pl.BlockSpec((tm, a.shape[1]), lambda i, j: (i, 0))

    def w_spec(a):
        return pl.BlockSpec((None, a.shape[1], tn), lambda i, j: (layer, 0, j))

    def g_spec(k):
        return pl.BlockSpec((tm, tn), lambda i, j: (i, gbase + k * step + j))

    return pl.pallas_call(
        _merge_kernel,
        grid=(n // tm, d_model // tn),
        in_specs=[y_spec(ya), y_spec(yb), y_spec(yc), w_spec(wa), w_spec(wb), w_spec(wc),
                  g_spec(0), g_spec(1), g_spec(2)],
        out_specs=pl.BlockSpec((tm, tn), lambda i, j: (i, j)),
        out_shape=jax.ShapeDtypeStruct((n, d_model), BF16),
        compiler_params=_cparams("parallel", "parallel"),
        name="merge_branches",
    )(ya, yb, yc, wa, wb, wc, proj, proj, proj)


def _router_kernel(x_ref, w_ref, b_ref, o_ref):
    logits = jnp.dot(x_ref[...], w_ref[...], preferred_element_type=F32)
    scores = jax.nn.sigmoid(logits)
    lane = lax.broadcasted_iota(jnp.int32, scores.shape, 1)
    real = lane < N_EXPERTS
    sel = _top_mask(jnp.where(real, scores + b_ref[...], -jnp.inf), lane, TOP_K)
    w = jnp.where(real, sel * scores, 0.0)
    o_ref[...] = w / jnp.sum(w, axis=-1, keepdims=True) * ROUTE_SCALE


def moe_router(x, router_w, router_b, tm=512):
    n, d = x.shape
    tm = min(tm, n)
    w = jnp.zeros((d, LANES), BF16).at[:, :N_EXPERTS].set(router_w.astype(BF16))
    b = jnp.zeros((1, LANES), F32).at[0, :N_EXPERTS].set(router_b.astype(F32))
    return pl.pallas_call(
        _router_kernel,
        grid=(n // tm,),
        in_specs=[pl.BlockSpec((tm, d), lambda i: (i, 0)), pl.BlockSpec((d, LANES), lambda i: (0, 0)),
                  pl.BlockSpec((1, LANES), lambda i: (0, 0))],
        out_specs=pl.BlockSpec((tm, LANES), lambda i: (i, 0)),
        out_shape=jax.ShapeDtypeStruct((n, LANES), F32),
        compiler_params=_cparams("parallel"),
        name="moe_router",
    )(x, w, b)


def _moe_up_kernel(ec, x_ref, w1_ref, w3_ref, gate_ref, ex_ref, o_ref):
    x = x_ref[...]
    w1 = jnp.concatenate([w1_ref[e] for e in range(ec)], axis=1)
    w3 = jnp.concatenate([w3_ref[e] for e in range(ec)], axis=1)
    h1 = jnp.dot(x, w1, preferred_element_type=F32)
    h3 = jnp.dot(x, w3, preferred_element_type=F32)
    g = gate_ref[...]
    g_hi = g.astype(BF16)
    g_lo = (g - g_hi.astype(F32)).astype(BF16)
    gexp = jnp.dot(jnp.concatenate([g_hi, g_lo], axis=1), ex_ref[...], preferred_element_type=F32)
    o_ref[...] = (h1 * jax.nn.sigmoid(h1) * h3 * gexp).astype(BF16)


def moe_hidden(x, gate, w1, w3, layer, tm=1024, ec=4):
    n, d = x.shape
    tm = min(tm, n)
    th = ec * D_EXPERT
    expand = (np.arange(N_EXPERTS * D_EXPERT)[None, :] // D_EXPERT == np.arange(LANES)[:, None])
    expand = jnp.asarray(np.concatenate([expand, expand], axis=0).astype(np.float32), dtype=BF16)
    w_spec = pl.BlockSpec((None, ec, d, D_EXPERT), lambda i, j: (layer, j, 0, 0))
    return pl.pallas_call(
        functools.partial(_moe_up_kernel, ec),
        grid=(n // tm, N_EXPERTS // ec),
        in_specs=[pl.BlockSpec((tm, d), lambda i, j: (i, 0)), w_spec, w_spec,
                  pl.BlockSpec((tm, LANES), lambda i, j: (i, 0)),
                  pl.BlockSpec((2 * LANES, th), lambda i, j: (0, j))],
        out_specs=pl.BlockSpec((tm, th), lambda i, j: (i, j)),
        out_shape=jax.ShapeDtypeStruct((n, N_EXPERTS * D_EXPERT), BF16),
        compiler_params=_cparams("parallel", "parallel"),
        name="moe_hidden",
    )(x, w1, w3, gate, expand)


def _glu_kernel(x_ref, w1_ref, w3_ref, o_ref):
    x = x_ref[...]
    h1 = jnp.dot(x, w1_ref[...], preferred_element_type=F32)
    h3 = jnp.dot(x, w3_ref[...], preferred_element_type=F32)
    o_ref[...] = (h1 * jax.nn.sigmoid(h1) * h3).astype(BF16)


def shared_hidden(x, ws1, ws3, layer, tm=1024):
    n, d = x.shape
    tm = min(tm, n)
    w_spec = pl.BlockSpec((None, d, D_SHARED), lambda i: (layer, 0, 0))
    return pl.pallas_call(
        _glu_kernel,
        grid=(n // tm,),
        in_specs=[pl.BlockSpec((tm, d), lambda i: (i, 0)), w_spec, w_spec],
        out_specs=pl.BlockSpec((tm, D_SHARED), lambda i: (i, 0)),
        out_shape=jax.ShapeDtypeStruct((n, D_SHARED), BF16),
        compiler_params=_cparams("parallel"),
        name="shared_hidden",
    )(x, ws1, ws3)


def _moe_down_kernel(hr_ref, hs_ref, w2_ref, ws2_ref, o_ref):
    y = jnp.dot(hr_ref[...], w2_ref[...], preferred_element_type=F32)
    y = y + jnp.dot(hs_ref[...], ws2_ref[...], preferred_element_type=F32)
    o_ref[...] = y.astype(o_ref.dtype)


def moe_down(hid_r, hid_s, w2, ws2, layer, tm=1024, tn=512):
    n, kr = hid_r.shape
    ks = hid_s.shape[1]
    d = w2.shape[-1]
    tm = min(tm, n)
    return pl.pallas_call(
        _moe_down_kernel,
        grid=(n // tm, d // tn),
        in_specs=[pl.BlockSpec((tm, kr), lambda i, j: (i, 0), pipeline_mode=pl.Buffered(1)),
                  pl.BlockSpec((tm, ks), lambda i, j: (i, 0)),
                  pl.BlockSpec((None, kr, tn), lambda i, j: (layer, 0, j)),
                  pl.BlockSpec((None, ks, tn), lambda i, j: (layer, 0, j))],
        out_specs=pl.BlockSpec((tm, tn), lambda i, j: (i, j)),
        out_shape=jax.ShapeDtypeStruct((n, d), BF16),
        compiler_params=_cparams("parallel", "parallel"),
        name="moe_down",
    )(hid_r, hid_s, w2, ws2)


def _branch_gate_weights(w_t):
    depth, _, d_model = w_t.shape
    ag = w_t[:, ATTN_COLS:ATTN_COLS + 3 * ATTN_HEADS, :].reshape(depth, 3, KV_GROUPS, HEADS_PER_GROUP, d_model)
    ag = jnp.moveaxis(ag, 2, 1).reshape(depth, KV_GROUPS, 3 * HEADS_PER_GROUP, d_model)
    ag = jnp.pad(ag, ((0, 0), (0, 0), (0, LANES - 3 * HEADS_PER_GROUP), (0, 0)))
    return ag.reshape(depth, KV_GROUPS * LANES, d_model)


def _hgrn_lower_bounds(logits):
    lbs = jnp.cumsum(jax.nn.softmax(logits.astype(F32), axis=0), axis=0)
    return lbs - lbs[0:1]


def kernel(x, positions, ln_in_g, ln_in_b, w_in, cmp_pos_k, cmp_pos_v, cmp_w1_k, cmp_w2_k, cmp_w1_v, cmp_w2_v,
           hg_lb_logits, hg_norm_g, pool_w, pool_scale, w_up_attn, w_up_hg, w_up_pool, w_o, ln1_g, ln1_b,
           router_w, router_b, w1, w3, w2, ws1, ws3, ws2, ln2_g, ln2_b):
    batch, seq, d = x.shape
    n = batch * seq
    depth = w_in.shape[0]
    lbs = _hgrn_lower_bounds(hg_lb_logits)
    tabs = rope_tables(positions)
    w_in_t = jnp.swapaxes(w_in, 1, 2).astype(BF16)
    w_tail_t = w_in_t[:, ATTN_COLS + 3 * ATTN_HEADS:, :]
    w_gate_t = _branch_gate_weights(w_in_t)
    w1b, w3b = w1.astype(BF16), w3.astype(BF16)
    w2b = w2.astype(BF16).reshape(depth, N_EXPERTS * D_EXPERT, d)
    ws1b, ws3b, ws2b = ws1.astype(BF16), ws3.astype(BF16), ws2.astype(BF16)
    wab, whb, wpb, wob = w_up_attn.astype(BF16), w_up_hg.astype(BF16), w_up_pool.astype(BF16), w_o.astype(BF16)
    h32, h16 = layer_norm_rows(x.reshape(n, d), ln_in_g, ln_in_b)
    for l in range(depth):
        proj_a = matmul_nt(h16, w_in_t, BF16, l, tm=512, tn=ATTN_COLS // 2, rows=ATTN_COLS)
        proj_g = matmul_nt(h16, w_gate_t, BF16, l)
        proj_b = matmul_nt(h16, w_tail_t, BF16, l)
        cmp_params = (cmp_pos_k[l], cmp_pos_v[l],
                      cmp_w1_k[l].reshape(CMP_BLOCK, HEAD_DIM, HEAD_DIM).astype(BF16), cmp_w2_k[l].astype(BF16),
                      cmp_w1_v[l].reshape(CMP_BLOCK, HEAD_DIM, HEAD_DIM).astype(BF16), cmp_w2_v[l].astype(BF16))
        ya = nsa_attention(proj_a, proj_g, tabs, cmp_params, batch, seq)
        yb = hgrn2(proj_b, lbs[l], hg_norm_g[l], batch, seq)
        yc = multiscale_pool(proj_b, pool_w[l].astype(BF16), pool_scale[l], batch, seq)
        merged = merge_branches(ya, yb, yc, wab, whb, wpb, proj_b, d, l)
        mix = matmul(merged, wob, BF16, layer=l)
        h32, h16 = layer_norm_rows(mix, ln1_g[l], ln1_b[l], res=h32, alpha=DN_ALPHA)
        gate = moe_router(h16, router_w[l], router_b[l])
        hid_r = moe_hidden(h16, gate, w1b, w3b, l)
        hid_s = shared_hidden(h16, ws1b, ws3b, l)
        ffn = moe_down(hid_r, hid_s, w2b, ws2b, l)
        h32, h16 = layer_norm_rows(ffn, ln2_g[l], ln2_b[l], res=h32, alpha=DN_ALPHA)
    return h32.reshape(batch, seq, d)
```

```python
import functools

import numpy as np
import jax
import jax.numpy as jnp
from jax import lax
from jax.experimental import pallas as pl
from jax.experimental.pallas import tpu as pltpu

F32 = jnp.float32
BF16 = jnp.bfloat16

DEPTH = 2
HEAD_DIM = 128
ATTN_HEADS = 16
KV_GROUPS = 2
HEADS_PER_GROUP = ATTN_HEADS // KV_GROUPS
ATTN_WIDTH = ATTN_HEADS * HEAD_DIM
KV_WIDTH = KV_GROUPS * HEAD_DIM
ROPE_DIM = HEAD_DIM // 4
ROPE_THETA = 500000.0
CMP_BLOCK = 32
CMP_STRIDE = 16
SLC_BLOCK = 32
SLC_TOPN = 8
WINDOW = 512
HG_HEADS = 8
HG_DK = 128
HG_DV = 128
HG_KWIDTH = HG_HEADS * HG_DK
HG_WIDTH = HG_HEADS * HG_DV
POOL_WINDOWS = (2, 4, 8, 16)
POOL_GROUP = 256
POOL_WIDTH = POOL_GROUP * len(POOL_WINDOWS)
N_EXPERTS = 64
TOP_K = 8
D_EXPERT = 128
D_SHARED = 256
ROUTE_SCALE = 2.5
DN_ALPHA = (2.0 * DEPTH) ** 0.25
LN_EPS = 1e-5
RMS_EPS = 1e-6
NEG = -1e30

LANES = 128
SUBLANES = 8
BF16_SUBLANES = 16
VMEM_LIMIT = 56 * 1024 * 1024

COL_KV = ATTN_WIDTH
ATTN_COLS = ATTN_WIDTH + 6 * KV_WIDTH
COL_HG = 0
COL_POOL = COL_HG + 4 * HG_WIDTH
COL_MG = COL_POOL + POOL_WIDTH

TQ = 256
KC = 512
WSPAN = WINDOW + TQ
HC = 128
HG_LEVELS = (64, 32, 16, 8)
HG_STEP_HEADS = 8


def _cparams(*sem):
    return pltpu.CompilerParams(dimension_semantics=sem, vmem_limit_bytes=VMEM_LIMIT)


def _ln_kernel(alpha, has_res, *refs):
    if has_res:
        x_ref, r_ref, g_ref, b_ref, o32_ref, o16_ref = refs
        x = alpha * r_ref[...] + x_ref[...].astype(F32)
    else:
        x_ref, g_ref, b_ref, o32_ref, o16_ref = refs
        x = x_ref[...]
    mu = jnp.mean(x, axis=-1, keepdims=True)
    xc = x - mu
    var = jnp.mean(xc * xc, axis=-1, keepdims=True)
    y = xc * lax.rsqrt(var + LN_EPS) * g_ref[...] + b_ref[...]
    o32_ref[...] = y
    o16_ref[...] = y.astype(BF16)


def layer_norm_rows(x, g, b, res=None, alpha=1.0, tm=256):
    n, d = x.shape
    row = pl.BlockSpec((tm, d), lambda i: (i, 0))
    vec = pl.BlockSpec((1, d), lambda i: (0, 0))
    ins = [x] + ([res] if res is not None else []) + [g.reshape(1, d), b.reshape(1, d)]
    specs = [row] + ([row] if res is not None else []) + [vec, vec]
    return pl.pallas_call(
        functools.partial(_ln_kernel, alpha, res is not None),
        grid=(n // tm,),
        in_specs=specs,
        out_specs=[row, row],
        out_shape=[jax.ShapeDtypeStruct((n, d), F32), jax.ShapeDtypeStruct((n, d), BF16)],
        compiler_params=_cparams("parallel"),
        name="layer_norm",
    )(*ins)


def _mm_kernel(x_ref, w_ref, o_ref):
    o_ref[...] = jnp.dot(x_ref[...], w_ref[...], preferred_element_type=F32).astype(o_ref.dtype)


def matmul(x, w, out_dtype, tm=1024, tn=1024, layer=None, cols=None):
    n, k = x.shape
    m = cols if cols is not None else w.shape[-1]
    tm, tn = min(tm, n), min(tn, m)
    assert m % tn == 0
    if layer is None:
        w_spec = pl.BlockSpec((k, tn), lambda i, j: (0, j))
    else:
        w_spec = pl.BlockSpec((None, k, tn), lambda i, j: (layer, 0, j))
    return pl.pallas_call(
        _mm_kernel,
        grid=(n // tm, m // tn),
        in_specs=[pl.BlockSpec((tm, k), lambda i, j: (i, 0)), w_spec],
        out_specs=pl.BlockSpec((tm, tn), lambda i, j: (i, j)),
        out_shape=jax.ShapeDtypeStruct((n, m), out_dtype),
        compiler_params=_cparams("parallel", "parallel"),
        name="matmul",
    )(x, w)


def _mm_nt_kernel(x_ref, w_ref, o_ref):
    o_ref[...] = lax.dot_general(x_ref[...], w_ref[0], (((1,), (1,)), ((), ())),
                                 preferred_element_type=F32).astype(o_ref.dtype)


def matmul_nt(x, wt, out_dtype, layer, tm=1024, tn=1024, row0=0, rows=None):
    n, k = x.shape
    m = rows if rows is not None else wt.shape[1] - row0
    tm, tn = min(tm, n), min(tn, m)
    assert m % tn == 0 and row0 % BF16_SUBLANES == 0
    return pl.pallas_call(
        _mm_nt_kernel,
        grid=(n // tm, m // tn),
        in_specs=[pl.BlockSpec((tm, k), lambda i, j: (i, 0)),
                  pl.BlockSpec((pl.Element(1), pl.Element(tn), pl.Element(k)),
                               lambda i, j: (layer, pl.multiple_of(row0 + j * tn, BF16_SUBLANES), 0))],
        out_specs=pl.BlockSpec((tm, tn), lambda i, j: (i, j)),
        out_shape=jax.ShapeDtypeStruct((n, m), out_dtype),
        compiler_params=_cparams("parallel", "parallel"),
        name="matmul_nt",
    )(x, wt)


def _rope_table_kernel(pos_ref, inv_ref, c_ref, sa_ref, sb_ref):
    ang = pos_ref[...].astype(F32) * inv_ref[...]
    lane = lax.broadcasted_iota(jnp.int32, ang.shape, 1)
    sn = jnp.sin(ang)
    c_ref[...] = jnp.cos(ang)
    sa_ref[...] = jnp.where(lane < ROPE_DIM // 2, -sn, 0.0)
    sb_ref[...] = jnp.where((lane >= ROPE_DIM // 2) & (lane < ROPE_DIM), sn, 0.0)


def rope_tables(positions, tm=1024):
    n = positions.size
    half = ROPE_DIM // 2
    inv = ROPE_THETA ** (-np.arange(half, dtype=np.float32) * 2.0 / ROPE_DIM)
    inv_full = np.zeros((1, LANES), np.float32)
    inv_full[0, :half] = inv
    inv_full[0, half:ROPE_DIM] = inv
    tm = min(tm, n)
    out = jax.ShapeDtypeStruct((n, LANES), F32)
    spec = pl.BlockSpec((tm, LANES), lambda i: (i, 0))
    return pl.pallas_call(
        _rope_table_kernel,
        grid=(n // tm,),
        in_specs=[pl.BlockSpec((tm, 1), lambda i: (i, 0)), pl.BlockSpec((1, LANES), lambda i: (0, 0))],
        out_specs=[spec, spec, spec],
        out_shape=[out, out, out],
        compiler_params=_cparams("parallel"),
        name="rope_tables",
    )(positions.reshape(n, 1), jnp.asarray(inv_full))


def _rope(x, c, sa, sb):
    return x * c + pltpu.roll(x, LANES - ROPE_DIM // 2, 1) * sa + pltpu.roll(x, ROPE_DIM // 2, 1) * sb


def _gelu_tanh(x):
    return 0.5 * x * (1.0 + jnp.tanh(0.7978845608028654 * (x + 0.044715 * x * x * x)))


def _top_mask(val, lane, n_pick, axis=-1):
    sel = jnp.zeros(val.shape, F32)
    for _ in range(n_pick):
        m = jnp.max(val, axis=axis, keepdims=True)
        idx = jnp.min(jnp.where(val == m, lane, LANES), axis=axis, keepdims=True)
        pick = lane == idx
        sel = jnp.where(pick, 1.0, sel)
        val = jnp.where(pick, -jnp.inf, val)
    return sel


def _attn_kernel(q_ref, kc_ref, vc_ref, ks_ref, vs_ref, kw_ref, vw_ref, gate_ref,
                 cq_ref, saq_ref, sbq_ref, ck_ref, sak_ref, sbk_ref,
                 posk_ref, posv_ref, w1k_ref, w2k_ref, w1v_ref, w2v_ref, ov_ref, nege_ref,
                 o_ref, kcmp_s, vcmp_s, kx_s, kwr_s, vsx_s, vwx_s, tmp_s, sca_s, scb_s, m_s, acc_s, sw_s, wbias_s):
    seq = ks_ref.shape[0]
    nh = seq // CMP_STRIDE
    n_cmp = (seq - CMP_BLOCK) // CMP_STRIDE + 1
    hpg = HEADS_PER_GROUP
    scale = HEAD_DIM ** -0.5
    scale2 = scale * 1.4426950408889634
    qt = pl.program_id(2)
    nt = (((1,), (1,)), ((), ()))

    @pl.when(qt == 0)
    def _per_sequence():
        def compress(t_ref, pos_ref, w1_ref, w2_ref, out_s):
            tmp_s[...] = t_ref[...].astype(F32)
            first = jnp.zeros((nh, HEAD_DIM), F32)
            second = jnp.zeros((nh, HEAD_DIM), F32)
            for j in range(CMP_BLOCK):
                x = tmp_s[pl.ds(j % CMP_STRIDE, nh, stride=CMP_STRIDE), :] + pos_ref[j:j + 1, :]
                p = jnp.dot(x.astype(BF16), w1_ref[j], preferred_element_type=F32)
                if j < CMP_STRIDE:
                    first = first + p
                else:
                    second = second + p
            pre = first + pltpu.roll(second, nh - 1, 0)
            hid = _gelu_tanh(pre).astype(BF16)
            out_s[...] = jnp.dot(hid, w2_ref[...], preferred_element_type=F32).astype(BF16)

        compress(kc_ref, posk_ref, w1k_ref, w2k_ref, kcmp_s)
        compress(vc_ref, posv_ref, w1v_ref, w2v_ref, vcmp_s)
        ck, sak, sbk = ck_ref[...], sak_ref[...], sbk_ref[...]
        kx_s[:, :HEAD_DIM] = _rope(ks_ref[...].astype(F32), ck, sak, sbk).astype(BF16)
        kx_s[:, HEAD_DIM:] = nege_ref[...]
        ones = jnp.ones((seq, HEAD_DIM), BF16)
        vsx_s[:, :HEAD_DIM] = vs_ref[...]
        vsx_s[:, HEAD_DIM:] = ones
        vwx_s[:, :HEAD_DIM] = vw_ref[...]
        vwx_s[:, HEAD_DIM:] = ones
        kwr_s[...] = _rope(kw_ref[...].astype(F32), ck, sak, sbk).astype(BF16)

    t0 = qt * TQ
    t = t0 + lax.broadcasted_iota(jnp.int32, (TQ, 1), 0)
    lane = lax.broadcasted_iota(jnp.int32, (TQ, LANES), 1)
    q = q_ref[...]
    qf = q.astype(F32)
    cq, saq, sbq = cq_ref[...], saq_ref[...], sbq_ref[...]
    heads = [slice(h * HEAD_DIM, (h + 1) * HEAD_DIM) for h in range(hpg)]
    q_raw = jnp.concatenate([q[:, s] for s in heads], axis=0)
    q_rot = jnp.concatenate([_rope(qf[:, s], cq, saq, sbq).astype(BF16) for s in heads], axis=0)

    s = lax.dot_general(q_raw, kcmp_s[...], nt, preferred_element_type=F32) * scale
    s3 = s.reshape(hpg, TQ, nh)
    vis = (lane * CMP_STRIDE + (CMP_BLOCK - 1) <= t) & (lane < n_cmp)
    s3 = jnp.where(vis[None], s3, NEG)
    e3 = jnp.exp(s3 - jnp.max(s3, axis=-1, keepdims=True))
    p3 = e3 / jnp.sum(e3, axis=-1, keepdims=True) * vis[None].astype(F32)
    pb = p3.astype(BF16)
    o_c = jnp.dot(pb.reshape(hpg * TQ, nh), vcmp_s[...], preferred_element_type=F32).reshape(hpg, TQ, HEAD_DIM)

    psum = jnp.sum(pb.astype(F32), axis=0)
    imp = lax.dot_general(ov_ref[...], psum, nt, preferred_element_type=F32, precision=lax.Precision.HIGHEST)
    n_slc = imp.shape[0]
    blk = lax.broadcasted_iota(jnp.int32, (n_slc, TQ), 0)
    blk_t = (t0 + lax.broadcasted_iota(jnp.int32, (n_slc, TQ), 1)) // SLC_BLOCK
    causal = blk <= blk_t
    forced = (blk == 0) | (blk == blk_t) | (blk == blk_t - 1)
    val = jnp.where(forced, jnp.inf, jnp.where(causal, imp, -jnp.inf))
    not_sel = jnp.where(causal, 1.0 - _top_mask(val, blk, SLC_TOPN, axis=0), 1.0)
    not_sel = jnp.concatenate([not_sel.T, jnp.zeros((TQ, HEAD_DIM - n_slc), F32)], axis=1).astype(BF16)
    q_ext = jnp.concatenate([q_rot, jnp.concatenate([not_sel] * hpg, axis=0)], axis=1)

    w0 = pl.multiple_of(jnp.maximum(t0 - WINDOW, 0), TQ)
    kk = kwr_s[pl.ds(w0, WSPAN), :]
    vv = vwx_s[pl.ds(w0, WSPAN), :]
    sw_s[...] = lax.dot_general(q_rot, kk, nt, preferred_element_type=F32)
    kpos = w0 + lax.broadcasted_iota(jnp.int32, (TQ, WSPAN), 1)
    wbias_s[...] = jnp.where((kpos <= t) & (t - kpos < WINDOW), 0.0, NEG)
    m_w = jnp.max(sw_s[...].reshape(hpg, TQ, WSPAN) + wbias_s[...][None], axis=-1, keepdims=True)
    pw = jnp.exp2((sw_s[...].reshape(hpg, TQ, WSPAN) + wbias_s[...][None] - m_w) * scale2)
    o_w = jnp.dot(pw.astype(BF16).reshape(hpg * TQ, WSPAN), vv, preferred_element_type=F32)
    o_w = (o_w[:, :HEAD_DIM] / o_w[:, HEAD_DIM:]).reshape(hpg, TQ, HEAD_DIM)

    g = jax.nn.sigmoid(gate_ref[...].astype(F32))
    o_cw = [g[:, h:h + 1] * o_c[h] + g[:, 2 * hpg + h:2 * hpg + h + 1] * o_w[h] for h in range(hpg)]

    n_chunks = (t0 + TQ + KC - 1) // KC
    last = n_chunks - 1
    tri_bias = jnp.where(lax.broadcasted_iota(jnp.int32, (TQ, TQ), 1) <= lax.broadcasted_iota(jnp.int32, (TQ, TQ), 0),
                         0.0, NEG)

    def scores(c):
        kk = kx_s[pl.ds(pl.multiple_of(c * KC, KC), KC), :]
        return lax.dot_general(q_ext, kk, nt, preferred_element_type=F32)

    def softmax_pv(sc_ref, c, first):
        vv = vsx_s[pl.ds(pl.multiple_of(c * KC, KC), KC), :]
        m_new = jnp.max(sc_ref[...].reshape(hpg, TQ, KC), axis=-1, keepdims=True)
        if not first:
            m_old = m_s[...].reshape(hpg, TQ, 1)
            m_new = jnp.maximum(m_old, m_new)
        p = jnp.exp2((sc_ref[...].reshape(hpg, TQ, KC) - m_new) * scale2)
        pv = jnp.dot(p.astype(BF16).reshape(hpg * TQ, KC), vv, preferred_element_type=F32)
        if first:
            acc_s[...] = pv
        else:
            a = jnp.exp2((m_old - m_new) * scale2).reshape(hpg * TQ, 1)
            acc_s[...] = a * acc_s[...] + pv
        m_s[...] = m_new.reshape(hpg * TQ, 1)

    sca_s[...] = scores(last)
    own = pl.ds(pl.multiple_of(t0 - last * KC, TQ), TQ)
    for h in range(hpg):
        sca_s[h * TQ:(h + 1) * TQ, own] += tri_bias
    scb_s[...] = scores(0)
    softmax_pv(sca_s, last, True)

    def chunk_pair(i, carry):
        c0 = 2 * i
        sca_s[...] = scores(jnp.minimum(c0 + 1, last))
        softmax_pv(scb_s, c0, False)

        @pl.when(c0 + 1 < last)
        def _():
            scb_s[...] = scores(jnp.minimum(c0 + 2, last))
            softmax_pv(sca_s, c0 + 1, False)

        return carry

    lax.fori_loop(0, (last + 1) // 2, chunk_pair, 0)
    acc = acc_s[...]
    o_s = (acc[:, :HEAD_DIM] / acc[:, HEAD_DIM:]).reshape(hpg, TQ, HEAD_DIM)
    for h in range(hpg):
        o_ref[:, heads[h]] = (o_cw[h] + g[:, hpg + h:hpg + h + 1] * o_s[h]).astype(BF16)


def _overlap_matrix(n_half, n_cmp, n_slc):
    c = np.arange(n_half)[None, :] * CMP_STRIDE
    s = np.arange(n_slc)[:, None] * SLC_BLOCK
    ov = np.clip(np.minimum(c + CMP_BLOCK, s + SLC_BLOCK) - np.maximum(c, s), 0, None) / CMP_STRIDE
    ov[:, n_cmp:] = 0.0
    return ov.astype(np.float32)


def nsa_attention(proj, proj_g, tabs, cmp_params, batch, seq):
    n = batch * seq
    nq = seq // TQ
    nh = seq // CMP_STRIDE
    n_cmp = (seq - CMP_BLOCK) // CMP_STRIDE + 1
    n_slc = seq // SLC_BLOCK
    assert nh == LANES and n_slc <= LANES and seq % KC == 0 and seq >= WSPAN
    c_tab, sa_tab, sb_tab = tabs
    posk, posv, w1k, w2k, w1v, w2v = cmp_params
    ov = jnp.asarray(_overlap_matrix(nh, n_cmp, n_slc))
    neg_onehot = np.where(np.arange(seq)[:, None] // SLC_BLOCK == np.arange(HEAD_DIM)[None, :], NEG, 0.0)
    expand = jnp.asarray(neg_onehot.astype(np.float32), dtype=BF16)

    gw = HEADS_PER_GROUP * HEAD_DIM
    qspec = pl.BlockSpec((TQ, gw), lambda b, g, i: (b * nq + i, g))

    def kvspec(slab):
        return pl.BlockSpec((seq, HEAD_DIM), lambda b, g, i: (b, COL_KV // HEAD_DIM + slab * KV_GROUPS + g))

    gspec = pl.BlockSpec((TQ, LANES), lambda b, g, i: (b * nq + i, g))
    tq_spec = pl.BlockSpec((TQ, LANES), lambda b, g, i: (b * nq + i, 0))
    tk_spec = pl.BlockSpec((seq, LANES), lambda b, g, i: (b, 0))

    def full(a):
        return pl.BlockSpec(a.shape, lambda b, g, i: (0,) * a.ndim)

    consts = [posk, posv, w1k, w2k, w1v, w2v, ov, expand]
    return pl.pallas_call(
        _attn_kernel,
        grid=(batch, KV_GROUPS, nq),
        in_specs=[qspec] + [kvspec(s) for s in range(6)] + [gspec] + [tq_spec] * 3 + [tk_spec] * 3
                 + [full(a) for a in consts],
        out_specs=pl.BlockSpec((TQ, gw), lambda b, g, i: (b * nq + i, g)),
        out_shape=jax.ShapeDtypeStruct((n, ATTN_WIDTH), BF16),
        scratch_shapes=[pltpu.VMEM((nh, HEAD_DIM), BF16), pltpu.VMEM((nh, HEAD_DIM), BF16),
                        pltpu.VMEM((seq, 2 * HEAD_DIM), BF16), pltpu.VMEM((seq, HEAD_DIM), BF16),
                        pltpu.VMEM((seq, 2 * HEAD_DIM), BF16), pltpu.VMEM((seq, 2 * HEAD_DIM), BF16),
                        pltpu.VMEM((seq, HEAD_DIM), F32),
                        pltpu.VMEM((HEADS_PER_GROUP * TQ, KC), F32), pltpu.VMEM((HEADS_PER_GROUP * TQ, KC), F32),
                        pltpu.VMEM((HEADS_PER_GROUP * TQ, 1), F32), pltpu.VMEM((HEADS_PER_GROUP * TQ, 2 * HEAD_DIM), F32),
                        pltpu.VMEM((HEADS_PER_GROUP * TQ, WSPAN), F32), pltpu.VMEM((TQ, WSPAN), F32)],
        compiler_params=_cparams("parallel", "parallel", "arbitrary"),
        name="nsa_attention",
    )(proj, proj, proj, proj, proj, proj, proj, proj_g, c_tab, sa_tab, sb_tab, c_tab, sa_tab, sb_tab, *consts)


def _hgrn_level_tables():
    t = np.arange(HC)[:, None]
    s = np.arange(HC)[None, :]
    lv = np.full((HC, HC), -1, np.int32)
    lv[(t // SUBLANES == s // SUBLANES)] = -1
    for i, m in enumerate(HG_LEVELS):
        ok = ((t // m) % 2 == 1) & (s // m == t // m - 1)
        lv[ok] = i
    tril = (s <= t).astype(np.float32)
    return lv, tril


def _hgrn_head(q_b, f_b, i_b, g_b, lb, ng, lv, tril, st_ref, b_ref):
    nt = (((1,), (1,)), ((), ()))
    tn = (((0,), (0,)), ((), ()))
    f = lb + (1.0 - lb) * jax.nn.sigmoid(f_b.astype(F32))
    logf = jnp.log(f)
    kk = 1.0 - f
    q = q_b.astype(F32)
    v = i_b.astype(F32)
    vb = i_b
    b = jnp.dot(tril, logf, preferred_element_type=F32, precision=lax.Precision.HIGHEST)
    b_ref[...] = b
    row = lax.broadcasted_iota(jnp.int32, (HC, HG_DK), 0)

    st = st_ref[...]
    o = lax.dot_general((q * jnp.exp(b)).astype(BF16), st.astype(BF16), nt, preferred_element_type=F32)

    a = jnp.zeros((HC, HC), F32)
    for i, m in enumerate(HG_LEVELS):
        ref_rows = [jnp.broadcast_to(b_ref[pl.ds((2 * j + 1) * m - 1, 1), :], (2 * m, HG_DK))
                    for j in range(HC // (2 * m))]
        d = b - jnp.concatenate(ref_rows, axis=0)
        odd = (row // m) % 2 == 1
        x = (jnp.where(odd, q, kk) * jnp.exp(-jnp.abs(d))).astype(BF16)
        am = lax.dot_general(x, x, nt, preferred_element_type=F32)
        a = jnp.where(lv == i, am, a)
    o = o + jnp.dot(a.astype(BF16), vb, preferred_element_type=F32)

    nb = HC // SUBLANES
    b3 = b.reshape(nb, SUBLANES, HG_DK)
    q3 = q.reshape(nb, SUBLANES, HG_DK)
    k3 = kk.reshape(nb, SUBLANES, HG_DK)
    v3 = v.reshape(nb, SUBLANES, HG_DV)
    r3 = lax.broadcasted_iota(jnp.int32, (nb, SUBLANES, HG_DK), 1)
    od = jnp.zeros((nb, SUBLANES, HG_DV), F32)
    for j in range(SUBLANES):
        bj = jnp.broadcast_to(b3[:, j:j + 1, :], b3.shape)
        kj = jnp.broadcast_to(k3[:, j:j + 1, :], b3.shape)
        vj = jnp.broadcast_to(v3[:, j:j + 1, :], b3.shape)
        dec = jnp.where(r3 >= j, jnp.exp(jnp.minimum(b3 - bj, 0.0)), 0.0)
        od = od + jnp.sum(q3 * kj * dec, axis=-1, keepdims=True) * vj
    o = o + od.reshape(HC, HG_DV)

    b_last = b_ref[pl.ds(HC - 1, 1), :]
    kd = (kk * jnp.exp(b_last - b)).astype(BF16)
    st_ref[...] = st * jnp.exp(b_last) + lax.dot_general(vb, kd, tn, preferred_element_type=F32)

    o = o * lax.rsqrt(jnp.mean(o * o, axis=-1, keepdims=True) + RMS_EPS) * ng
    gg = g_b.astype(F32)
    return (o * (gg * jax.nn.sigmoid(gg))).astype(BF16)


def _hgrn_kernel(q_ref, f_ref, i_ref, g_ref, lb_ref, ng_ref, lv_ref, tril_ref, o_ref, st_s, b_s):
    @pl.when(pl.program_id(2) == 0)
    def _():
        st_s[...] = jnp.zeros_like(st_s)

    lv, tril = lv_ref[...], tril_ref[...]
    lb, ng = lb_ref[0], ng_ref[0]
    for h in range(HG_STEP_HEADS):
        s = slice(h * HG_DK, (h + 1) * HG_DK)
        o_ref[:, s] = _hgrn_head(q_ref[:, s], f_ref[:, s], i_ref[:, s], g_ref[:, s], lb[:, s], ng[:, s],
                                 lv, tril, st_s.at[h], b_s.at[h])


def hgrn2(proj, lb, norm_g, batch, seq):
    n = batch * seq
    nc = seq // HC
    hs = HG_STEP_HEADS
    wide = hs * HG_DK
    lv, tril = _hgrn_level_tables()
    base = COL_HG // wide

    def slab(k):
        return pl.BlockSpec((HC, wide), lambda b, h, c: (b * nc + c, base + k * (HG_HEADS // hs) + h))

    vec = pl.BlockSpec((1, 1, wide), lambda b, h, c: (h, 0, 0))
    const = pl.BlockSpec((HC, HC), lambda b, h, c: (0, 0))
    return pl.pallas_call(
        _hgrn_kernel,
        grid=(batch, HG_HEADS // hs, nc),
        in_specs=[slab(0), slab(1), slab(2), slab(3), vec, vec, const, const],
        out_specs=pl.BlockSpec((HC, wide), lambda b, h, c: (b * nc + c, h)),
        out_shape=jax.ShapeDtypeStruct((n, HG_WIDTH), BF16),
        scratch_shapes=[pltpu.VMEM((hs, HG_DV, HG_DK), F32), pltpu.VMEM((hs, HC, HG_DK), F32)],
        compiler_params=_cparams("parallel", "parallel", "arbitrary"),
        name="hgrn2",
    )(proj, proj, proj, proj, lb.reshape(HG_HEADS // hs, 1, wide), norm_g.reshape(HG_HEADS // hs, 1, wide),
      jnp.asarray(lv), jnp.asarray(tril))


def _pool_kernel(p_ref, w_ref, sc_ref, o_ref):
    gi = pl.program_id(1)
    x = p_ref[...].astype(F32)
    t = lax.broadcasted_iota(jnp.int32, x.shape, 0)
    acc = x
    for k in range(len(POOL_WINDOWS)):
        sh = 1 << k
        nxt = acc + jnp.where(t >= sh, pltpu.roll(acc, sh, 0), 0.0)
        acc = jnp.where(k <= gi, nxt, acc)
    width = lax.shift_left(jnp.int32(2), gi)
    cnt = jnp.minimum(t + 1, width).astype(F32)
    mixed = acc / cnt - x
    y = jnp.dot(mixed.astype(BF16), w_ref[0], preferred_element_type=F32) * sc_ref[0]
    o_ref[...] = y.astype(BF16)


def multiscale_pool(proj, w_pool, scale, batch, seq):
    n = batch * seq
    ng = len(POOL_WINDOWS)
    base = COL_POOL // POOL_GROUP
    return pl.pallas_call(
        _pool_kernel,
        grid=(batch, ng),
        in_specs=[pl.BlockSpec((seq, POOL_GROUP), lambda b, g: (b, base + g)),
                  pl.BlockSpec((1, POOL_GROUP, POOL_GROUP), lambda b, g: (g, 0, 0)),
                  pl.BlockSpec((1, 1, POOL_GROUP), lambda b, g: (g, 0, 0))],
        out_specs=pl.BlockSpec((seq, POOL_GROUP), lambda b, g: (b, g)),
        out_shape=jax.ShapeDtypeStruct((n, POOL_WIDTH), BF16),
        compiler_params=_cparams("parallel", "parallel"),
        name="multiscale_pool",
    )(proj, w_pool, scale.reshape(ng, 1, POOL_GROUP))


def _merge_kernel(ya_ref, yb_ref, yc_ref, wa_ref, wb_ref, wc_ref, ga_ref, gb_ref, gc_ref, o_ref):
    def branch(y_ref, w_ref, g_ref):
        up = jnp.dot(y_ref[...], w_ref[...], preferred_element_type=F32)
        return jax.nn.sigmoid(g_ref[...].astype(F32)) * up

    o_ref[...] = (branch(ya_ref, wa_ref, ga_ref) + branch(yb_ref, wb_ref, gb_ref)
                  + branch(yc_ref, wc_ref, gc_ref)).astype(o_ref.dtype)


def merge_branches(ya, yb, yc, wa, wb, wc, proj, d_model, layer, tm=1024, tn=512):
    n = ya.shape[0]
    tm = min(tm, n)
    gbase = COL_MG // tn
    step = d_model // tn

    def y_spec(a):
        return pl.BlockSpec((tm, a.shape[1]), lambda i, j: (i, 0))

    def w_spec(a):
        return pl.BlockSpec((None, a.shape[1], tn), lambda i, j: (layer, 0, j))

    def g_spec(k):
        return pl.BlockSpec((tm, tn), lambda i, j: (i, gbase + k * step + j))

    return pl.pallas_call(
        _merge_kernel,
        grid=(n // tm, d_model // tn),
        in_specs=[y_spec(ya), y_spec(yb), y_spec(yc), w_spec(wa), w_spec(wb), w_spec(wc),
                  g_spec(0), g_spec(1), g_spec(2)],
        out_specs=pl.BlockSpec((tm, tn), lambda i, j: (i, j)),
        out_shape=jax.ShapeDtypeStruct((n, d_model), BF16),
        compiler_params=_cparams("parallel", "parallel"),
        name="merge_branches",
    )(ya, yb, yc, wa, wb, wc, proj, proj, proj)


def _router_kernel(x_ref, w_ref, b_ref, o_ref):
    logits = jnp.dot(x_ref[...], w_ref[...], preferred_element_type=F32)
    scores = jax.nn.sigmoid(logits)
    lane = lax.broadcasted_iota(jnp.int32, scores.shape, 1)
    real = lane < N_EXPERTS
    sel = _top_mask(jnp.where(real, scores + b_ref[...], -jnp.inf), lane, TOP_K)
    w = jnp.where(real, sel * scores, 0.0)
    o_ref[...] = w / jnp.sum(w, axis=-1, keepdims=True) * ROUTE_SCALE


def moe_router(x, router_w, router_b, tm=512):
    n, d = x.shape
    tm = min(tm, n)
    w = jnp.zeros((d, LANES), BF16).at[:, :N_EXPERTS].set(router_w.astype(BF16))
    b = jnp.zeros((1, LANES), F32).at[0, :N_EXPERTS].set(router_b.astype(F32))
    return pl.pallas_call(
        _router_kernel,
        grid=(n // tm,),
        in_specs=[pl.BlockSpec((tm, d), lambda i: (i, 0)), pl.BlockSpec((d, LANES), lambda i: (0, 0)),
                  pl.BlockSpec((1, LANES), lambda i: (0, 0))],
        out_specs=pl.BlockSpec((tm, LANES), lambda i: (i, 0)),
        out_shape=jax.ShapeDtypeStruct((n, LANES), F32),
        compiler_params=_cparams("parallel"),
        name="moe_router",
    )(x, w, b)


def _moe_up_kernel(ec, x_ref, w1_ref, w3_ref, gate_ref, ex_ref, o_ref):
    x = x_ref[...]
    w1 = jnp.concatenate([w1_ref[e] for e in range(ec)], axis=1)
    w3 = jnp.concatenate([w3_ref[e] for e in range(ec)], axis=1)
    h1 = jnp.dot(x, w1, preferred_element_type=F32)
    h3 = jnp.dot(x, w3, preferred_element_type=F32)
    g = gate_ref[...]
    g_hi = g.astype(BF16)
    g_lo = (g - g_hi.astype(F32)).astype(BF16)
    gexp = jnp.dot(jnp.concatenate([g_hi, g_lo], axis=1), ex_ref[...], preferred_element_type=F32)
    o_ref[...] = (h1 * jax.nn.sigmoid(h1) * h3 * gexp).astype(BF16)


def moe_hidden(x, gate, w1, w3, layer, tm=1024, ec=4):
    n, d = x.shape
    tm = min(tm, n)
    th = ec * D_EXPERT
    expand = (np.arange(N_EXPERTS * D_EXPERT)[None, :] // D_EXPERT == np.arange(LANES)[:, None])
    expand = jnp.asarray(np.concatenate([expand, expand], axis=0).astype(np.float32), dtype=BF16)
    w_spec = pl.BlockSpec((None, ec, d, D_EXPERT), lambda i, j: (layer, j, 0, 0))
    return pl.pallas_call(
        functools.partial(_moe_up_kernel, ec),
        grid=(n // tm, N_EXPERTS // ec),
        in_specs=[pl.BlockSpec((tm, d), lambda i, j: (i, 0)), w_spec, w_spec,
                  pl.BlockSpec((tm, LANES), lambda i, j: (i, 0)),
                  pl.BlockSpec((2 * LANES, th), lambda i, j: (0, j))],
        out_specs=pl.BlockSpec((tm, th), lambda i, j: (i, j)),
        out_shape=jax.ShapeDtypeStruct((n, N_EXPERTS * D_EXPERT), BF16),
        compiler_params=_cparams("parallel", "parallel"),
        name="moe_hidden",
    )(x, w1, w3, gate, expand)


def _glu_kernel(x_ref, w1_ref, w3_ref, o_ref):
    x = x_ref[...]
    h1 = jnp.dot(x, w1_ref[...], preferred_element_type=F32)
    h3 = jnp.dot(x, w3_ref[...], preferred_element_type=F32)
    o_ref[...] = (h1 * jax.nn.sigmoid(h1) * h3).astype(BF16)


def shared_hidden(x, ws1, ws3, layer, tm=1024):
    n, d = x.shape
    tm = min(tm, n)
    w_spec = pl.BlockSpec((None, d, D_SHARED), lambda i: (layer, 0, 0))
    return pl.pallas_call(
        _glu_kernel,
        grid=(n // tm,),
        in_specs=[pl.BlockSpec((tm, d), lambda i: (i, 0)), w_spec, w_spec],
        out_specs=pl.BlockSpec((tm, D_SHARED), lambda i: (i, 0)),
        out_shape=jax.ShapeDtypeStruct((n, D_SHARED), BF16),
        compiler_params=_cparams("parallel"),
        name="shared_hidden",
    )(x, ws1, ws3)


def _moe_down_kernel(hr_ref, hs_ref, w2_ref, ws2_ref, o_ref):
    y = jnp.dot(hr_ref[...], w2_ref[...], preferred_element_type=F32)
    y = y + jnp.dot(hs_ref[...], ws2_ref[...], preferred_element_type=F32)
    o_ref[...] = y.astype(o_ref.dtype)


def moe_down(hid_r, hid_s, w2, ws2, layer, tm=1024, tn=512):
    n, kr = hid_r.shape
    ks = hid_s.shape[1]
    d = w2.shape[-1]
    tm = min(tm, n)
    return pl.pallas_call(
        _moe_down_kernel,
        grid=(n // tm, d // tn),
        in_specs=[pl.BlockSpec((tm, kr), lambda i, j: (i, 0)),
                  pl.BlockSpec((tm, ks), lambda i, j: (i, 0)),
                  pl.BlockSpec((None, kr, tn), lambda i, j: (layer, 0, j)),
                  pl.BlockSpec((None, ks, tn), lambda i, j: (layer, 0, j))],
        out_specs=pl.BlockSpec((tm, tn), lambda i, j: (i, j)),
        out_shape=jax.ShapeDtypeStruct((n, d), BF16),
        compiler_params=_cparams("parallel", "parallel"),
        name="moe_down",
    )(hid_r, hid_s, w2, ws2)


def _branch_gate_weights(w_t):
    depth, _, d_model = w_t.shape
    ag = w_t[:, ATTN_COLS:ATTN_COLS + 3 * ATTN_HEADS, :].reshape(depth, 3, KV_GROUPS, HEADS_PER_GROUP, d_model)
    ag = jnp.moveaxis(ag, 2, 1).reshape(depth, KV_GROUPS, 3 * HEADS_PER_GROUP, d_model)
    ag = jnp.pad(ag, ((0, 0), (0, 0), (0, LANES - 3 * HEADS_PER_GROUP), (0, 0)))
    return ag.reshape(depth, KV_GROUPS * LANES, d_model)


def _hgrn_lower_bounds(logits):
    lbs = jnp.cumsum(jax.nn.softmax(logits.astype(F32), axis=0), axis=0)
    return lbs - lbs[0:1]


def kernel(x, positions, ln_in_g, ln_in_b, w_in, cmp_pos_k, cmp_pos_v, cmp_w1_k, cmp_w2_k, cmp_w1_v, cmp_w2_v,
           hg_lb_logits, hg_norm_g, pool_w, pool_scale, w_up_attn, w_up_hg, w_up_pool, w_o, ln1_g, ln1_b,
           router_w, router_b, w1, w3, w2, ws1, ws3, ws2, ln2_g, ln2_b):
    batch, seq, d = x.shape
    n = batch * seq
    depth = w_in.shape[0]
    lbs = _hgrn_lower_bounds(hg_lb_logits)
    tabs = rope_tables(positions)
    w_in_t = jnp.swapaxes(w_in, 1, 2).astype(BF16)
    w_gate_t = _branch_gate_weights(w_in_t)
    w1b, w3b = w1.astype(BF16), w3.astype(BF16)
    w2b = w2.astype(BF16).reshape(depth, N_EXPERTS * D_EXPERT, d)
    ws1b, ws3b, ws2b = ws1.astype(BF16), ws3.astype(BF16), ws2.astype(BF16)
    wab, whb, wpb, wob = w_up_attn.astype(BF16), w_up_hg.astype(BF16), w_up_pool.astype(BF16), w_o.astype(BF16)
    h32, h16 = layer_norm_rows(x.reshape(n, d), ln_in_g, ln_in_b)
    for l in range(depth):
        proj_a = matmul_nt(h16, w_in_t, BF16, l, tm=512, tn=ATTN_COLS // 2, rows=ATTN_COLS)
        proj_g = matmul_nt(h16, w_gate_t, BF16, l)
        proj_b = matmul_nt(h16, w_in_t, BF16, l, row0=ATTN_COLS + 3 * ATTN_HEADS)
        cmp_params = (cmp_pos_k[l], cmp_pos_v[l],
                      cmp_w1_k[l].reshape(CMP_BLOCK, HEAD_DIM, HEAD_DIM).astype(BF16), cmp_w2_k[l].astype(BF16),
                      cmp_w1_v[l].reshape(CMP_BLOCK, HEAD_DIM, HEAD_DIM).astype(BF16), cmp_w2_v[l].astype(BF16))
        ya = nsa_attention(proj_a, proj_g, tabs, cmp_params, batch, seq)
        yb = hgrn2(proj_b, lbs[l], hg_norm_g[l], batch, seq)
        yc = multiscale_pool(proj_b, pool_w[l].astype(BF16), pool_scale[l], batch, seq)
        merged = merge_branches(ya, yb, yc, wab, whb, wpb, proj_b, d, l)
        mix = matmul(merged, wob, BF16, layer=l)
        h32, h16 = layer_norm_rows(mix, ln1_g[l], ln1_b[l], res=h32, alpha=DN_ALPHA)
        gate = moe_router(h16, router_w[l], router_b[l])
        hid_r = moe_hidden(h16, gate, w1b, w3b, l)
        hid_s = shared_hidden(h16, ws1b, ws3b, l)
        ffn = moe_down(hid_r, hid_s, w2b, ws2b, l)
        h32, h16 = layer_norm_rows(ffn, ln2_g[l], ln2_b[l], res=h32, alpha=DN_ALPHA)
    return h32.reshape(batch, seq, d)
```

```python
import functools

import numpy as np
import jax
import jax.numpy as jnp
from jax import lax
from jax.experimental import pallas as pl
from jax.experimental.pallas import tpu as pltpu

F32 = jnp.float32
BF16 = jnp.bfloat16

DEPTH = 2
HEAD_DIM = 128
ATTN_HEADS = 16
KV_GROUPS = 2
HEADS_PER_GROUP = ATTN_HEADS // KV_GROUPS
ATTN_WIDTH = ATTN_HEADS * HEAD_DIM
KV_WIDTH = KV_GROUPS * HEAD_DIM
ROPE_DIM = HEAD_DIM // 4
ROPE_THETA = 500000.0
CMP_BLOCK = 32
CMP_STRIDE = 16
SLC_BLOCK = 32
SLC_TOPN = 8
WINDOW = 512
HG_HEADS = 8
HG_DK = 128
HG_DV = 128
HG_KWIDTH = HG_HEADS * HG_DK
HG_WIDTH = HG_HEADS * HG_DV
POOL_WINDOWS = (2, 4, 8, 16)
POOL_GROUP = 256
POOL_WIDTH = POOL_GROUP * len(POOL_WINDOWS)
N_EXPERTS = 64
TOP_K = 8
D_EXPERT = 128
D_SHARED = 256
ROUTE_SCALE = 2.5
DN_ALPHA = (2.0 * DEPTH) ** 0.25
LN_EPS = 1e-5
RMS_EPS = 1e-6
NEG = -1e30

LANES = 128
SUBLANES = 8
BF16_SUBLANES = 16
VMEM_LIMIT = 56 * 1024 * 1024

COL_KV = ATTN_WIDTH
ATTN_COLS = ATTN_WIDTH + 6 * KV_WIDTH
COL_HG = 0
COL_POOL = COL_HG + 4 * HG_WIDTH
COL_MG = COL_POOL + POOL_WIDTH

TQ = 256
KC = 512
WSPAN = WINDOW + TQ
HC = 128
HG_LEVELS = (64, 32, 16, 8)
HG_STEP_HEADS = 8


def _cparams(*sem):
    return pltpu.CompilerParams(dimension_semantics=sem, vmem_limit_bytes=VMEM_LIMIT)


def _ln_kernel(alpha, has_res, *refs):
    if has_res:
        x_ref, r_ref, g_ref, b_ref, o_ref = refs
        x = alpha * r_ref[...].astype(F32) + x_ref[...].astype(F32)
    else:
        x_ref, g_ref, b_ref, o_ref = refs
        x = x_ref[...].astype(F32)
    mu = jnp.mean(x, axis=-1, keepdims=True)
    xc = x - mu
    var = jnp.mean(xc * xc, axis=-1, keepdims=True)
    y = xc * lax.rsqrt(var + LN_EPS) * g_ref[...] + b_ref[...]
    o_ref[...] = y.astype(o_ref.dtype)


def layer_norm_rows(x, g, b, res=None, alpha=1.0, out_dtype=BF16, tm=256):
    n, d = x.shape
    row = pl.BlockSpec((tm, d), lambda i: (i, 0))
    vec = pl.BlockSpec((1, d), lambda i: (0, 0))
    ins = [x] + ([res] if res is not None else []) + [g.reshape(1, d), b.reshape(1, d)]
    specs = [row] + ([row] if res is not None else []) + [vec, vec]
    return pl.pallas_call(
        functools.partial(_ln_kernel, alpha, res is not None),
        grid=(n // tm,),
        in_specs=specs,
        out_specs=row,
        out_shape=jax.ShapeDtypeStruct((n, d), out_dtype),
        compiler_params=_cparams("parallel"),
        name="layer_norm",
    )(*ins)


def _mm_kernel(x_ref, w_ref, o_ref):
    o_ref[...] = jnp.dot(x_ref[...], w_ref[...], preferred_element_type=F32).astype(o_ref.dtype)


def matmul(x, w, out_dtype, tm=1024, tn=1024, layer=None, cols=None):
    n, k = x.shape
    m = cols if cols is not None else w.shape[-1]
    tm, tn = min(tm, n), min(tn, m)
    assert m % tn == 0
    if layer is None:
        w_spec = pl.BlockSpec((k, tn), lambda i, j: (0, j))
    else:
        w_spec = pl.BlockSpec((None, k, tn), lambda i, j: (layer, 0, j))
    return pl.pallas_call(
        _mm_kernel,
        grid=(n // tm, m // tn),
        in_specs=[pl.BlockSpec((tm, k), lambda i, j: (i, 0)), w_spec],
        out_specs=pl.BlockSpec((tm, tn), lambda i, j: (i, j)),
        out_shape=jax.ShapeDtypeStruct((n, m), out_dtype),
        compiler_params=_cparams("parallel", "parallel"),
        name="matmul",
    )(x, w)


def _mm_nt_kernel(x_ref, w_ref, o_ref):
    o_ref[...] = lax.dot_general(x_ref[...], w_ref[0], (((1,), (1,)), ((), ())),
                                 preferred_element_type=F32).astype(o_ref.dtype)


def matmul_nt(x, wt, out_dtype, layer, tm=1024, tn=1024, row0=0, rows=None):
    n, k = x.shape
    m = rows if rows is not None else wt.shape[1] - row0
    tm, tn = min(tm, n), min(tn, m)
    assert m % tn == 0 and row0 % BF16_SUBLANES == 0
    return pl.pallas_call(
        _mm_nt_kernel,
        grid=(n // tm, m // tn),
        in_specs=[pl.BlockSpec((tm, k), lambda i, j: (i, 0)),
                  pl.BlockSpec((pl.Element(1), pl.Element(tn), pl.Element(k)),
                               lambda i, j: (layer, pl.multiple_of(row0 + j * tn, BF16_SUBLANES), 0))],
        out_specs=pl.BlockSpec((tm, tn), lambda i, j: (i, j)),
        out_shape=jax.ShapeDtypeStruct((n, m), out_dtype),
        compiler_params=_cparams("parallel", "parallel"),
        name="matmul_nt",
    )(x, wt)


def _rope_table_kernel(pos_ref, inv_ref, c_ref, sa_ref, sb_ref):
    ang = pos_ref[...].astype(F32) * inv_ref[...]
    lane = lax.broadcasted_iota(jnp.int32, ang.shape, 1)
    sn = jnp.sin(ang)
    c_ref[...] = jnp.cos(ang)
    sa_ref[...] = jnp.where(lane < ROPE_DIM // 2, -sn, 0.0)
    sb_ref[...] = jnp.where((lane >= ROPE_DIM // 2) & (lane < ROPE_DIM), sn, 0.0)


def rope_tables(positions, tm=1024):
    n = positions.size
    half = ROPE_DIM // 2
    inv = ROPE_THETA ** (-np.arange(half, dtype=np.float32) * 2.0 / ROPE_DIM)
    inv_full = np.zeros((1, LANES), np.float32)
    inv_full[0, :half] = inv
    inv_full[0, half:ROPE_DIM] = inv
    tm = min(tm, n)
    out = jax.ShapeDtypeStruct((n, LANES), F32)
    spec = pl.BlockSpec((tm, LANES), lambda i: (i, 0))
    return pl.pallas_call(
        _rope_table_kernel,
        grid=(n // tm,),
        in_specs=[pl.BlockSpec((tm, 1), lambda i: (i, 0)), pl.BlockSpec((1, LANES), lambda i: (0, 0))],
        out_specs=[spec, spec, spec],
        out_shape=[out, out, out],
        compiler_params=_cparams("parallel"),
        name="rope_tables",
    )(positions.reshape(n, 1), jnp.asarray(inv_full))


def _rope(x, c, sa, sb):
    return x * c + pltpu.roll(x, LANES - ROPE_DIM // 2, 1) * sa + pltpu.roll(x, ROPE_DIM // 2, 1) * sb


def _gelu_tanh(x):
    return 0.5 * x * (1.0 + jnp.tanh(0.7978845608028654 * (x + 0.044715 * x * x * x)))


def _top_mask(val, lane, n_pick, axis=-1):
    sel = jnp.zeros(val.shape, F32)
    for _ in range(n_pick):
        m = jnp.max(val, axis=axis, keepdims=True)
        idx = jnp.min(jnp.where(val == m, lane, LANES), axis=axis, keepdims=True)
        pick = lane == idx
        sel = jnp.where(pick, 1.0, sel)
        val = jnp.where(pick, -jnp.inf, val)
    return sel


def _attn_kernel(q_ref, kc_ref, vc_ref, ks_ref, vs_ref, kw_ref, vw_ref, gate_ref,
                 cq_ref, saq_ref, sbq_ref, ck_ref, sak_ref, sbk_ref,
                 posk_ref, posv_ref, w1k_ref, w2k_ref, w1v_ref, w2v_ref, ov_ref, nege_ref,
                 o_ref, kcmp_s, vcmp_s, kx_s, kwr_s, vsx_s, vwx_s, tmp_s, sca_s, scb_s, m_s, acc_s, sw_s, wbias_s):
    seq = ks_ref.shape[0]
    nh = seq // CMP_STRIDE
    n_cmp = (seq - CMP_BLOCK) // CMP_STRIDE + 1
    hpg = HEADS_PER_GROUP
    scale = HEAD_DIM ** -0.5
    scale2 = scale * 1.4426950408889634
    qt = pl.program_id(2)
    nt = (((1,), (1,)), ((), ()))

    @pl.when(qt == 0)
    def _per_sequence():
        def compress(t_ref, pos_ref, w1_ref, w2_ref, out_s):
            tmp_s[...] = t_ref[...].astype(F32)
            first = jnp.zeros((nh, HEAD_DIM), F32)
            second = jnp.zeros((nh, HEAD_DIM), F32)
            for j in range(CMP_BLOCK):
                x = tmp_s[pl.ds(j % CMP_STRIDE, nh, stride=CMP_STRIDE), :] + pos_ref[j:j + 1, :]
                p = jnp.dot(x.astype(BF16), w1_ref[j], preferred_element_type=F32)
                if j < CMP_STRIDE:
                    first = first + p
                else:
                    second = second + p
            pre = first + pltpu.roll(second, nh - 1, 0)
            hid = _gelu_tanh(pre).astype(BF16)
            out_s[...] = jnp.dot(hid, w2_ref[...], preferred_element_type=F32).astype(BF16)

        compress(kc_ref, posk_ref, w1k_ref, w2k_ref, kcmp_s)
        compress(vc_ref, posv_ref, w1v_ref, w2v_ref, vcmp_s)
        ck, sak, sbk = ck_ref[...], sak_ref[...], sbk_ref[...]
        kx_s[:, :HEAD_DIM] = _rope(ks_ref[...].astype(F32), ck, sak, sbk).astype(BF16)
        kx_s[:, HEAD_DIM:] = nege_ref[...]
        ones = jnp.ones((seq, HEAD_DIM), BF16)
        vsx_s[:, :HEAD_DIM] = vs_ref[...]
        vsx_s[:, HEAD_DIM:] = ones
        vwx_s[:, :HEAD_DIM] = vw_ref[...]
        vwx_s[:, HEAD_DIM:] = ones
        kwr_s[...] = _rope(kw_ref[...].astype(F32), ck, sak, sbk).astype(BF16)

    t0 = qt * TQ
    t = t0 + lax.broadcasted_iota(jnp.int32, (TQ, 1), 0)
    lane = lax.broadcasted_iota(jnp.int32, (TQ, LANES), 1)
    q = q_ref[...]
    qf = q.astype(F32)
    cq, saq, sbq = cq_ref[...], saq_ref[...], sbq_ref[...]
    heads = [slice(h * HEAD_DIM, (h + 1) * HEAD_DIM) for h in range(hpg)]
    q_raw = jnp.concatenate([q[:, s] for s in heads], axis=0)
    q_rot = jnp.concatenate([_rope(qf[:, s], cq, saq, sbq).astype(BF16) for s in heads], axis=0)

    s = lax.dot_general(q_raw, kcmp_s[...], nt, preferred_element_type=F32) * scale
    s3 = s.reshape(hpg, TQ, nh)
    vis = (lane * CMP_STRIDE + (CMP_BLOCK - 1) <= t) & (lane < n_cmp)
    s3 = jnp.where(vis[None], s3, NEG)
    e3 = jnp.exp(s3 - jnp.max(s3, axis=-1, keepdims=True))
    p3 = e3 / jnp.sum(e3, axis=-1, keepdims=True) * vis[None].astype(F32)
    pb = p3.astype(BF16)
    o_c = jnp.dot(pb.reshape(hpg * TQ, nh), vcmp_s[...], preferred_element_type=F32).reshape(hpg, TQ, HEAD_DIM)

    psum = jnp.sum(pb.astype(F32), axis=0)
    imp = lax.dot_general(ov_ref[...], psum, nt, preferred_element_type=F32, precision=lax.Precision.HIGHEST)
    n_slc = imp.shape[0]
    blk = lax.broadcasted_iota(jnp.int32, (n_slc, TQ), 0)
    blk_t = (t0 + lax.broadcasted_iota(jnp.int32, (n_slc, TQ), 1)) // SLC_BLOCK
    causal = blk <= blk_t
    forced = (blk == 0) | (blk == blk_t) | (blk == blk_t - 1)
    val = jnp.where(forced, jnp.inf, jnp.where(causal, imp, -jnp.inf))
    not_sel = jnp.where(causal, 1.0 - _top_mask(val, blk, SLC_TOPN, axis=0), 1.0)
    not_sel = jnp.concatenate([not_sel.T, jnp.zeros((TQ, HEAD_DIM - n_slc), F32)], axis=1).astype(BF16)
    q_ext = jnp.concatenate([q_rot, jnp.concatenate([not_sel] * hpg, axis=0)], axis=1)

    w0 = pl.multiple_of(jnp.maximum(t0 - WINDOW, 0), TQ)
    kk = kwr_s[pl.ds(w0, WSPAN), :]
    vv = vwx_s[pl.ds(w0, WSPAN), :]
    sw_s[...] = lax.dot_general(q_rot, kk, nt, preferred_element_type=F32)
    kpos = w0 + lax.broadcasted_iota(jnp.int32, (TQ, WSPAN), 1)
    wbias_s[...] = jnp.where((kpos <= t) & (t - kpos < WINDOW), 0.0, NEG)
    m_w = jnp.max(sw_s[...].reshape(hpg, TQ, WSPAN) + wbias_s[...][None], axis=-1, keepdims=True)
    pw = jnp.exp2((sw_s[...].reshape(hpg, TQ, WSPAN) + wbias_s[...][None] - m_w) * scale2)
    o_w = jnp.dot(pw.astype(BF16).reshape(hpg * TQ, WSPAN), vv, preferred_element_type=F32)
    o_w = (o_w[:, :HEAD_DIM] / o_w[:, HEAD_DIM:]).reshape(hpg, TQ, HEAD_DIM)

    g = jax.nn.sigmoid(gate_ref[...].astype(F32))
    o_cw = [g[:, h:h + 1] * o_c[h] + g[:, 2 * hpg + h:2 * hpg + h + 1] * o_w[h] for h in range(hpg)]

    n_chunks = (t0 + TQ + KC - 1) // KC
    last = n_chunks - 1
    tri_bias = jnp.where(lax.broadcasted_iota(jnp.int32, (TQ, TQ), 1) <= lax.broadcasted_iota(jnp.int32, (TQ, TQ), 0),
                         0.0, NEG)

    def scores(c):
        kk = kx_s[pl.ds(pl.multiple_of(c * KC, KC), KC), :]
        return lax.dot_general(q_ext, kk, nt, preferred_element_type=F32)

    def softmax_pv(sc_ref, c, first):
        vv = vsx_s[pl.ds(pl.multiple_of(c * KC, KC), KC), :]
        m_new = jnp.max(sc_ref[...].reshape(hpg, TQ, KC), axis=-1, keepdims=True)
        if not first:
            m_old = m_s[...].reshape(hpg, TQ, 1)
            m_new = jnp.maximum(m_old, m_new)
        p = jnp.exp2((sc_ref[...].reshape(hpg, TQ, KC) - m_new) * scale2)
        pv = jnp.dot(p.astype(BF16).reshape(hpg * TQ, KC), vv, preferred_element_type=F32)
        if first:
            acc_s[...] = pv
        else:
            a = jnp.exp2((m_old - m_new) * scale2).reshape(hpg * TQ, 1)
            acc_s[...] = a * acc_s[...] + pv
        m_s[...] = m_new.reshape(hpg * TQ, 1)

    sca_s[...] = scores(last)
    own = pl.ds(pl.multiple_of(t0 - last * KC, TQ), TQ)
    for h in range(hpg):
        sca_s[h * TQ:(h + 1) * TQ, own] += tri_bias
    scb_s[...] = scores(0)
    softmax_pv(sca_s, last, True)

    def chunk_pair(i, carry):
        c0 = 2 * i
        sca_s[...] = scores(jnp.minimum(c0 + 1, last))
        softmax_pv(scb_s, c0, False)

        @pl.when(c0 + 1 < last)
        def _():
            scb_s[...] = scores(jnp.minimum(c0 + 2, last))
            softmax_pv(sca_s, c0 + 1, False)

        return carry

    lax.fori_loop(0, (last + 1) // 2, chunk_pair, 0)
    acc = acc_s[...]
    o_s = (acc[:, :HEAD_DIM] / acc[:, HEAD_DIM:]).reshape(hpg, TQ, HEAD_DIM)
    for h in range(hpg):
        o_ref[:, heads[h]] = (o_cw[h] + g[:, hpg + h:hpg + h + 1] * o_s[h]).astype(BF16)


def _overlap_matrix(n_half, n_cmp, n_slc):
    c = np.arange(n_half)[None, :] * CMP_STRIDE
    s = np.arange(n_slc)[:, None] * SLC_BLOCK
    ov = np.clip(np.minimum(c + CMP_BLOCK, s + SLC_BLOCK) - np.maximum(c, s), 0, None) / CMP_STRIDE
    ov[:, n_cmp:] = 0.0
    return ov.astype(np.float32)


def nsa_attention(proj, proj_g, tabs, cmp_params, batch, seq):
    n = batch * seq
    nq = seq // TQ
    nh = seq // CMP_STRIDE
    n_cmp = (seq - CMP_BLOCK) // CMP_STRIDE + 1
    n_slc = seq // SLC_BLOCK
    assert nh == LANES and n_slc <= LANES and seq % KC == 0 and seq >= WSPAN
    c_tab, sa_tab, sb_tab = tabs
    posk, posv, w1k, w2k, w1v, w2v = cmp_params
    ov = jnp.asarray(_overlap_matrix(nh, n_cmp, n_slc))
    neg_onehot = np.where(np.arange(seq)[:, None] // SLC_BLOCK == np.arange(HEAD_DIM)[None, :], NEG, 0.0)
    expand = jnp.asarray(neg_onehot.astype(np.float32), dtype=BF16)

    gw = HEADS_PER_GROUP * HEAD_DIM
    qspec = pl.BlockSpec((TQ, gw), lambda b, g, i: (b * nq + i, g))

    def kvspec(slab):
        return pl.BlockSpec((seq, HEAD_DIM), lambda b, g, i: (b, COL_KV // HEAD_DIM + slab * KV_GROUPS + g))

    gspec = pl.BlockSpec((TQ, LANES), lambda b, g, i: (b * nq + i, g))
    tq_spec = pl.BlockSpec((TQ, LANES), lambda b, g, i: (b * nq + i, 0))
    tk_spec = pl.BlockSpec((seq, LANES), lambda b, g, i: (b, 0))

    def full(a):
        return pl.BlockSpec(a.shape, lambda b, g, i: (0,) * a.ndim)

    consts = [posk, posv, w1k, w2k, w1v, w2v, ov, expand]
    return pl.pallas_call(
        _attn_kernel,
        grid=(batch, KV_GROUPS, nq),
        in_specs=[qspec] + [kvspec(s) for s in range(6)] + [gspec] + [tq_spec] * 3 + [tk_spec] * 3
                 + [full(a) for a in consts],
        out_specs=pl.BlockSpec((TQ, gw), lambda b, g, i: (b * nq + i, g)),
        out_shape=jax.ShapeDtypeStruct((n, ATTN_WIDTH), BF16),
        scratch_shapes=[pltpu.VMEM((nh, HEAD_DIM), BF16), pltpu.VMEM((nh, HEAD_DIM), BF16),
                        pltpu.VMEM((seq, 2 * HEAD_DIM), BF16), pltpu.VMEM((seq, HEAD_DIM), BF16),
                        pltpu.VMEM((seq, 2 * HEAD_DIM), BF16), pltpu.VMEM((seq, 2 * HEAD_DIM), BF16),
                        pltpu.VMEM((seq, HEAD_DIM), F32),
                        pltpu.VMEM((HEADS_PER_GROUP * TQ, KC), F32), pltpu.VMEM((HEADS_PER_GROUP * TQ, KC), F32),
                        pltpu.VMEM((HEADS_PER_GROUP * TQ, 1), F32), pltpu.VMEM((HEADS_PER_GROUP * TQ, 2 * HEAD_DIM), F32),
                        pltpu.VMEM((HEADS_PER_GROUP * TQ, WSPAN), F32), pltpu.VMEM((TQ, WSPAN), F32)],
        compiler_params=_cparams("parallel", "parallel", "arbitrary"),
        name="nsa_attention",
    )(proj, proj, proj, proj, proj, proj, proj, proj_g, c_tab, sa_tab, sb_tab, c_tab, sa_tab, sb_tab, *consts)


def _hgrn_level_tables():
    t = np.arange(HC)[:, None]
    s = np.arange(HC)[None, :]
    lv = np.full((HC, HC), -1, np.int32)
    for i, m in enumerate(HG_LEVELS):
        ok = ((t // m) % 2 == 1) & (s // m == t // m - 1)
        lv[ok] = i
    tril = (s <= t).astype(np.float32)
    return lv, tril


def _hgrn_head(q_b, f_b, i_b, g_b, lb, ng, lv, tril, st_ref, b_ref):
    nt = (((1,), (1,)), ((), ()))
    tn = (((0,), (0,)), ((), ()))
    f = lb + (1.0 - lb) * jax.nn.sigmoid(f_b.astype(F32))
    logf = jnp.log(f)
    kk = 1.0 - f
    q = q_b.astype(F32)
    v = i_b.astype(F32)
    vb = i_b
    b = jnp.dot(tril, logf, preferred_element_type=F32, precision=lax.Precision.HIGHEST)
    b_ref[...] = b
    row = lax.broadcasted_iota(jnp.int32, (HC, HG_DK), 0)

    st = st_ref[...]
    o = lax.dot_general((q * jnp.exp(b)).astype(BF16), st.astype(BF16), nt, preferred_element_type=F32)

    a = jnp.zeros((HC, HC), F32)
    for i, m in enumerate(HG_LEVELS):
        ref_rows = [jnp.broadcast_to(b_ref[pl.ds((2 * j + 1) * m - 1, 1), :], (2 * m, HG_DK))
                    for j in range(HC // (2 * m))]
        d = b - jnp.concatenate(ref_rows, axis=0)
        odd = (row // m) % 2 == 1
        x = (jnp.where(odd, q, kk) * jnp.exp(-jnp.abs(d))).astype(BF16)
        am = lax.dot_general(x, x, nt, preferred_element_type=F32)
        a = jnp.where(lv == i, am, a)

    nb = HC // SUBLANES
    b3 = b.reshape(nb, SUBLANES, HG_DK)
    o = o + jnp.dot(a.astype(BF16), vb, preferred_element_type=F32)

    q3 = q.reshape(nb, SUBLANES, HG_DK)
    k3 = kk.reshape(nb, SUBLANES, HG_DK)
    v3 = v.reshape(nb, SUBLANES, HG_DV)
    r3 = lax.broadcasted_iota(jnp.int32, (nb, SUBLANES, HG_DK), 1)
    od = jnp.zeros((nb, SUBLANES, HG_DV), F32)
    for j in range(SUBLANES):
        bj = jnp.broadcast_to(b3[:, j:j + 1, :], b3.shape)
        kj = jnp.broadcast_to(k3[:, j:j + 1, :], b3.shape)
        vj = jnp.broadcast_to(v3[:, j:j + 1, :], b3.shape)
        dec = jnp.where(r3 >= j, jnp.exp(jnp.minimum(b3 - bj, 0.0)), 0.0)
        od = od + jnp.sum(q3 * kj * dec, axis=-1, keepdims=True) * vj
    o = o + od.reshape(HC, HG_DV)

    b_last = b_ref[pl.ds(HC - 1, 1), :]
    kd = (kk * jnp.exp(b_last - b)).astype(BF16)
    st_ref[...] = st * jnp.exp(b_last) + lax.dot_general(vb, kd, tn, preferred_element_type=F32)

    o = o * lax.rsqrt(jnp.mean(o * o, axis=-1, keepdims=True) + RMS_EPS) * ng
    gg = g_b.astype(F32)
    return (o * (gg * jax.nn.sigmoid(gg))).astype(BF16)


def _hgrn_kernel(q_ref, f_ref, i_ref, g_ref, lb_ref, ng_ref, lv_ref, tril_ref, o_ref, st_s, b_s):
    @pl.when(pl.program_id(2) == 0)
    def _():
        st_s[...] = jnp.zeros_like(st_s)

    lv, tril = lv_ref[...], tril_ref[...]
    lb, ng = lb_ref[0], ng_ref[0]
    for h in range(HG_STEP_HEADS):
        s = slice(h * HG_DK, (h + 1) * HG_DK)
        o_ref[:, s] = _hgrn_head(q_ref[:, s], f_ref[:, s], i_ref[:, s], g_ref[:, s], lb[:, s], ng[:, s],
                                 lv, tril, st_s.at[h], b_s.at[h])


def hgrn2(proj, lb, norm_g, batch, seq):
    n = batch * seq
    nc = seq // HC
    hs = HG_STEP_HEADS
    wide = hs * HG_DK
    lv, tril = _hgrn_level_tables()
    base = COL_HG // wide

    def slab(k):
        return pl.BlockSpec((HC, wide), lambda b, h, c: (b * nc + c, base + k * (HG_HEADS // hs) + h))

    vec = pl.BlockSpec((1, 1, wide), lambda b, h, c: (h, 0, 0))
    const = pl.BlockSpec((HC, HC), lambda b, h, c: (0, 0))
    return pl.pallas_call(
        _hgrn_kernel,
        grid=(batch, HG_HEADS // hs, nc),
        in_specs=[slab(0), slab(1), slab(2), slab(3), vec, vec, const, const],
        out_specs=pl.BlockSpec((HC, wide), lambda b, h, c: (b * nc + c, h)),
        out_shape=jax.ShapeDtypeStruct((n, HG_WIDTH), BF16),
        scratch_shapes=[pltpu.VMEM((hs, HG_DV, HG_DK), F32), pltpu.VMEM((hs, HC, HG_DK), F32)],
        compiler_params=_cparams("parallel", "parallel", "arbitrary"),
        name="hgrn2",
    )(proj, proj, proj, proj, lb.reshape(HG_HEADS // hs, 1, wide), norm_g.reshape(HG_HEADS // hs, 1, wide),
      jnp.asarray(lv), jnp.asarray(tril))


def _pool_kernel(p_ref, w_ref, sc_ref, o_ref):
    gi = pl.program_id(1)
    x = p_ref[...].astype(F32)
    t = lax.broadcasted_iota(jnp.int32, x.shape, 0)
    acc = x
    for k in range(len(POOL_WINDOWS)):
        sh = 1 << k
        nxt = acc + jnp.where(t >= sh, pltpu.roll(acc, sh, 0), 0.0)
        acc = jnp.where(k <= gi, nxt, acc)
    width = lax.shift_left(jnp.int32(2), gi)
    cnt = jnp.minimum(t + 1, width).astype(F32)
    mixed = acc / cnt - x
    y = jnp.dot(mixed.astype(BF16), w_ref[0], preferred_element_type=F32) * sc_ref[0]
    o_ref[...] = y.astype(BF16)


def multiscale_pool(proj, w_pool, scale, batch, seq):
    n = batch * seq
    ng = len(POOL_WINDOWS)
    base = COL_POOL // POOL_GROUP
    return pl.pallas_call(
        _pool_kernel,
        grid=(batch, ng),
        in_specs=[pl.BlockSpec((seq, POOL_GROUP), lambda b, g: (b, base + g)),
                  pl.BlockSpec((1, POOL_GROUP, POOL_GROUP), lambda b, g: (g, 0, 0)),
                  pl.BlockSpec((1, 1, POOL_GROUP), lambda b, g: (g, 0, 0))],
        out_specs=pl.BlockSpec((seq, POOL_GROUP), lambda b, g: (b, g)),
        out_shape=jax.ShapeDtypeStruct((n, POOL_WIDTH), BF16),
        compiler_params=_cparams("parallel", "parallel"),
        name="multiscale_pool",
    )(proj, w_pool, scale.reshape(ng, 1, POOL_GROUP))


def _merge_kernel(ya_ref, yb_ref, yc_ref, wa_ref, wb_ref, wc_ref, ga_ref, gb_ref, gc_ref, o_ref):
    def branch(y_ref, w_ref, g_ref):
        up = jnp.dot(y_ref[...], w_ref[...], preferred_element_type=F32)
        return (0.5 + 0.5 * jnp.tanh(0.5 * g_ref[...].astype(F32))) * up

    o_ref[...] = (branch(ya_ref, wa_ref, ga_ref) + branch(yb_ref, wb_ref, gb_ref)
                  + branch(yc_ref, wc_ref, gc_ref)).astype(o_ref.dtype)


def merge_branches(ya, yb, yc, wa, wb, wc, proj, d_model, layer, tm=1024, tn=512):
    n = ya.shape[0]
    tm = min(tm, n)
    gbase = COL_MG // tn
    step = d_model // tn

    def y_spec(a):
        return pl.BlockSpec((tm, a.shape[1]), lambda i, j: (i, 0))

    def w_spec(a):
        return pl.BlockSpec((None, a.shape[1], tn), lambda i, j: (layer, 0, j))

    def g_spec(k):
        return pl.BlockSpec((tm, tn), lambda i, j: (i, gbase + k * step + j))

    return pl.pallas_call(
        _merge_kernel,
        grid=(n // tm, d_model // tn),
        in_specs=[y_spec(ya), y_spec(yb), y_spec(yc), w_spec(wa), w_spec(wb), w_spec(wc),
                  g_spec(0), g_spec(1), g_spec(2)],
        out_specs=pl.BlockSpec((tm, tn), lambda i, j: (i, j)),
        out_shape=jax.ShapeDtypeStruct((n, d_model), BF16),
        compiler_params=_cparams("parallel", "parallel"),
        name="merge_branches",
    )(ya, yb, yc, wa, wb, wc, proj, proj, proj)


def _router_kernel(x_ref, w_ref, b_ref, o_ref):
    logits = lax.dot_general(w_ref[...], x_ref[...], (((1,), (1,)), ((), ())),
                             preferred_element_type=F32)
    scores = jax.nn.sigmoid(logits)
    row = lax.broadcasted_iota(jnp.int32, scores.shape, 0)
    real = row < N_EXPERTS
    sel = _top_mask(jnp.where(real, scores + b_ref[...], -jnp.inf), row, TOP_K, axis=0)
    w = jnp.where(real, sel * scores, 0.0)
    w = w / jnp.sum(w, axis=0, keepdims=True) * ROUTE_SCALE
    o_ref[...] = w.T


def moe_router(x, router_w, router_b, tm=512):
    n, d = x.shape
    tm = min(tm, n)
    w = jnp.zeros((LANES, d), BF16).at[:N_EXPERTS].set(router_w.T.astype(BF16))
    b = jnp.zeros((LANES, 1), F32).at[:N_EXPERTS, 0].set(router_b.astype(F32))
    return pl.pallas_call(
        _router_kernel,
        grid=(n // tm,),
        in_specs=[pl.BlockSpec((tm, d), lambda i: (i, 0)), pl.BlockSpec((LANES, d), lambda i: (0, 0)),
                  pl.BlockSpec((LANES, 1), lambda i: (0, 0))],
        out_specs=pl.BlockSpec((tm, LANES), lambda i: (i, 0)),
        out_shape=jax.ShapeDtypeStruct((n, LANES), F32),
        compiler_params=_cparams("parallel"),
        name="moe_router",
    )(x, w, b)


def _moe_up_kernel(ec, x_ref, w1_ref, w3_ref, gate_ref, ex_ref, o_ref):
    x = x_ref[...]
    w1 = jnp.concatenate([w1_ref[e] for e in range(ec)], axis=1)
    w3 = jnp.concatenate([w3_ref[e] for e in range(ec)], axis=1)
    h1 = jnp.dot(x, w1, preferred_element_type=F32)
    h3 = jnp.dot(x, w3, preferred_element_type=F32)
    g = gate_ref[...]
    g_hi = g.astype(BF16)
    g_lo = (g - g_hi.astype(F32)).astype(BF16)
    gexp = jnp.dot(jnp.concatenate([g_hi, g_lo], axis=1), ex_ref[...], preferred_element_type=F32)
    o_ref[...] = (h1 * jax.nn.sigmoid(h1) * h3 * gexp).astype(BF16)


def moe_hidden(x, gate, w1, w3, layer, tm=1024, ec=4):
    n, d = x.shape
    tm = min(tm, n)
    th = ec * D_EXPERT
    expand = (np.arange(N_EXPERTS * D_EXPERT)[None, :] // D_EXPERT == np.arange(LANES)[:, None])
    expand = jnp.asarray(np.concatenate([expand, expand], axis=0).astype(np.float32), dtype=BF16)
    w_spec = pl.BlockSpec((None, ec, d, D_EXPERT), lambda i, j: (layer, j, 0, 0))
    return pl.pallas_call(
        functools.partial(_moe_up_kernel, ec),
        grid=(n // tm, N_EXPERTS // ec),
        in_specs=[pl.BlockSpec((tm, d), lambda i, j: (i, 0)), w_spec, w_spec,
                  pl.BlockSpec((tm, LANES), lambda i, j: (i, 0)),
                  pl.BlockSpec((2 * LANES, th), lambda i, j: (0, j))],
        out_specs=pl.BlockSpec((tm, th), lambda i, j: (i, j)),
        out_shape=jax.ShapeDtypeStruct((n, N_EXPERTS * D_EXPERT), BF16),
        compiler_params=_cparams("parallel", "parallel"),
        name="moe_hidden",
    )(x, w1, w3, gate, expand)


def _glu_kernel(x_ref, w1_ref, w3_ref, o_ref):
    x = x_ref[...]
    h1 = jnp.dot(x, w1_ref[...], preferred_element_type=F32)
    h3 = jnp.dot(x, w3_ref[...], preferred_element_type=F32)
    o_ref[...] = (h1 * jax.nn.sigmoid(h1) * h3).astype(BF16)


def shared_hidden(x, ws1, ws3, layer, tm=1024):
    n, d = x.shape
    tm = min(tm, n)
    w_spec = pl.BlockSpec((None, d, D_SHARED), lambda i: (layer, 0, 0))
    return pl.pallas_call(
        _glu_kernel,
        grid=(n // tm,),
        in_specs=[pl.BlockSpec((tm, d), lambda i: (i, 0)), w_spec, w_spec],
        out_specs=pl.BlockSpec((tm, D_SHARED), lambda i: (i, 0)),
        out_shape=jax.ShapeDtypeStruct((n, D_SHARED), BF16),
        compiler_params=_cparams("parallel"),
        name="shared_hidden",
    )(x, ws1, ws3)


def _moe_down_kernel(hr_ref, hs_ref, w2_ref, ws2_ref, o_ref):
    y = jnp.dot(hr_ref[...], w2_ref[...], preferred_element_type=F32)
    y = y + jnp.dot(hs_ref[...], ws2_ref[...], preferred_element_type=F32)
    o_ref[...] = y.astype(o_ref.dtype)


def moe_down(hid_r, hid_s, w2, ws2, layer, tm=1024, tn=512):
    n, kr = hid_r.shape
    ks = hid_s.shape[1]
    d = w2.shape[-1]
    tm = min(tm, n)
    return pl.pallas_call(
        _moe_down_kernel,
        grid=(n // tm, d // tn),
        in_specs=[pl.BlockSpec((tm, kr), lambda i, j: (i, 0)),
                  pl.BlockSpec((tm, ks), lambda i, j: (i, 0)),
                  pl.BlockSpec((None, kr, tn), lambda i, j: (layer, 0, j)),
                  pl.BlockSpec((None, ks, tn), lambda i, j: (layer, 0, j))],
        out_specs=pl.BlockSpec((tm, tn), lambda i, j: (i, j)),
        out_shape=jax.ShapeDtypeStruct((n, d), BF16),
        compiler_params=_cparams("parallel", "parallel"),
        name="moe_down",
    )(hid_r, hid_s, w2, ws2)


def _branch_gate_weights(w_t):
    depth, _, d_model = w_t.shape
    ag = w_t[:, ATTN_COLS:ATTN_COLS + 3 * ATTN_HEADS, :].reshape(depth, 3, KV_GROUPS, HEADS_PER_GROUP, d_model)
    ag = jnp.moveaxis(ag, 2, 1).reshape(depth, KV_GROUPS, 3 * HEADS_PER_GROUP, d_model)
    ag = jnp.pad(ag, ((0, 0), (0, 0), (0, LANES - 3 * HEADS_PER_GROUP), (0, 0)))
    return ag.reshape(depth, KV_GROUPS * LANES, d_model)


def _hgrn_lower_bounds(logits):
    lbs = jnp.cumsum(jax.nn.softmax(logits.astype(F32), axis=0), axis=0)
    return lbs - lbs[0:1]


def kernel(x, positions, ln_in_g, ln_in_b, w_in, cmp_pos_k, cmp_pos_v, cmp_w1_k, cmp_w2_k, cmp_w1_v, cmp_w2_v,
           hg_lb_logits, hg_norm_g, pool_w, pool_scale, w_up_attn, w_up_hg, w_up_pool, w_o, ln1_g, ln1_b,
           router_w, router_b, w1, w3, w2, ws1, ws3, ws2, ln2_g, ln2_b):
    batch, seq, d = x.shape
    n = batch * seq
    depth = w_in.shape[0]
    lbs = _hgrn_lower_bounds(hg_lb_logits)
    tabs = rope_tables(positions)
    w_in_t = jnp.swapaxes(w_in, 1, 2).astype(BF16)
    w_gate_t = _branch_gate_weights(w_in_t)
    w1b, w3b = w1.astype(BF16), w3.astype(BF16)
    w2b = w2.astype(BF16).reshape(depth, N_EXPERTS * D_EXPERT, d)
    ws1b, ws3b, ws2b = ws1.astype(BF16), ws3.astype(BF16), ws2.astype(BF16)
    wab, whb, wpb, wob = w_up_attn.astype(BF16), w_up_hg.astype(BF16), w_up_pool.astype(BF16), w_o.astype(BF16)
    h = layer_norm_rows(x.reshape(n, d), ln_in_g, ln_in_b)
    for l in range(depth):
        proj_a = matmul_nt(h, w_in_t, BF16, l, tm=512, tn=ATTN_COLS // 2, rows=ATTN_COLS)
        proj_g = matmul_nt(h, w_gate_t, BF16, l)
        proj_b = matmul_nt(h, w_in_t, BF16, l, row0=ATTN_COLS + 3 * ATTN_HEADS)
        cmp_params = (cmp_pos_k[l], cmp_pos_v[l],
                      cmp_w1_k[l].reshape(CMP_BLOCK, HEAD_DIM, HEAD_DIM).astype(BF16), cmp_w2_k[l].astype(BF16),
                      cmp_w1_v[l].reshape(CMP_BLOCK, HEAD_DIM, HEAD_DIM).astype(BF16), cmp_w2_v[l].astype(BF16))
        ya = nsa_attention(proj_a, proj_g, tabs, cmp_params, batch, seq)
        yb = hgrn2(proj_b, lbs[l], hg_norm_g[l], batch, seq)
        yc = multiscale_pool(proj_b, pool_w[l].astype(BF16), pool_scale[l], batch, seq)
        merged = merge_branches(ya, yb, yc, wab, whb, wpb, proj_b, d, l)
        mix = matmul(merged, wob, BF16, layer=l)
        h = layer_norm_rows(mix, ln1_g[l], ln1_b[l], res=h, alpha=DN_ALPHA)
        gate = moe_router(h, router_w[l], router_b[l])
        hid_r = moe_hidden(h, gate, w1b, w3b, l)
        hid_s = shared_hidden(h, ws1b, ws3b, l)
        ffn = moe_down(hid_r, hid_s, w2b, ws2b, l)
        h = layer_norm_rows(ffn, ln2_g[l], ln2_b[l], res=h, alpha=DN_ALPHA,
                            out_dtype=F32 if l == depth - 1 else BF16)
    return h.reshape(batch, seq, d)
```

```python
import functools

import numpy as np
import jax
import jax.numpy as jnp
from jax import lax
from jax.experimental import pallas as pl
from jax.experimental.pallas import tpu as pltpu

F32 = jnp.float32
BF16 = jnp.bfloat16

DEPTH = 2
HEAD_DIM = 128
ATTN_HEADS = 16
KV_GROUPS = 2
HEADS_PER_GROUP = ATTN_HEADS // KV_GROUPS
ATTN_WIDTH = ATTN_HEADS * HEAD_DIM
KV_WIDTH = KV_GROUPS * HEAD_DIM
ROPE_DIM = HEAD_DIM // 4
ROPE_THETA = 500000.0
CMP_BLOCK = 32
CMP_STRIDE = 16
SLC_BLOCK = 32
SLC_TOPN = 8
WINDOW = 512
HG_HEADS = 8
HG_DK = 128
HG_DV = 128
HG_KWIDTH = HG_HEADS * HG_DK
HG_WIDTH = HG_HEADS * HG_DV
POOL_WINDOWS = (2, 4, 8, 16)
POOL_GROUP = 256
POOL_WIDTH = POOL_GROUP * len(POOL_WINDOWS)
N_EXPERTS = 64
TOP_K = 8
D_EXPERT = 128
D_SHARED = 256
ROUTE_SCALE = 2.5
DN_ALPHA = (2.0 * DEPTH) ** 0.25
LN_EPS = 1e-5
RMS_EPS = 1e-6
NEG = -1e30

LANES = 128
SUBLANES = 8
BF16_SUBLANES = 16
VMEM_LIMIT = 56 * 1024 * 1024

COL_KV = ATTN_WIDTH
ATTN_COLS = ATTN_WIDTH + 6 * KV_WIDTH
COL_HG = 0
COL_POOL = COL_HG + 4 * HG_WIDTH
COL_MG = COL_POOL + POOL_WIDTH

TQ = 256
KC = 512
WSPAN = WINDOW + TQ
HC = 128
HG_LEVELS = (64, 32, 16, 8)
HG_STEP_HEADS = 8


def _cparams(*sem):
    return pltpu.CompilerParams(dimension_semantics=sem, vmem_limit_bytes=VMEM_LIMIT)


def _ln_kernel(alpha, has_res, *refs):
    if has_res:
        x_ref, r_ref, g_ref, b_ref, o_ref = refs
        x = alpha * r_ref[...].astype(F32) + x_ref[...].astype(F32)
    else:
        x_ref, g_ref, b_ref, o_ref = refs
        x = x_ref[...].astype(F32)
    mu = jnp.mean(x, axis=-1, keepdims=True)
    xc = x - mu
    var = jnp.mean(xc * xc, axis=-1, keepdims=True)
    y = xc * lax.rsqrt(var + LN_EPS) * g_ref[...] + b_ref[...]
    o_ref[...] = y.astype(o_ref.dtype)


def layer_norm_rows(x, g, b, res=None, alpha=1.0, out_dtype=BF16, tm=512):
    n, d = x.shape
    row = pl.BlockSpec((tm, d), lambda i: (i, 0))
    vec = pl.BlockSpec((1, d), lambda i: (0, 0))
    ins = [x] + ([res] if res is not None else []) + [g.reshape(1, d), b.reshape(1, d)]
    specs = [row] + ([row] if res is not None else []) + [vec, vec]
    return pl.pallas_call(
        functools.partial(_ln_kernel, alpha, res is not None),
        grid=(n // tm,),
        in_specs=specs,
        out_specs=row,
        out_shape=jax.ShapeDtypeStruct((n, d), out_dtype),
        compiler_params=_cparams("parallel"),
        name="layer_norm",
    )(*ins)


def _mm_kernel(x_ref, w_ref, o_ref):
    o_ref[...] = jnp.dot(x_ref[...], w_ref[...], preferred_element_type=F32).astype(o_ref.dtype)


def matmul(x, w, out_dtype, tm=1024, tn=1024, layer=None, cols=None):
    n, k = x.shape
    m = cols if cols is not None else w.shape[-1]
    tm, tn = min(tm, n), min(tn, m)
    assert m % tn == 0
    if layer is None:
        w_spec = pl.BlockSpec((k, tn), lambda i, j: (0, j))
    else:
        w_spec = pl.BlockSpec((None, k, tn), lambda i, j: (layer, 0, j))
    return pl.pallas_call(
        _mm_kernel,
        grid=(n // tm, m // tn),
        in_specs=[pl.BlockSpec((tm, k), lambda i, j: (i, 0)), w_spec],
        out_specs=pl.BlockSpec((tm, tn), lambda i, j: (i, j)),
        out_shape=jax.ShapeDtypeStruct((n, m), out_dtype),
        compiler_params=_cparams("parallel", "parallel"),
        name="matmul",
    )(x, w)


def _mm_nt_kernel(x_ref, w_ref, o_ref):
    o_ref[...] = lax.dot_general(x_ref[...], w_ref[0], (((1,), (1,)), ((), ())),
                                 preferred_element_type=F32).astype(o_ref.dtype)


def matmul_nt(x, wt, out_dtype, layer, tm=1024, tn=1024, row0=0, rows=None):
    n, k = x.shape
    m = rows if rows is not None else wt.shape[1] - row0
    tm, tn = min(tm, n), min(tn, m)
    assert m % tn == 0 and row0 % BF16_SUBLANES == 0
    return pl.pallas_call(
        _mm_nt_kernel,
        grid=(n // tm, m // tn),
        in_specs=[pl.BlockSpec((tm, k), lambda i, j: (i, 0)),
                  pl.BlockSpec((pl.Element(1), pl.Element(tn), pl.Element(k)),
                               lambda i, j: (layer, pl.multiple_of(row0 + j * tn, BF16_SUBLANES), 0))],
        out_specs=pl.BlockSpec((tm, tn), lambda i, j: (i, j)),
        out_shape=jax.ShapeDtypeStruct((n, m), out_dtype),
        compiler_params=_cparams("parallel", "parallel"),
        name="matmul_nt",
    )(x, wt)


def _rope_table_kernel(pos_ref, inv_ref, c_ref, sa_ref, sb_ref):
    ang = pos_ref[...].astype(F32) * inv_ref[...]
    lane = lax.broadcasted_iota(jnp.int32, ang.shape, 1)
    sn = jnp.sin(ang)
    c_ref[...] = jnp.cos(ang)
    sa_ref[...] = jnp.where(lane < ROPE_DIM // 2, -sn, 0.0)
    sb_ref[...] = jnp.where((lane >= ROPE_DIM // 2) & (lane < ROPE_DIM), sn, 0.0)


def rope_tables(positions, tm=1024):
    n = positions.size
    half = ROPE_DIM // 2
    inv = ROPE_THETA ** (-np.arange(half, dtype=np.float32) * 2.0 / ROPE_DIM)
    inv_full = np.zeros((1, LANES), np.float32)
    inv_full[0, :half] = inv
    inv_full[0, half:ROPE_DIM] = inv
    tm = min(tm, n)
    out = jax.ShapeDtypeStruct((n, LANES), F32)
    spec = pl.BlockSpec((tm, LANES), lambda i: (i, 0))
    return pl.pallas_call(
        _rope_table_kernel,
        grid=(n // tm,),
        in_specs=[pl.BlockSpec((tm, 1), lambda i: (i, 0)), pl.BlockSpec((1, LANES), lambda i: (0, 0))],
        out_specs=[spec, spec, spec],
        out_shape=[out, out, out],
        compiler_params=_cparams("parallel"),
        name="rope_tables",
    )(positions.reshape(n, 1), jnp.asarray(inv_full))


def _rope(x, c, sa, sb):
    return x * c + pltpu.roll(x, LANES - ROPE_DIM // 2, 1) * sa + pltpu.roll(x, ROPE_DIM // 2, 1) * sb


def _gelu_tanh(x):
    return 0.5 * x * (1.0 + jnp.tanh(0.7978845608028654 * (x + 0.044715 * x * x * x)))


def _top_mask(val, lane, n_pick, axis=-1):
    sel = jnp.zeros(val.shape, F32)
    for _ in range(n_pick):
        m = jnp.max(val, axis=axis, keepdims=True)
        idx = jnp.min(jnp.where(val == m, lane, LANES), axis=axis, keepdims=True)
        pick = lane == idx
        sel = jnp.where(pick, 1.0, sel)
        val = jnp.where(pick, -jnp.inf, val)
    return sel


def _attn_kernel(q_ref, kc_ref, vc_ref, ks_ref, vs_ref, kw_ref, vw_ref, gate_ref,
                 cq_ref, saq_ref, sbq_ref, ck_ref, sak_ref, sbk_ref,
                 posk_ref, posv_ref, w1k_ref, w2k_ref, w1v_ref, w2v_ref, ov_ref, nege_ref,
                 o_ref, kcmp_s, vcmp_s, kx_s, kwr_s, vsx_s, vwx_s, tmp_s, sca_s, scb_s, m_s, acc_s, sw_s, wbias_s):
    seq = ks_ref.shape[0]
    nh = seq // CMP_STRIDE
    n_cmp = (seq - CMP_BLOCK) // CMP_STRIDE + 1
    hpg = HEADS_PER_GROUP
    scale = HEAD_DIM ** -0.5
    scale2 = scale * 1.4426950408889634
    qt = pl.program_id(2)
    nt = (((1,), (1,)), ((), ()))

    @pl.when(qt == 0)
    def _per_sequence():
        def compress(t_ref, pos_ref, w1_ref, w2_ref, out_s):
            tmp_s[...] = t_ref[...].astype(F32)
            first = jnp.zeros((nh, HEAD_DIM), F32)
            second = jnp.zeros((nh, HEAD_DIM), F32)
            for j in range(CMP_BLOCK):
                x = tmp_s[pl.ds(j % CMP_STRIDE, nh, stride=CMP_STRIDE), :] + pos_ref[j:j + 1, :]
                p = jnp.dot(x.astype(BF16), w1_ref[j], preferred_element_type=F32)
                if j < CMP_STRIDE:
                    first = first + p
                else:
                    second = second + p
            pre = first + pltpu.roll(second, nh - 1, 0)
            hid = _gelu_tanh(pre).astype(BF16)
            out_s[...] = jnp.dot(hid, w2_ref[...], preferred_element_type=F32).astype(BF16)

        compress(kc_ref, posk_ref, w1k_ref, w2k_ref, kcmp_s)
        compress(vc_ref, posv_ref, w1v_ref, w2v_ref, vcmp_s)
        ck, sak, sbk = ck_ref[...], sak_ref[...], sbk_ref[...]
        kx_s[:, :HEAD_DIM] = _rope(ks_ref[...].astype(F32), ck, sak, sbk).astype(BF16)
        kx_s[:, HEAD_DIM:] = nege_ref[...]
        ones = jnp.ones((seq, HEAD_DIM), BF16)
        vsx_s[:, :HEAD_DIM] = vs_ref[...]
        vsx_s[:, HEAD_DIM:] = ones
        vwx_s[:, :HEAD_DIM] = vw_ref[...]
        vwx_s[:, HEAD_DIM:] = ones
        kwr_s[...] = _rope(kw_ref[...].astype(F32), ck, sak, sbk).astype(BF16)

    t0 = qt * TQ
    t = t0 + lax.broadcasted_iota(jnp.int32, (TQ, 1), 0)
    lane = lax.broadcasted_iota(jnp.int32, (TQ, LANES), 1)
    q = q_ref[...]
    qf = q.astype(F32)
    cq, saq, sbq = cq_ref[...], saq_ref[...], sbq_ref[...]
    heads = [slice(h * HEAD_DIM, (h + 1) * HEAD_DIM) for h in range(hpg)]
    q_raw = jnp.concatenate([q[:, s] for s in heads], axis=0)
    q_rot = jnp.concatenate([_rope(qf[:, s], cq, saq, sbq).astype(BF16) for s in heads], axis=0)

    s = lax.dot_general(q_raw, kcmp_s[...], nt, preferred_element_type=F32) * scale
    s3 = s.reshape(hpg, TQ, nh)
    vis = (lane * CMP_STRIDE + (CMP_BLOCK - 1) <= t) & (lane < n_cmp)
    s3 = jnp.where(vis[None], s3, NEG)
    e3 = jnp.exp(s3 - jnp.max(s3, axis=-1, keepdims=True))
    p3 = e3 / jnp.sum(e3, axis=-1, keepdims=True) * vis[None].astype(F32)
    pb = p3.astype(BF16)
    o_c = jnp.dot(pb.reshape(hpg * TQ, nh), vcmp_s[...], preferred_element_type=F32).reshape(hpg, TQ, HEAD_DIM)

    psum = jnp.sum(pb.astype(F32), axis=0)
    imp = lax.dot_general(ov_ref[...], psum, nt, preferred_element_type=F32, precision=lax.Precision.HIGHEST)
    n_slc = imp.shape[0]
    blk = lax.broadcasted_iota(jnp.int32, (n_slc, TQ), 0)
    blk_t = (t0 + lax.broadcasted_iota(jnp.int32, (n_slc, TQ), 1)) // SLC_BLOCK
    causal = blk <= blk_t
    forced = (blk == 0) | (blk == blk_t) | (blk == blk_t - 1)
    val = jnp.where(forced, jnp.inf, jnp.where(causal, imp, -jnp.inf))
    not_sel = jnp.where(causal, 1.0 - _top_mask(val, blk, SLC_TOPN, axis=0), 1.0)
    not_sel = jnp.concatenate([not_sel.T, jnp.zeros((TQ, HEAD_DIM - n_slc), F32)], axis=1).astype(BF16)
    q_ext = jnp.concatenate([q_rot, jnp.concatenate([not_sel] * hpg, axis=0)], axis=1)

    w0 = pl.multiple_of(jnp.maximum(t0 - WINDOW, 0), TQ)
    kk = kwr_s[pl.ds(w0, WSPAN), :]
    vv = vwx_s[pl.ds(w0, WSPAN), :]
    sw_s[...] = lax.dot_general(q_rot, kk, nt, preferred_element_type=F32)
    kpos = w0 + lax.broadcasted_iota(jnp.int32, (TQ, WSPAN), 1)
    wbias_s[...] = jnp.where((kpos <= t) & (t - kpos < WINDOW), 0.0, NEG)
    m_w = jnp.max(sw_s[...].reshape(hpg, TQ, WSPAN) + wbias_s[...][None], axis=-1, keepdims=True)
    pw = jnp.exp2((sw_s[...].reshape(hpg, TQ, WSPAN) + wbias_s[...][None] - m_w) * scale2)
    o_w = jnp.dot(pw.astype(BF16).reshape(hpg * TQ, WSPAN), vv, preferred_element_type=F32)
    o_w = (o_w[:, :HEAD_DIM] / o_w[:, HEAD_DIM:]).reshape(hpg, TQ, HEAD_DIM)

    g = jax.nn.sigmoid(gate_ref[...].astype(F32))
    o_cw = [g[:, h:h + 1] * o_c[h] + g[:, 2 * hpg + h:2 * hpg + h + 1] * o_w[h] for h in range(hpg)]

    n_chunks = (t0 + TQ + KC - 1) // KC
    last = n_chunks - 1
    tri_bias = jnp.where(lax.broadcasted_iota(jnp.int32, (TQ, TQ), 1) <= lax.broadcasted_iota(jnp.int32, (TQ, TQ), 0),
                         0.0, NEG)

    def scores(c):
        kk = kx_s[pl.ds(pl.multiple_of(c * KC, KC), KC), :]
        return lax.dot_general(q_ext, kk, nt, preferred_element_type=F32)

    def softmax_pv(sc_ref, c, first):
        vv = vsx_s[pl.ds(pl.multiple_of(c * KC, KC), KC), :]
        m_new = jnp.max(sc_ref[...].reshape(hpg, TQ, KC), axis=-1, keepdims=True)
        if not first:
            m_old = m_s[...].reshape(hpg, TQ, 1)
            m_new = jnp.maximum(m_old, m_new)
        p = jnp.exp2((sc_ref[...].reshape(hpg, TQ, KC) - m_new) * scale2)
        pv = jnp.dot(p.astype(BF16).reshape(hpg * TQ, KC), vv, preferred_element_type=F32)
        if first:
            acc_s[...] = pv
        else:
            a = jnp.exp2((m_old - m_new) * scale2).reshape(hpg * TQ, 1)
            acc_s[...] = a * acc_s[...] + pv
        m_s[...] = m_new.reshape(hpg * TQ, 1)

    sca_s[...] = scores(last)
    own = pl.ds(pl.multiple_of(t0 - last * KC, TQ), TQ)
    for h in range(hpg):
        sca_s[h * TQ:(h + 1) * TQ, own] += tri_bias
    scb_s[...] = scores(0)
    softmax_pv(sca_s, last, True)

    def chunk_pair(i, carry):
        c0 = 2 * i
        sca_s[...] = scores(jnp.minimum(c0 + 1, last))
        softmax_pv(scb_s, c0, False)

        @pl.when(c0 + 1 < last)
        def _():
            scb_s[...] = scores(jnp.minimum(c0 + 2, last))
            softmax_pv(sca_s, c0 + 1, False)

        return carry

    lax.fori_loop(0, (last + 1) // 2, chunk_pair, 0)
    acc = acc_s[...]
    o_s = (acc[:, :HEAD_DIM] / acc[:, HEAD_DIM:]).reshape(hpg, TQ, HEAD_DIM)
    for h in range(hpg):
        o_ref[:, heads[h]] = (o_cw[h] + g[:, hpg + h:hpg + h + 1] * o_s[h]).astype(BF16)


def _overlap_matrix(n_half, n_cmp, n_slc):
    c = np.arange(n_half)[None, :] * CMP_STRIDE
    s = np.arange(n_slc)[:, None] * SLC_BLOCK
    ov = np.clip(np.minimum(c + CMP_BLOCK, s + SLC_BLOCK) - np.maximum(c, s), 0, None) / CMP_STRIDE
    ov[:, n_cmp:] = 0.0
    return ov.astype(np.float32)


def nsa_attention(proj, proj_g, tabs, cmp_params, batch, seq):
    n = batch * seq
    nq = seq // TQ
    nh = seq // CMP_STRIDE
    n_cmp = (seq - CMP_BLOCK) // CMP_STRIDE + 1
    n_slc = seq // SLC_BLOCK
    assert nh == LANES and n_slc <= LANES and seq % KC == 0 and seq >= WSPAN
    c_tab, sa_tab, sb_tab = tabs
    posk, posv, w1k, w2k, w1v, w2v = cmp_params
    ov = jnp.asarray(_overlap_matrix(nh, n_cmp, n_slc))
    neg_onehot = np.where(np.arange(seq)[:, None] // SLC_BLOCK == np.arange(HEAD_DIM)[None, :], NEG, 0.0)
    expand = jnp.asarray(neg_onehot.astype(np.float32), dtype=BF16)

    gw = HEADS_PER_GROUP * HEAD_DIM
    qspec = pl.BlockSpec((TQ, gw), lambda b, g, i: (b * nq + i, g))

    def kvspec(slab):
        return pl.BlockSpec((seq, HEAD_DIM), lambda b, g, i: (b, COL_KV // HEAD_DIM + slab * KV_GROUPS + g))

    gspec = pl.BlockSpec((TQ, LANES), lambda b, g, i: (b * nq + i, g))
    tq_spec = pl.BlockSpec((TQ, LANES), lambda b, g, i: (b * nq + i, 0))
    tk_spec = pl.BlockSpec((seq, LANES), lambda b, g, i: (b, 0))

    def full(a):
        return pl.BlockSpec(a.shape, lambda b, g, i: (0,) * a.ndim)

    consts = [posk, posv, w1k, w2k, w1v, w2v, ov, expand]
    return pl.pallas_call(
        _attn_kernel,
        grid=(batch, KV_GROUPS, nq),
        in_specs=[qspec] + [kvspec(s) for s in range(6)] + [gspec] + [tq_spec] * 3 + [tk_spec] * 3
                 + [full(a) for a in consts],
        out_specs=pl.BlockSpec((TQ, gw), lambda b, g, i: (b * nq + i, g)),
        out_shape=jax.ShapeDtypeStruct((n, ATTN_WIDTH), BF16),
        scratch_shapes=[pltpu.VMEM((nh, HEAD_DIM), BF16), pltpu.VMEM((nh, HEAD_DIM), BF16),
                        pltpu.VMEM((seq, 2 * HEAD_DIM), BF16), pltpu.VMEM((seq, HEAD_DIM), BF16),
                        pltpu.VMEM((seq, 2 * HEAD_DIM), BF16), pltpu.VMEM((seq, 2 * HEAD_DIM), BF16),
                        pltpu.VMEM((seq, HEAD_DIM), F32),
                        pltpu.VMEM((HEADS_PER_GROUP * TQ, KC), F32), pltpu.VMEM((HEADS_PER_GROUP * TQ, KC), F32),
                        pltpu.VMEM((HEADS_PER_GROUP * TQ, 1), F32), pltpu.VMEM((HEADS_PER_GROUP * TQ, 2 * HEAD_DIM), F32),
                        pltpu.VMEM((HEADS_PER_GROUP * TQ, WSPAN), F32), pltpu.VMEM((TQ, WSPAN), F32)],
        compiler_params=_cparams("parallel", "parallel", "arbitrary"),
        name="nsa_attention",
    )(proj, proj, proj, proj, proj, proj, proj, proj_g, c_tab, sa_tab, sb_tab, c_tab, sa_tab, sb_tab, *consts)


def _hgrn_level_tables():
    t = np.arange(HC)[:, None]
    s = np.arange(HC)[None, :]
    lv = np.full((HC, HC), -1, np.int32)
    for i, m in enumerate(HG_LEVELS):
        ok = ((t // m) % 2 == 1) & (s // m == t // m - 1)
        lv[ok] = i
    tril = (s <= t).astype(np.float32)
    return lv, tril


def _hgrn_kernel(q_ref, f_ref, i_ref, g_ref, lb_ref, ng_ref, lv_ref, tril_ref, o_ref, st_s, b_s):
    @pl.when(pl.program_id(2) == 0)
    def _():
        st_s[...] = jnp.zeros_like(st_s)

    nt = (((1,), (1,)), ((), ()))
    tn = (((0,), (0,)), ((), ()))
    heads = range(HG_STEP_HEADS)
    cols = [slice(h * HG_DK, (h + 1) * HG_DK) for h in heads]
    lv, tril = lv_ref[...], tril_ref[...]
    lb, ng = lb_ref[0], ng_ref[0]
    row = lax.broadcasted_iota(jnp.int32, (HC, HG_DK), 0)

    f = [lb[:, s] + (1.0 - lb[:, s]) * jax.nn.sigmoid(f_ref[:, s].astype(F32)) for s in cols]
    logf = [jnp.log(x) for x in f]
    kk = [1.0 - x for x in f]
    q = [q_ref[:, s].astype(F32) for s in cols]
    vb = [i_ref[:, s] for s in cols]
    b = [jnp.dot(tril, x, preferred_element_type=F32, precision=lax.Precision.HIGHEST) for x in logf]
    for h in heads:
        b_s[h] = b[h]

    st = [st_s[h] for h in heads]
    o = [lax.dot_general((q[h] * jnp.exp(b[h])).astype(BF16), st[h].astype(BF16), nt, preferred_element_type=F32)
         for h in heads]

    a = [jnp.zeros((HC, HC), F32) for _ in heads]
    for i, m in enumerate(HG_LEVELS):
        odd = (row // m) % 2 == 1
        for h in heads:
            ref_rows = [jnp.broadcast_to(b_s[h, pl.ds((2 * j + 1) * m - 1, 1), :], (2 * m, HG_DK))
                        for j in range(HC // (2 * m))]
            d = b[h] - jnp.concatenate(ref_rows, axis=0)
            x = (jnp.where(odd, q[h], kk[h]) * jnp.exp(-jnp.abs(d))).astype(BF16)
            am = lax.dot_general(x, x, nt, preferred_element_type=F32)
            a[h] = jnp.where(lv == i, am, a[h])
    o = [o[h] + jnp.dot(a[h].astype(BF16), vb[h], preferred_element_type=F32) for h in heads]

    nb = HC // SUBLANES
    r3 = lax.broadcasted_iota(jnp.int32, (nb, SUBLANES, HG_DK), 1)
    b3 = [x.reshape(nb, SUBLANES, HG_DK) for x in b]
    q3 = [x.reshape(nb, SUBLANES, HG_DK) for x in q]
    k3 = [x.reshape(nb, SUBLANES, HG_DK) for x in kk]
    v3 = [x.astype(F32).reshape(nb, SUBLANES, HG_DV) for x in vb]
    od = [jnp.zeros((nb, SUBLANES, HG_DV), F32) for _ in heads]
    for j in range(SUBLANES):
        for h in heads:
            bj = jnp.broadcast_to(b3[h][:, j:j + 1, :], b3[h].shape)
            kj = jnp.broadcast_to(k3[h][:, j:j + 1, :], b3[h].shape)
            vj = jnp.broadcast_to(v3[h][:, j:j + 1, :], b3[h].shape)
            dec = jnp.where(r3 >= j, jnp.exp(jnp.minimum(b3[h] - bj, 0.0)), 0.0)
            od[h] = od[h] + jnp.sum(q3[h] * kj * dec, axis=-1, keepdims=True) * vj

    for h in heads:
        b_last = b_s[h, pl.ds(HC - 1, 1), :]
        kd = (kk[h] * jnp.exp(b_last - b[h])).astype(BF16)
        st_s[h] = st[h] * jnp.exp(b_last) + lax.dot_general(vb[h], kd, tn, preferred_element_type=F32)

        oh = o[h] + od[h].reshape(HC, HG_DV)
        oh = oh * lax.rsqrt(jnp.mean(oh * oh, axis=-1, keepdims=True) + RMS_EPS) * ng[:, cols[h]]
        gg = g_ref[:, cols[h]].astype(F32)
        o_ref[:, cols[h]] = (oh * (gg * jax.nn.sigmoid(gg))).astype(BF16)


def hgrn2(proj, lb, norm_g, batch, seq):
    n = batch * seq
    nc = seq // HC
    hs = HG_STEP_HEADS
    wide = hs * HG_DK
    lv, tril = _hgrn_level_tables()
    base = COL_HG // wide

    def slab(k):
        return pl.BlockSpec((HC, wide), lambda b, h, c: (b * nc + c, base + k * (HG_HEADS // hs) + h))

    vec = pl.BlockSpec((1, 1, wide), lambda b, h, c: (h, 0, 0))
    const = pl.BlockSpec((HC, HC), lambda b, h, c: (0, 0))
    return pl.pallas_call(
        _hgrn_kernel,
        grid=(batch, HG_HEADS // hs, nc),
        in_specs=[slab(0), slab(1), slab(2), slab(3), vec, vec, const, const],
        out_specs=pl.BlockSpec((HC, wide), lambda b, h, c: (b * nc + c, h)),
        out_shape=jax.ShapeDtypeStruct((n, HG_WIDTH), BF16),
        scratch_shapes=[pltpu.VMEM((hs, HG_DV, HG_DK), F32), pltpu.VMEM((hs, HC, HG_DK), F32)],
        compiler_params=_cparams("parallel", "parallel", "arbitrary"),
        name="hgrn2",
    )(proj, proj, proj, proj, lb.reshape(HG_HEADS // hs, 1, wide), norm_g.reshape(HG_HEADS // hs, 1, wide),
      jnp.asarray(lv), jnp.asarray(tril))


def _pool_kernel(p_ref, w_ref, sc_ref, o_ref):
    gi = pl.program_id(1)
    x = p_ref[...].astype(F32)
    t = lax.broadcasted_iota(jnp.int32, x.shape, 0)
    acc = x
    for k in range(len(POOL_WINDOWS)):
        sh = 1 << k
        nxt = acc + jnp.where(t >= sh, pltpu.roll(acc, sh, 0), 0.0)
        acc = jnp.where(k <= gi, nxt, acc)
    width = lax.shift_left(jnp.int32(2), gi)
    cnt = jnp.minimum(t + 1, width).astype(F32)
    mixed = acc / cnt - x
    y = jnp.dot(mixed.astype(BF16), w_ref[0], preferred_element_type=F32) * sc_ref[0]
    o_ref[...] = y.astype(BF16)


def multiscale_pool(proj, w_pool, scale, batch, seq):
    n = batch * seq
    ng = len(POOL_WINDOWS)
    base = COL_POOL // POOL_GROUP
    return pl.pallas_call(
        _pool_kernel,
        grid=(batch, ng),
        in_specs=[pl.BlockSpec((seq, POOL_GROUP), lambda b, g: (b, base + g)),
                  pl.BlockSpec((1, POOL_GROUP, POOL_GROUP), lambda b, g: (g, 0, 0)),
                  pl.BlockSpec((1, 1, POOL_GROUP), lambda b, g: (g, 0, 0))],
        out_specs=pl.BlockSpec((seq, POOL_GROUP), lambda b, g: (b, g)),
        out_shape=jax.ShapeDtypeStruct((n, POOL_WIDTH), BF16),
        compiler_params=_cparams("parallel", "parallel"),
        name="multiscale_pool",
    )(proj, w_pool, scale.reshape(ng, 1, POOL_GROUP))


def _merge_kernel(ya_ref, yb_ref, yc_ref, wa_ref, wb_ref, wc_ref, ga_ref, gb_ref, gc_ref, o_ref):
    def branch(y_ref, w_ref, g_ref):
        up = jnp.dot(y_ref[...], w_ref[...], preferred_element_type=F32)
        return (0.5 + 0.5 * jnp.tanh(0.5 * g_ref[...].astype(F32))) * up

    o_ref[...] = (branch(ya_ref, wa_ref, ga_ref) + branch(yb_ref, wb_ref, gb_ref)
                  + branch(yc_ref, wc_ref, gc_ref)).astype(o_ref.dtype)


def merge_branches(ya, yb, yc, wa, wb, wc, proj, d_model, layer, tm=1024, tn=512):
    n = ya.shape[0]
    tm = min(tm, n)
    gbase = COL_MG // tn
    step = d_model // tn

    def y_spec(a):
        return pl.BlockSpec((tm, a.shape[1]), lambda i, j: (i, 0))

    def w_spec(a):
        return pl.BlockSpec((None, a.shape[1], tn), lambda i, j: (layer, 0, j))

    def g_spec(k):
        return pl.BlockSpec((tm, tn), lambda i, j: (i, gbase + k * step + j))

    return pl.pallas_call(
        _merge_kernel,
        grid=(n // tm, d_model // tn),
        in_specs=[y_spec(ya), y_spec(yb), y_spec(yc), w_spec(wa), w_spec(wb), w_spec(wc),
                  g_spec(0), g_spec(1), g_spec(2)],
        out_specs=pl.BlockSpec((tm, tn), lambda i, j: (i, j)),
        out_shape=jax.ShapeDtypeStruct((n, d_model), BF16),
        compiler_params=_cparams("parallel", "parallel"),
        name="merge_branches",
    )(ya, yb, yc, wa, wb, wc, proj, proj, proj)


def _router_kernel(x_ref, w_ref, b_ref, o_ref):
    logits = lax.dot_general(w_ref[...], x_ref[...], (((1,), (1,)), ((), ())),
                             preferred_element_type=F32)
    scores = jax.nn.sigmoid(logits)
    row = lax.broadcasted_iota(jnp.int32, scores.shape, 0)
    real = row < N_EXPERTS
    sel = _top_mask(jnp.where(real, scores + b_ref[...], -jnp.inf), row, TOP_K, axis=0)
    w = jnp.where(real, sel * scores, 0.0)
    w = w / jnp.sum(w, axis=0, keepdims=True) * ROUTE_SCALE
    o_ref[...] = w.T


def moe_router(x, router_w, router_b, tm=512):
    n, d = x.shape
    tm = min(tm, n)
    w = jnp.zeros((LANES, d), BF16).at[:N_EXPERTS].set(router_w.T.astype(BF16))
    b = jnp.zeros((LANES, 1), F32).at[:N_EXPERTS, 0].set(router_b.astype(F32))
    return pl.pallas_call(
        _router_kernel,
        grid=(n // tm,),
        in_specs=[pl.BlockSpec((tm, d), lambda i: (i, 0)), pl.BlockSpec((LANES, d), lambda i: (0, 0)),
                  pl.BlockSpec((LANES, 1), lambda i: (0, 0))],
        out_specs=pl.BlockSpec((tm, LANES), lambda i: (i, 0)),
        out_shape=jax.ShapeDtypeStruct((n, LANES), F32),
        compiler_params=_cparams("parallel"),
        name="moe_router",
    )(x, w, b)


def _moe_up_kernel(ec, x_ref, w1_ref, w3_ref, gate_ref, ex_ref, o_ref):
    x = x_ref[...]
    w1 = jnp.concatenate([w1_ref[e] for e in range(ec)], axis=1)
    w3 = jnp.concatenate([w3_ref[e] for e in range(ec)], axis=1)
    h1 = jnp.dot(x, w1, preferred_element_type=F32)
    h3 = jnp.dot(x, w3, preferred_element_type=F32)
    g = gate_ref[...]
    g_hi = g.astype(BF16)
    g_lo = (g - g_hi.astype(F32)).astype(BF16)
    gexp = jnp.dot(jnp.concatenate([g_hi, g_lo], axis=1), ex_ref[...], preferred_element_type=F32)
    o_ref[...] = (h1 * jax.nn.sigmoid(h1) * h3 * gexp).astype(BF16)


def moe_hidden(x, gate, w1, w3, layer, tm=1024, ec=4):
    n, d = x.shape
    tm = min(tm, n)
    th = ec * D_EXPERT
    expand = (np.arange(N_EXPERTS * D_EXPERT)[None, :] // D_EXPERT == np.arange(LANES)[:, None])
    expand = jnp.asarray(np.concatenate([expand, expand], axis=0).astype(np.float32), dtype=BF16)
    w_spec = pl.BlockSpec((None, ec, d, D_EXPERT), lambda i, j: (layer, j, 0, 0))
    return pl.pallas_call(
        functools.partial(_moe_up_kernel, ec),
        grid=(n // tm, N_EXPERTS // ec),
        in_specs=[pl.BlockSpec((tm, d), lambda i, j: (i, 0)), w_spec, w_spec,
                  pl.BlockSpec((tm, LANES), lambda i, j: (i, 0)),
                  pl.BlockSpec((2 * LANES, th), lambda i, j: (0, j))],
        out_specs=pl.BlockSpec((tm, th), lambda i, j: (i, j)),
        out_shape=jax.ShapeDtypeStruct((n, N_EXPERTS * D_EXPERT), BF16),
        compiler_params=_cparams("parallel", "parallel"),
        name="moe_hidden",
    )(x, w1, w3, gate, expand)


def _glu_kernel(x_ref, w1_ref, w3_ref, o_ref):
    x = x_ref[...]
    h1 = jnp.dot(x, w1_ref[...], preferred_element_type=F32)
    h3 = jnp.dot(x, w3_ref[...], preferred_element_type=F32)
    o_ref[...] = (h1 * jax.nn.sigmoid(h1) * h3).astype(BF16)


def shared_hidden(x, ws1, ws3, layer, tm=1024):
    n, d = x.shape
    tm = min(tm, n)
    w_spec = pl.BlockSpec((None, d, D_SHARED), lambda i: (layer, 0, 0))
    return pl.pallas_call(
        _glu_kernel,
        grid=(n // tm,),
        in_specs=[pl.BlockSpec((tm, d), lambda i: (i, 0)), w_spec, w_spec],
        out_specs=pl.BlockSpec((tm, D_SHARED), lambda i: (i, 0)),
        out_shape=jax.ShapeDtypeStruct((n, D_SHARED), BF16),
        compiler_params=_cparams("parallel"),
        name="shared_hidden",
    )(x, ws1, ws3)


def _moe_down_kernel(hr_ref, hs_ref, w2_ref, ws2_ref, o_ref):
    y = jnp.dot(hr_ref[...], w2_ref[...], preferred_element_type=F32)
    y = y + jnp.dot(hs_ref[...], ws2_ref[...], preferred_element_type=F32)
    o_ref[...] = y.astype(o_ref.dtype)


def moe_down(hid_r, hid_s, w2, ws2, layer, tm=1024, tn=512):
    n, kr = hid_r.shape
    ks = hid_s.shape[1]
    d = w2.shape[-1]
    tm = min(tm, n)
    return pl.pallas_call(
        _moe_down_kernel,
        grid=(n // tm, d // tn),
        in_specs=[pl.BlockSpec((tm, kr), lambda i, j: (i, 0)),
                  pl.BlockSpec((tm, ks), lambda i, j: (i, 0)),
                  pl.BlockSpec((None, kr, tn), lambda i, j: (layer, 0, j)),
                  pl.BlockSpec((None, ks, tn), lambda i, j: (layer, 0, j))],
        out_specs=pl.BlockSpec((tm, tn), lambda i, j: (i, j)),
        out_shape=jax.ShapeDtypeStruct((n, d), BF16),
        compiler_params=_cparams("parallel", "parallel"),
        name="moe_down",
    )(hid_r, hid_s, w2, ws2)


def _branch_gate_weights(w_t):
    depth, _, d_model = w_t.shape
    ag = w_t[:, ATTN_COLS:ATTN_COLS + 3 * ATTN_HEADS, :].reshape(depth, 3, KV_GROUPS, HEADS_PER_GROUP, d_model)
    ag = jnp.moveaxis(ag, 2, 1).reshape(depth, KV_GROUPS, 3 * HEADS_PER_GROUP, d_model)
    ag = jnp.pad(ag, ((0, 0), (0, 0), (0, LANES - 3 * HEADS_PER_GROUP), (0, 0)))
    return ag.reshape(depth, KV_GROUPS * LANES, d_model)


def _hgrn_lower_bounds(logits):
    lbs = jnp.cumsum(jax.nn.softmax(logits.astype(F32), axis=0), axis=0)
    return lbs - lbs[0:1]


def kernel(x, positions, ln_in_g, ln_in_b, w_in, cmp_pos_k, cmp_pos_v, cmp_w1_k, cmp_w2_k, cmp_w1_v, cmp_w2_v,
           hg_lb_logits, hg_norm_g, pool_w, pool_scale, w_up_attn, w_up_hg, w_up_pool, w_o, ln1_g, ln1_b,
           router_w, router_b, w1, w3, w2, ws1, ws3, ws2, ln2_g, ln2_b):
    batch, seq, d = x.shape
    n = batch * seq
    depth = w_in.shape[0]
    lbs = _hgrn_lower_bounds(hg_lb_logits)
    tabs = rope_tables(positions)
    w_in_t = jnp.swapaxes(w_in, 1, 2).astype(BF16)
    w_gate_t = _branch_gate_weights(w_in_t)
    w1b, w3b = w1.astype(BF16), w3.astype(BF16)
    w2b = w2.astype(BF16).reshape(depth, N_EXPERTS * D_EXPERT, d)
    ws1b, ws3b, ws2b = ws1.astype(BF16), ws3.astype(BF16), ws2.astype(BF16)
    wab, whb, wpb, wob = w_up_attn.astype(BF16), w_up_hg.astype(BF16), w_up_pool.astype(BF16), w_o.astype(BF16)
    h = layer_norm_rows(x.reshape(n, d), ln_in_g, ln_in_b)
    for l in range(depth):
        proj_a = matmul_nt(h, w_in_t, BF16, l, tm=512, tn=ATTN_COLS // 2, rows=ATTN_COLS)
        proj_g = matmul_nt(h, w_gate_t, BF16, l)
        proj_b = matmul_nt(h, w_in_t, BF16, l, row0=ATTN_COLS + 3 * ATTN_HEADS)
        cmp_params = (cmp_pos_k[l], cmp_pos_v[l],
                      cmp_w1_k[l].reshape(CMP_BLOCK, HEAD_DIM, HEAD_DIM).astype(BF16), cmp_w2_k[l].astype(BF16),
                      cmp_w1_v[l].reshape(CMP_BLOCK, HEAD_DIM, HEAD_DIM).astype(BF16), cmp_w2_v[l].astype(BF16))
        ya = nsa_attention(proj_a, proj_g, tabs, cmp_params, batch, seq)
        yb = hgrn2(proj_b, lbs[l], hg_norm_g[l], batch, seq)
        yc = multiscale_pool(proj_b, pool_w[l].astype(BF16), pool_scale[l], batch, seq)
        merged = merge_branches(ya, yb, yc, wab, whb, wpb, proj_b, d, l)
        mix = matmul(merged, wob, BF16, layer=l)
        h = layer_norm_rows(mix, ln1_g[l], ln1_b[l], res=h, alpha=DN_ALPHA)
        gate = moe_router(h, router_w[l], router_b[l])
        hid_r = moe_hidden(h, gate, w1b, w3b, l)
        hid_s = shared_hidden(h, ws1b, ws3b, l)
        ffn = moe_down(hid_r, hid_s, w2b, ws2b, l)
        h = layer_norm_rows(ffn, ln2_g[l], ln2_b[l], res=h, alpha=DN_ALPHA,
                            out_dtype=F32 if l == depth - 1 else BF16)
    return h.reshape(batch, seq, d)
```

```python
import functools

import numpy as np
import jax
import jax.numpy as jnp
from jax import lax
from jax.experimental import pallas as pl
from jax.experimental.pallas import tpu as pltpu

F32 = jnp.float32
BF16 = jnp.bfloat16

DEPTH = 2
HEAD_DIM = 128
ATTN_HEADS = 16
KV_GROUPS = 2
HEADS_PER_GROUP = ATTN_HEADS // KV_GROUPS
ATTN_WIDTH = ATTN_HEADS * HEAD_DIM
KV_WIDTH = KV_GROUPS * HEAD_DIM
ROPE_DIM = HEAD_DIM // 4
ROPE_THETA = 500000.0
CMP_BLOCK = 32
CMP_STRIDE = 16
SLC_BLOCK = 32
SLC_TOPN = 8
WINDOW = 512
HG_HEADS = 8
HG_DK = 128
HG_DV = 128
HG_KWIDTH = HG_HEADS * HG_DK
HG_WIDTH = HG_HEADS * HG_DV
POOL_WINDOWS = (2, 4, 8, 16)
POOL_GROUP = 256
POOL_WIDTH = POOL_GROUP * len(POOL_WINDOWS)
N_EXPERTS = 64
TOP_K = 8
D_EXPERT = 128
D_SHARED = 256
ROUTE_SCALE = 2.5
DN_ALPHA = (2.0 * DEPTH) ** 0.25
LN_EPS = 1e-5
RMS_EPS = 1e-6
NEG = -1e30

LANES = 128
SUBLANES = 8
BF16_SUBLANES = 16
VMEM_LIMIT = 56 * 1024 * 1024

COL_KV = ATTN_WIDTH
ATTN_COLS = ATTN_WIDTH + 6 * KV_WIDTH
COL_HG = 0
COL_POOL = COL_HG + 4 * HG_WIDTH
COL_MG = COL_POOL + POOL_WIDTH

TQ = 256
KC = 512
WSPAN = WINDOW + TQ
HC = 128
HG_LEVELS = (64, 32, 16, 8, 4, 2, 1)
HG_STEP_HEADS = 8


def _cparams(*sem):
    return pltpu.CompilerParams(dimension_semantics=sem, vmem_limit_bytes=VMEM_LIMIT)


def _ln_kernel(alpha, has_res, *refs):
    if has_res:
        x_ref, r_ref, g_ref, b_ref, o_ref = refs
        x = alpha * r_ref[...].astype(F32) + x_ref[...].astype(F32)
    else:
        x_ref, g_ref, b_ref, o_ref = refs
        x = x_ref[...].astype(F32)
    mu = jnp.mean(x, axis=-1, keepdims=True)
    xc = x - mu
    var = jnp.mean(xc * xc, axis=-1, keepdims=True)
    y = xc * lax.rsqrt(var + LN_EPS) * g_ref[...] + b_ref[...]
    o_ref[...] = y.astype(o_ref.dtype)


def layer_norm_rows(x, g, b, res=None, alpha=1.0, out_dtype=BF16, tm=512):
    n, d = x.shape
    row = pl.BlockSpec((tm, d), lambda i: (i, 0))
    vec = pl.BlockSpec((1, d), lambda i: (0, 0))
    ins = [x] + ([res] if res is not None else []) + [g.reshape(1, d), b.reshape(1, d)]
    specs = [row] + ([row] if res is not None else []) + [vec, vec]
    return pl.pallas_call(
        functools.partial(_ln_kernel, alpha, res is not None),
        grid=(n // tm,),
        in_specs=specs,
        out_specs=row,
        out_shape=jax.ShapeDtypeStruct((n, d), out_dtype),
        compiler_params=_cparams("parallel"),
        name="layer_norm",
    )(*ins)


def _mm_kernel(x_ref, w_ref, o_ref):
    o_ref[...] = jnp.dot(x_ref[...], w_ref[...], preferred_element_type=F32).astype(o_ref.dtype)


def matmul(x, w, out_dtype, layer, tm=1024, tn=1024):
    n, k = x.shape
    m = w.shape[-1]
    tm, tn = min(tm, n), min(tn, m)
    assert m % tn == 0
    return pl.pallas_call(
        _mm_kernel,
        grid=(n // tm, m // tn),
        in_specs=[pl.BlockSpec((tm, k), lambda i, j: (i, 0)),
                  pl.BlockSpec((None, k, tn), lambda i, j: (layer, 0, j))],
        out_specs=pl.BlockSpec((tm, tn), lambda i, j: (i, j)),
        out_shape=jax.ShapeDtypeStruct((n, m), out_dtype),
        compiler_params=_cparams("parallel", "parallel"),
        name="matmul",
    )(x, w)


def _mm_nt_kernel(x_ref, w_ref, o_ref):
    o_ref[...] = lax.dot_general(x_ref[...], w_ref[0], (((1,), (1,)), ((), ())),
                                 preferred_element_type=F32).astype(o_ref.dtype)


def matmul_nt(x, wt, out_dtype, layer, tm=1024, tn=1024, row0=0, rows=None):
    n, k = x.shape
    m = rows if rows is not None else wt.shape[1] - row0
    tm, tn = min(tm, n), min(tn, m)
    assert m % tn == 0 and row0 % BF16_SUBLANES == 0
    return pl.pallas_call(
        _mm_nt_kernel,
        grid=(n // tm, m // tn),
        in_specs=[pl.BlockSpec((tm, k), lambda i, j: (i, 0)),
                  pl.BlockSpec((pl.Element(1), pl.Element(tn), pl.Element(k)),
                               lambda i, j: (layer, pl.multiple_of(row0 + j * tn, BF16_SUBLANES), 0))],
        out_specs=pl.BlockSpec((tm, tn), lambda i, j: (i, j)),
        out_shape=jax.ShapeDtypeStruct((n, m), out_dtype),
        compiler_params=_cparams("parallel", "parallel"),
        name="matmul_nt",
    )(x, wt)


def _rope_table_kernel(pos_ref, inv_ref, c_ref, sa_ref, sb_ref):
    ang = pos_ref[...].astype(F32) * inv_ref[...]
    lane = lax.broadcasted_iota(jnp.int32, ang.shape, 1)
    sn = jnp.sin(ang)
    c_ref[...] = jnp.cos(ang)
    sa_ref[...] = jnp.where(lane < ROPE_DIM // 2, -sn, 0.0)
    sb_ref[...] = jnp.where((lane >= ROPE_DIM // 2) & (lane < ROPE_DIM), sn, 0.0)


def rope_tables(positions, tm=1024):
    n = positions.size
    half = ROPE_DIM // 2
    inv = ROPE_THETA ** (-np.arange(half, dtype=np.float32) * 2.0 / ROPE_DIM)
    inv_full = np.zeros((1, LANES), np.float32)
    inv_full[0, :half] = inv
    inv_full[0, half:ROPE_DIM] = inv
    tm = min(tm, n)
    out = jax.ShapeDtypeStruct((n, LANES), F32)
    spec = pl.BlockSpec((tm, LANES), lambda i: (i, 0))
    return pl.pallas_call(
        _rope_table_kernel,
        grid=(n // tm,),
        in_specs=[pl.BlockSpec((tm, 1), lambda i: (i, 0)), pl.BlockSpec((1, LANES), lambda i: (0, 0))],
        out_specs=[spec, spec, spec],
        out_shape=[out, out, out],
        compiler_params=_cparams("parallel"),
        name="rope_tables",
    )(positions.reshape(n, 1), jnp.asarray(inv_full))


def _rope(x, c, sa, sb):
    return x * c + pltpu.roll(x, LANES - ROPE_DIM // 2, 1) * sa + pltpu.roll(x, ROPE_DIM // 2, 1) * sb


def _gelu_tanh(x):
    return 0.5 * x * (1.0 + jnp.tanh(0.7978845608028654 * (x + 0.044715 * x * x * x)))


def _top_mask(val, lane, n_pick, axis=-1):
    sel = jnp.zeros(val.shape, F32)
    for _ in range(n_pick):
        m = jnp.max(val, axis=axis, keepdims=True)
        idx = jnp.min(jnp.where(val == m, lane, LANES), axis=axis, keepdims=True)
        pick = lane == idx
        sel = jnp.where(pick, 1.0, sel)
        val = jnp.where(pick, -jnp.inf, val)
    return sel


def _attn_kernel(q_ref, kc_ref, vc_ref, ks_ref, vs_ref, kw_ref, vw_ref, gate_ref,
                 cq_ref, saq_ref, sbq_ref, ck_ref, sak_ref, sbk_ref,
                 posk_ref, posv_ref, w1k_ref, w2k_ref, w1v_ref, w2v_ref, ov_ref, nege_ref,
                 o_ref, kcmp_s, vcmp_s, kx_s, kwr_s, vsx_s, vwx_s, tmp_s, sca_s, scb_s, m_s, acc_s, sw_s, wbias_s):
    seq = ks_ref.shape[0]
    nh = seq // CMP_STRIDE
    n_cmp = (seq - CMP_BLOCK) // CMP_STRIDE + 1
    hpg = HEADS_PER_GROUP
    scale = HEAD_DIM ** -0.5
    scale2 = scale * 1.4426950408889634
    qt = pl.program_id(2)
    nt = (((1,), (1,)), ((), ()))

    @pl.when(qt == 0)
    def _per_sequence():
        def compress(t_ref, pos_ref, w1_ref, w2_ref, out_s):
            tmp_s[...] = t_ref[...].astype(F32)
            first = jnp.zeros((nh, HEAD_DIM), F32)
            second = jnp.zeros((nh, HEAD_DIM), F32)
            for j in range(CMP_BLOCK):
                x = tmp_s[pl.ds(j % CMP_STRIDE, nh, stride=CMP_STRIDE), :] + pos_ref[j:j + 1, :]
                p = jnp.dot(x.astype(BF16), w1_ref[j], preferred_element_type=F32)
                if j < CMP_STRIDE:
                    first = first + p
                else:
                    second = second + p
            pre = first + pltpu.roll(second, nh - 1, 0)
            hid = _gelu_tanh(pre).astype(BF16)
            out_s[...] = jnp.dot(hid, w2_ref[...], preferred_element_type=F32).astype(BF16)

        compress(kc_ref, posk_ref, w1k_ref, w2k_ref, kcmp_s)
        compress(vc_ref, posv_ref, w1v_ref, w2v_ref, vcmp_s)
        ck, sak, sbk = ck_ref[...], sak_ref[...], sbk_ref[...]
        kx_s[:, :HEAD_DIM] = _rope(ks_ref[...].astype(F32), ck, sak, sbk).astype(BF16)
        kx_s[:, HEAD_DIM:] = nege_ref[...]
        ones = jnp.ones((seq, HEAD_DIM), BF16)
        vsx_s[:, :HEAD_DIM] = vs_ref[...]
        vsx_s[:, HEAD_DIM:] = ones
        vwx_s[:, :HEAD_DIM] = vw_ref[...]
        vwx_s[:, HEAD_DIM:] = ones
        kwr_s[...] = _rope(kw_ref[...].astype(F32), ck, sak, sbk).astype(BF16)

    t0 = qt * TQ
    t = t0 + lax.broadcasted_iota(jnp.int32, (TQ, 1), 0)
    lane = lax.broadcasted_iota(jnp.int32, (TQ, LANES), 1)
    q = q_ref[...]
    qf = q.astype(F32)
    cq, saq, sbq = cq_ref[...], saq_ref[...], sbq_ref[...]
    heads = [slice(h * HEAD_DIM, (h + 1) * HEAD_DIM) for h in range(hpg)]
    q_raw = jnp.concatenate([q[:, s] for s in heads], axis=0)
    q_rot = jnp.concatenate([_rope(qf[:, s], cq, saq, sbq).astype(BF16) for s in heads], axis=0)

    s = lax.dot_general(q_raw, kcmp_s[...], nt, preferred_element_type=F32) * scale
    s3 = s.reshape(hpg, TQ, nh)
    vis = (lane * CMP_STRIDE + (CMP_BLOCK - 1) <= t) & (lane < n_cmp)
    s3 = jnp.where(vis[None], s3, NEG)
    e3 = jnp.exp(s3 - jnp.max(s3, axis=-1, keepdims=True))
    inv = 1.0 / jnp.sum(e3, axis=-1, keepdims=True)
    pb = jnp.where(vis[None], e3 * inv, 0.0).astype(BF16)
    o_c = jnp.dot(pb.reshape(hpg * TQ, nh), vcmp_s[...], preferred_element_type=F32).reshape(hpg, TQ, HEAD_DIM)

    psum = jnp.sum(pb.astype(F32), axis=0)
    imp = lax.dot_general(ov_ref[...], psum, nt, preferred_element_type=F32, precision=lax.Precision.HIGHEST)
    n_slc = imp.shape[0]
    blk = lax.broadcasted_iota(jnp.int32, (n_slc, TQ), 0)
    blk_t = (t0 + lax.broadcasted_iota(jnp.int32, (n_slc, TQ), 1)) // SLC_BLOCK
    causal = blk <= blk_t
    forced = (blk == 0) | (blk == blk_t) | (blk == blk_t - 1)
    val = jnp.where(forced, jnp.inf, jnp.where(causal, imp, -jnp.inf))
    not_sel = jnp.where(causal, 1.0 - _top_mask(val, blk, SLC_TOPN, axis=0), 1.0)
    not_sel = jnp.concatenate([not_sel.T, jnp.zeros((TQ, HEAD_DIM - n_slc), F32)], axis=1).astype(BF16)
    q_ext = jnp.concatenate([q_rot, jnp.concatenate([not_sel] * hpg, axis=0)], axis=1)

    w0 = pl.multiple_of(jnp.maximum(t0 - WINDOW, 0), TQ)
    kk = kwr_s[pl.ds(w0, WSPAN), :]
    vv = vwx_s[pl.ds(w0, WSPAN), :]
    sw_s[...] = lax.dot_general(q_rot, kk, nt, preferred_element_type=F32)
    kpos = w0 + lax.broadcasted_iota(jnp.int32, (TQ, WSPAN), 1)
    wbias_s[...] = jnp.where((kpos <= t) & (t - kpos < WINDOW), 0.0, NEG)
    m_w = jnp.max(sw_s[...].reshape(hpg, TQ, WSPAN) + wbias_s[...][None], axis=-1, keepdims=True)
    pw = jnp.exp2((sw_s[...].reshape(hpg, TQ, WSPAN) + wbias_s[...][None] - m_w) * scale2)
    o_w = jnp.dot(pw.astype(BF16).reshape(hpg * TQ, WSPAN), vv, preferred_element_type=F32)
    o_w = (o_w[:, :HEAD_DIM] / o_w[:, HEAD_DIM:]).reshape(hpg, TQ, HEAD_DIM)

    g = jax.nn.sigmoid(gate_ref[...].astype(F32))
    o_cw = [g[:, h:h + 1] * o_c[h] + g[:, 2 * hpg + h:2 * hpg + h + 1] * o_w[h] for h in range(hpg)]

    n_chunks = (t0 + TQ + KC - 1) // KC
    last = n_chunks - 1
    tri_bias = jnp.where(lax.broadcasted_iota(jnp.int32, (TQ, TQ), 1) <= lax.broadcasted_iota(jnp.int32, (TQ, TQ), 0),
                         0.0, NEG)

    def scores(c):
        kk = kx_s[pl.ds(pl.multiple_of(c * KC, KC), KC), :]
        return lax.dot_general(q_ext, kk, nt, preferred_element_type=F32)

    def softmax_pv(sc_ref, c, first):
        vv = vsx_s[pl.ds(pl.multiple_of(c * KC, KC), KC), :]
        m_new = jnp.max(sc_ref[...].reshape(hpg, TQ, KC), axis=-1, keepdims=True)
        if not first:
            m_old = m_s[...].reshape(hpg, TQ, 1)
            m_new = jnp.maximum(m_old, m_new)
        p = jnp.exp2((sc_ref[...].reshape(hpg, TQ, KC) - m_new) * scale2)
        pv = jnp.dot(p.astype(BF16).reshape(hpg * TQ, KC), vv, preferred_element_type=F32)
        if first:
            acc_s[...] = pv
        else:
            a = jnp.exp2((m_old - m_new) * scale2).reshape(hpg * TQ, 1)
            acc_s[...] = a * acc_s[...] + pv
        m_s[...] = m_new.reshape(hpg * TQ, 1)

    sca_s[...] = scores(last)
    own = pl.ds(pl.multiple_of(t0 - last * KC, TQ), TQ)
    for h in range(hpg):
        sca_s[h * TQ:(h + 1) * TQ, own] += tri_bias
    scb_s[...] = scores(0)
    softmax_pv(sca_s, last, True)

    def chunk_pair(i, carry):
        c0 = 2 * i
        sca_s[...] = scores(jnp.minimum(c0 + 1, last))
        softmax_pv(scb_s, c0, False)

        @pl.when(c0 + 1 < last)
        def _():
            scb_s[...] = scores(jnp.minimum(c0 + 2, last))
            softmax_pv(sca_s, c0 + 1, False)

        return carry

    lax.fori_loop(0, (last + 1) // 2, chunk_pair, 0)
    acc = acc_s[...]
    o_s = (acc[:, :HEAD_DIM] / acc[:, HEAD_DIM:]).reshape(hpg, TQ, HEAD_DIM)
    for h in range(hpg):
        o_ref[:, heads[h]] = (o_cw[h] + g[:, hpg + h:hpg + h + 1] * o_s[h]).astype(BF16)


def _overlap_matrix(n_half, n_cmp, n_slc):
    c = np.arange(n_half)[None, :] * CMP_STRIDE
    s = np.arange(n_slc)[:, None] * SLC_BLOCK
    ov = np.clip(np.minimum(c + CMP_BLOCK, s + SLC_BLOCK) - np.maximum(c, s), 0, None) / CMP_STRIDE
    ov[:, n_cmp:] = 0.0
    return ov.astype(np.float32)


def nsa_attention(proj, proj_g, tabs, cmp_params, batch, seq):
    n = batch * seq
    nq = seq // TQ
    nh = seq // CMP_STRIDE
    n_cmp = (seq - CMP_BLOCK) // CMP_STRIDE + 1
    n_slc = seq // SLC_BLOCK
    assert nh == LANES and n_slc <= LANES and seq % KC == 0 and seq >= WSPAN
    c_tab, sa_tab, sb_tab = tabs
    posk, posv, w1k, w2k, w1v, w2v = cmp_params
    ov = jnp.asarray(_overlap_matrix(nh, n_cmp, n_slc))
    neg_onehot = np.where(np.arange(seq)[:, None] // SLC_BLOCK == np.arange(HEAD_DIM)[None, :], NEG, 0.0)
    expand = jnp.asarray(neg_onehot.astype(np.float32), dtype=BF16)

    gw = HEADS_PER_GROUP * HEAD_DIM
    qspec = pl.BlockSpec((TQ, gw), lambda b, g, i: (b * nq + i, g))

    def kvspec(slab):
        return pl.BlockSpec((seq, HEAD_DIM), lambda b, g, i: (b, COL_KV // HEAD_DIM + slab * KV_GROUPS + g))

    gspec = pl.BlockSpec((TQ, LANES), lambda b, g, i: (b * nq + i, g))
    tq_spec = pl.BlockSpec((TQ, LANES), lambda b, g, i: (b * nq + i, 0))
    tk_spec = pl.BlockSpec((seq, LANES), lambda b, g, i: (b, 0))

    def full(a):
        return pl.BlockSpec(a.shape, lambda b, g, i: (0,) * a.ndim)

    consts = [posk, posv, w1k, w2k, w1v, w2v, ov, expand]
    return pl.pallas_call(
        _attn_kernel,
        grid=(batch, KV_GROUPS, nq),
        in_specs=[qspec] + [kvspec(s) for s in range(6)] + [gspec] + [tq_spec] * 3 + [tk_spec] * 3
                 + [full(a) for a in consts],
        out_specs=pl.BlockSpec((TQ, gw), lambda b, g, i: (b * nq + i, g)),
        out_shape=jax.ShapeDtypeStruct((n, ATTN_WIDTH), BF16),
        scratch_shapes=[pltpu.VMEM((nh, HEAD_DIM), BF16), pltpu.VMEM((nh, HEAD_DIM), BF16),
                        pltpu.VMEM((seq, 2 * HEAD_DIM), BF16), pltpu.VMEM((seq, HEAD_DIM), BF16),
                        pltpu.VMEM((seq, 2 * HEAD_DIM), BF16), pltpu.VMEM((seq, 2 * HEAD_DIM), BF16),
                        pltpu.VMEM((seq, HEAD_DIM), F32),
                        pltpu.VMEM((HEADS_PER_GROUP * TQ, KC), F32), pltpu.VMEM((HEADS_PER_GROUP * TQ, KC), F32),
                        pltpu.VMEM((HEADS_PER_GROUP * TQ, 1), F32), pltpu.VMEM((HEADS_PER_GROUP * TQ, 2 * HEAD_DIM), F32),
                        pltpu.VMEM((HEADS_PER_GROUP * TQ, WSPAN), F32), pltpu.VMEM((TQ, WSPAN), F32)],
        compiler_params=_cparams("parallel", "parallel", "arbitrary"),
        name="nsa_attention",
    )(proj, proj, proj, proj, proj, proj, proj, proj_g, c_tab, sa_tab, sb_tab, c_tab, sa_tab, sb_tab, *consts)


def _hgrn_level_tables():
    t = np.arange(HC)[:, None]
    s = np.arange(HC)[None, :]
    lv = np.full((HC, HC), -1, np.int32)
    for i, m in enumerate(HG_LEVELS):
        ok = ((t // m) % 2 == 1) & (s // m == t // m - 1)
        lv[ok] = i
    lv[t == s] = len(HG_LEVELS)
    tril = (s <= t).astype(np.float32)
    return lv, tril


def _hgrn_kernel(q_ref, f_ref, i_ref, g_ref, lb_ref, ng_ref, lv_ref, tril_ref, o_ref, st_s, b_s):
    @pl.when(pl.program_id(2) == 0)
    def _():
        st_s[...] = jnp.zeros_like(st_s)

    nt = (((1,), (1,)), ((), ()))
    tn = (((0,), (0,)), ((), ()))
    heads = range(HG_STEP_HEADS)
    cols = [slice(h * HG_DK, (h + 1) * HG_DK) for h in heads]
    lv, tril = lv_ref[...], tril_ref[...]
    lb, ng = lb_ref[0], ng_ref[0]
    row = lax.broadcasted_iota(jnp.int32, (HC, HG_DK), 0)

    f = [lb[:, s] + (1.0 - lb[:, s]) * jax.nn.sigmoid(f_ref[:, s].astype(F32)) for s in cols]
    logf = [jnp.log(x) for x in f]
    kk = [1.0 - x for x in f]
    q = [q_ref[:, s].astype(F32) for s in cols]
    vb = [i_ref[:, s] for s in cols]
    b = [jnp.dot(tril, x, preferred_element_type=F32, precision=lax.Precision.HIGHEST) for x in logf]
    for h in heads:
        b_s[h] = b[h]

    st = [st_s[h] for h in heads]
    o = [lax.dot_general((q[h] * jnp.exp(b[h])).astype(BF16), st[h].astype(BF16), nt, preferred_element_type=F32)
         for h in heads]

    nb = HC // SUBLANES
    r3 = lax.broadcasted_iota(jnp.int32, (nb, SUBLANES, HG_DK), 1)
    b3 = [x.reshape(nb, SUBLANES, HG_DK) for x in b]
    a = [lax.dot_general(q[h].astype(BF16), kk[h].astype(BF16), nt, preferred_element_type=F32) for h in heads]
    a = [jnp.where(lv == len(HG_LEVELS), x, 0.0) for x in a]
    for i, m in enumerate(HG_LEVELS):
        odd = (row // m) % 2 == 1
        for h in heads:
            if m >= SUBLANES:
                ref_rows = [jnp.broadcast_to(b_s[h, pl.ds((2 * j + 1) * m - 1, 1), :], (2 * m, HG_DK))
                            for j in range(HC // (2 * m))]
                ref = jnp.concatenate(ref_rows, axis=0)
            else:
                ref = jnp.broadcast_to(b3[h][:, m - 1:m, :], b3[h].shape)
                for j in range(1, SUBLANES // (2 * m)):
                    r = (2 * j + 1) * m - 1
                    ref = jnp.where(r3 >= 2 * j * m, jnp.broadcast_to(b3[h][:, r:r + 1, :], b3[h].shape), ref)
                ref = ref.reshape(HC, HG_DK)
            x = (jnp.where(odd, q[h], kk[h]) * jnp.exp(-jnp.abs(b[h] - ref))).astype(BF16)
            am = lax.dot_general(x, x, nt, preferred_element_type=F32)
            a[h] = jnp.where(lv == i, am, a[h])
    o = [o[h] + jnp.dot(a[h].astype(BF16), vb[h], preferred_element_type=F32) for h in heads]

    for h in heads:
        b_last = b_s[h, pl.ds(HC - 1, 1), :]
        kd = (kk[h] * jnp.exp(b_last - b[h])).astype(BF16)
        st_s[h] = st[h] * jnp.exp(b_last) + lax.dot_general(vb[h], kd, tn, preferred_element_type=F32)

        oh = o[h] * lax.rsqrt(jnp.mean(o[h] * o[h], axis=-1, keepdims=True) + RMS_EPS) * ng[:, cols[h]]
        gg = g_ref[:, cols[h]].astype(F32)
        o_ref[:, cols[h]] = (oh * (gg * jax.nn.sigmoid(gg))).astype(BF16)


def hgrn2(proj, lb, norm_g, batch, seq):
    n = batch * seq
    nc = seq // HC
    hs = HG_STEP_HEADS
    wide = hs * HG_DK
    lv, tril = _hgrn_level_tables()
    base = COL_HG // wide

    def slab(k):
        return pl.BlockSpec((HC, wide), lambda b, h, c: (b * nc + c, base + k * (HG_HEADS // hs) + h))

    vec = pl.BlockSpec((1, 1, wide), lambda b, h, c: (h, 0, 0))
    const = pl.BlockSpec((HC, HC), lambda b, h, c: (0, 0))
    return pl.pallas_call(
        _hgrn_kernel,
        grid=(batch, HG_HEADS // hs, nc),
        in_specs=[slab(0), slab(1), slab(2), slab(3), vec, vec, const, const],
        out_specs=pl.BlockSpec((HC, wide), lambda b, h, c: (b * nc + c, h)),
        out_shape=jax.ShapeDtypeStruct((n, HG_WIDTH), BF16),
        scratch_shapes=[pltpu.VMEM((hs, HG_DV, HG_DK), F32), pltpu.VMEM((hs, HC, HG_DK), F32)],
        compiler_params=_cparams("parallel", "parallel", "arbitrary"),
        name="hgrn2",
    )(proj, proj, proj, proj, lb.reshape(HG_HEADS // hs, 1, wide), norm_g.reshape(HG_HEADS // hs, 1, wide),
      jnp.asarray(lv), jnp.asarray(tril))


def _pool_kernel(p_ref, w_ref, sc_ref, o_ref):
    gi = pl.program_id(1)
    x = p_ref[...].astype(F32)
    t = lax.broadcasted_iota(jnp.int32, x.shape, 0)
    acc = x
    for k in range(len(POOL_WINDOWS)):
        sh = 1 << k
        nxt = acc + jnp.where(t >= sh, pltpu.roll(acc, sh, 0), 0.0)
        acc = jnp.where(k <= gi, nxt, acc)
    width = lax.shift_left(jnp.int32(2), gi)
    cnt = jnp.minimum(t + 1, width).astype(F32)
    mixed = acc / cnt - x
    y = jnp.dot(mixed.astype(BF16), w_ref[0], preferred_element_type=F32) * sc_ref[0]
    o_ref[...] = y.astype(BF16)


def multiscale_pool(proj, w_pool, scale, batch, seq):
    n = batch * seq
    ng = len(POOL_WINDOWS)
    base = COL_POOL // POOL_GROUP
    return pl.pallas_call(
        _pool_kernel,
        grid=(batch, ng),
        in_specs=[pl.BlockSpec((seq, POOL_GROUP), lambda b, g: (b, base + g)),
                  pl.BlockSpec((1, POOL_GROUP, POOL_GROUP), lambda b, g: (g, 0, 0)),
                  pl.BlockSpec((1, 1, POOL_GROUP), lambda b, g: (g, 0, 0))],
        out_specs=pl.BlockSpec((seq, POOL_GROUP), lambda b, g: (b, g)),
        out_shape=jax.ShapeDtypeStruct((n, POOL_WIDTH), BF16),
        compiler_params=_cparams("parallel", "parallel"),
        name="multiscale_pool",
    )(proj, w_pool, scale.reshape(ng, 1, POOL_GROUP))


def _merge_kernel(ya_ref, yb_ref, yc_ref, wa_ref, wb_ref, wc_ref, ga_ref, gb_ref, gc_ref, o_ref):
    def branch(y_ref, w_ref, g_ref):
        up = jnp.dot(y_ref[...], w_ref[...], preferred_element_type=F32)
        return (0.5 + 0.5 * jnp.tanh(0.5 * g_ref[...].astype(F32))) * up

    o_ref[...] = (branch(ya_ref, wa_ref, ga_ref) + branch(yb_ref, wb_ref, gb_ref)
                  + branch(yc_ref, wc_ref, gc_ref)).astype(o_ref.dtype)


def merge_branches(ya, yb, yc, wa, wb, wc, proj, d_model, layer, tm=1024, tn=512):
    n = ya.shape[0]
    tm = min(tm, n)
    gbase = COL_MG // tn
    step = d_model // tn

    def y_spec(a):
        return pl.BlockSpec((tm, a.shape[1]), lambda i, j: (i, 0))

    def w_spec(a):
        return pl.BlockSpec((None, a.shape[1], tn), lambda i, j: (layer, 0, j))

    def g_spec(k):
        return pl.BlockSpec((tm, tn), lambda i, j: (i, gbase + k * step + j))

    return pl.pallas_call(
        _merge_kernel,
        grid=(n // tm, d_model // tn),
        in_specs=[y_spec(ya), y_spec(yb), y_spec(yc), w_spec(wa), w_spec(wb), w_spec(wc),
                  g_spec(0), g_spec(1), g_spec(2)],
        out_specs=pl.BlockSpec((tm, tn), lambda i, j: (i, j)),
        out_shape=jax.ShapeDtypeStruct((n, d_model), BF16),
        compiler_params=_cparams("parallel", "parallel"),
        name="merge_branches",
    )(ya, yb, yc, wa, wb, wc, proj, proj, proj)


def _router_kernel(x_ref, w_ref, b_ref, o_ref):
    logits = lax.dot_general(w_ref[...], x_ref[...], (((1,), (1,)), ((), ())),
                             preferred_element_type=F32)
    scores = jax.nn.sigmoid(logits)
    row = lax.broadcasted_iota(jnp.int32, scores.shape, 0)
    real = row < N_EXPERTS
    sel = _top_mask(jnp.where(real, scores + b_ref[...], -jnp.inf), row, TOP_K, axis=0)
    w = jnp.where(real, sel * scores, 0.0)
    w = w / jnp.sum(w, axis=0, keepdims=True) * ROUTE_SCALE
    o_ref[...] = w.T


def moe_router(x, router_w, router_b, tm=512):
    n, d = x.shape
    tm = min(tm, n)
    w = jnp.zeros((LANES, d), BF16).at[:N_EXPERTS].set(router_w.T.astype(BF16))
    b = jnp.zeros((LANES, 1), F32).at[:N_EXPERTS, 0].set(router_b.astype(F32))
    return pl.pallas_call(
        _router_kernel,
        grid=(n // tm,),
        in_specs=[pl.BlockSpec((tm, d), lambda i: (i, 0)), pl.BlockSpec((LANES, d), lambda i: (0, 0)),
                  pl.BlockSpec((LANES, 1), lambda i: (0, 0))],
        out_specs=pl.BlockSpec((tm, LANES), lambda i: (i, 0)),
        out_shape=jax.ShapeDtypeStruct((n, LANES), F32),
        compiler_params=_cparams("parallel"),
        name="moe_router",
    )(x, w, b)


def _moe_up_kernel(ec, x_ref, w1_ref, w3_ref, gate_ref, ex_ref, o_ref):
    x = x_ref[...]
    w1 = jnp.concatenate([w1_ref[e] for e in range(ec)], axis=1)
    w3 = jnp.concatenate([w3_ref[e] for e in range(ec)], axis=1)
    h1 = jnp.dot(x, w1, preferred_element_type=F32)
    h3 = jnp.dot(x, w3, preferred_element_type=F32)
    g = gate_ref[...]
    g_hi = g.astype(BF16)
    g_lo = (g - g_hi.astype(F32)).astype(BF16)
    gexp = jnp.dot(jnp.concatenate([g_hi, g_lo], axis=1), ex_ref[...], preferred_element_type=F32)
    o_ref[...] = (h1 * jax.nn.sigmoid(h1) * h3 * gexp).astype(BF16)


def moe_hidden(x, gate, w1, w3, layer, tm=1024, ec=4):
    n, d = x.shape
    tm = min(tm, n)
    th = ec * D_EXPERT
    expand = (np.arange(N_EXPERTS * D_EXPERT)[None, :] // D_EXPERT == np.arange(LANES)[:, None])
    expand = jnp.asarray(np.concatenate([expand, expand], axis=0).astype(np.float32), dtype=BF16)
    w_spec = pl.BlockSpec((None, ec, d, D_EXPERT), lambda i, j: (layer, j, 0, 0))
    return pl.pallas_call(
        functools.partial(_moe_up_kernel, ec),
        grid=(n // tm, N_EXPERTS // ec),
        in_specs=[pl.BlockSpec((tm, d), lambda i, j: (i, 0)), w_spec, w_spec,
                  pl.BlockSpec((tm, LANES), lambda i, j: (i, 0)),
                  pl.BlockSpec((2 * LANES, th), lambda i, j: (0, j))],
        out_specs=pl.BlockSpec((tm, th), lambda i, j: (i, j)),
        out_shape=jax.ShapeDtypeStruct((n, N_EXPERTS * D_EXPERT), BF16),
        compiler_params=_cparams("parallel", "parallel"),
        name="moe_hidden",
    )(x, w1, w3, gate, expand)


def _glu_kernel(x_ref, w1_ref, w3_ref, o_ref):
    x = x_ref[...]
    h1 = jnp.dot(x, w1_ref[...], preferred_element_type=F32)
    h3 = jnp.dot(x, w3_ref[...], preferred_element_type=F32)
    o_ref[...] = (h1 * jax.nn.sigmoid(h1) * h3).astype(BF16)


def shared_hidden(x, ws1, ws3, layer, tm=1024):
    n, d = x.shape
    tm = min(tm, n)
    w_spec = pl.BlockSpec((None, d, D_SHARED), lambda i: (layer, 0, 0))
    return pl.pallas_call(
        _glu_kernel,
        grid=(n // tm,),
        in_specs=[pl.BlockSpec((tm, d), lambda i: (i, 0)), w_spec, w_spec],
        out_specs=pl.BlockSpec((tm, D_SHARED), lambda i: (i, 0)),
        out_shape=jax.ShapeDtypeStruct((n, D_SHARED), BF16),
        compiler_params=_cparams("parallel"),
        name="shared_hidden",
    )(x, ws1, ws3)


def _moe_down_kernel(hr_ref, hs_ref, w2_ref, ws2_ref, o_ref):
    y = jnp.dot(hr_ref[...], w2_ref[...], preferred_element_type=F32)
    y = y + jnp.dot(hs_ref[...], ws2_ref[...], preferred_element_type=F32)
    o_ref[...] = y.astype(o_ref.dtype)


def moe_down(hid_r, hid_s, w2, ws2, layer, tm=1024, tn=512):
    n, kr = hid_r.shape
    ks = hid_s.shape[1]
    d = w2.shape[-1]
    tm = min(tm, n)
    return pl.pallas_call(
        _moe_down_kernel,
        grid=(n // tm, d // tn),
        in_specs=[pl.BlockSpec((tm, kr), lambda i, j: (i, 0)),
                  pl.BlockSpec((tm, ks), lambda i, j: (i, 0)),
                  pl.BlockSpec((None, kr, tn), lambda i, j: (layer, 0, j)),
                  pl.BlockSpec((None, ks, tn), lambda i, j: (layer, 0, j))],
        out_specs=pl.BlockSpec((tm, tn), lambda i, j: (i, j)),
        out_shape=jax.ShapeDtypeStruct((n, d), BF16),
        compiler_params=_cparams("parallel", "parallel"),
        name="moe_down",
    )(hid_r, hid_s, w2, ws2)


def _branch_gate_weights(w_t):
    depth, _, d_model = w_t.shape
    ag = w_t[:, ATTN_COLS:ATTN_COLS + 3 * ATTN_HEADS, :].reshape(depth, 3, KV_GROUPS, HEADS_PER_GROUP, d_model)
    ag = jnp.moveaxis(ag, 2, 1).reshape(depth, KV_GROUPS, 3 * HEADS_PER_GROUP, d_model)
    ag = jnp.pad(ag, ((0, 0), (0, 0), (0, LANES - 3 * HEADS_PER_GROUP), (0, 0)))
    return ag.reshape(depth, KV_GROUPS * LANES, d_model)


def _hgrn_lower_bounds(logits):
    lbs = jnp.cumsum(jax.nn.softmax(logits.astype(F32), axis=0), axis=0)
    return lbs - lbs[0:1]


def kernel(x, positions, ln_in_g, ln_in_b, w_in, cmp_pos_k, cmp_pos_v, cmp_w1_k, cmp_w2_k, cmp_w1_v, cmp_w2_v,
           hg_lb_logits, hg_norm_g, pool_w, pool_scale, w_up_attn, w_up_hg, w_up_pool, w_o, ln1_g, ln1_b,
           router_w, router_b, w1, w3, w2, ws1, ws3, ws2, ln2_g, ln2_b):
    batch, seq, d = x.shape
    n = batch * seq
    depth = w_in.shape[0]
    lbs = _hgrn_lower_bounds(hg_lb_logits)
    tabs = rope_tables(positions)
    w_in_t = jnp.swapaxes(w_in, 1, 2).astype(BF16)
    w_gate_t = _branch_gate_weights(w_in_t)
    w1b, w3b = w1.astype(BF16), w3.astype(BF16)
    w2b = w2.astype(BF16).reshape(depth, N_EXPERTS * D_EXPERT, d)
    ws1b, ws3b, ws2b = ws1.astype(BF16), ws3.astype(BF16), ws2.astype(BF16)
    wab, whb, wpb, wob = w_up_attn.astype(BF16), w_up_hg.astype(BF16), w_up_pool.astype(BF16), w_o.astype(BF16)
    h = layer_norm_rows(x.reshape(n, d), ln_in_g, ln_in_b)
    for l in range(depth):
        proj_a = matmul_nt(h, w_in_t, BF16, l, tm=512, tn=ATTN_COLS // 2, rows=ATTN_COLS)
        proj_g = matmul_nt(h, w_gate_t, BF16, l)
        proj_b = matmul_nt(h, w_in_t, BF16, l, row0=ATTN_COLS + 3 * ATTN_HEADS)
        cmp_params = (cmp_pos_k[l], cmp_pos_v[l],
                      cmp_w1_k[l].reshape(CMP_BLOCK, HEAD_DIM, HEAD_DIM).astype(BF16), cmp_w2_k[l].astype(BF16),
                      cmp_w1_v[l].reshape(CMP_BLOCK, HEAD_DIM, HEAD_DIM).astype(BF16), cmp_w2_v[l].astype(BF16))
        ya = nsa_attention(proj_a, proj_g, tabs, cmp_params, batch, seq)
        yb = hgrn2(proj_b, lbs[l], hg_norm_g[l], batch, seq)
        yc = multiscale_pool(proj_b, pool_w[l].astype(BF16), pool_scale[l], batch, seq)
        merged = merge_branches(ya, yb, yc, wab, whb, wpb, proj_b, d, l)
        mix = matmul(merged, wob, BF16, l)
        h = layer_norm_rows(mix, ln1_g[l], ln1_b[l], res=h, alpha=DN_ALPHA)
        gate = moe_router(h, router_w[l], router_b[l])
        hid_r = moe_hidden(h, gate, w1b, w3b, l)
        hid_s = shared_hidden(h, ws1b, ws3b, l)
        ffn = moe_down(hid_r, hid_s, w2b, ws2b, l)
        h = layer_norm_rows(ffn, ln2_g[l], ln2_b[l], res=h, alpha=DN_ALPHA,
                            out_dtype=F32 if l == depth - 1 else BF16)
    return h.reshape(batch, seq, d)
```

```python
import functools

import numpy as np
import jax
import jax.numpy as jnp
from jax import lax
from jax.experimental import pallas as pl
from jax.experimental.pallas import tpu as pltpu

F32 = jnp.float32
BF16 = jnp.bfloat16

DEPTH = 2
HEAD_DIM = 128
ATTN_HEADS = 16
KV_GROUPS = 2
HEADS_PER_GROUP = ATTN_HEADS // KV_GROUPS
ATTN_WIDTH = ATTN_HEADS * HEAD_DIM
KV_WIDTH = KV_GROUPS * HEAD_DIM
ROPE_DIM = HEAD_DIM // 4
ROPE_THETA = 500000.0
CMP_BLOCK = 32
CMP_STRIDE = 16
SLC_BLOCK = 32
SLC_TOPN = 8
WINDOW = 512
HG_HEADS = 8
HG_DK = 128
HG_DV = 128
HG_KWIDTH = HG_HEADS * HG_DK
HG_WIDTH = HG_HEADS * HG_DV
POOL_WINDOWS = (2, 4, 8, 16)
POOL_GROUP = 256
POOL_WIDTH = POOL_GROUP * len(POOL_WINDOWS)
N_EXPERTS = 64
TOP_K = 8
D_EXPERT = 128
D_SHARED = 256
ROUTE_SCALE = 2.5
DN_ALPHA = (2.0 * DEPTH) ** 0.25
LN_EPS = 1e-5
RMS_EPS = 1e-6
NEG = -1e30

LANES = 128
SUBLANES = 8
BF16_SUBLANES = 16
VMEM_LIMIT = 56 * 1024 * 1024

COL_KV = ATTN_WIDTH
ATTN_COLS = ATTN_WIDTH + 6 * KV_WIDTH
COL_HG = 0
COL_POOL = COL_HG + 4 * HG_WIDTH
COL_MG = COL_POOL + POOL_WIDTH

TQ = 256
KC = 512
WSPAN = WINDOW + TQ
HC = 128
HG_LEVELS = (64, 32, 16, 8, 4, 2, 1)
HG_STEP_HEADS = 8


def _cparams(*sem):
    return pltpu.CompilerParams(dimension_semantics=sem, vmem_limit_bytes=VMEM_LIMIT)


def _ln_kernel(alpha, has_res, *refs):
    if has_res:
        x_ref, r_ref, g_ref, b_ref, o_ref = refs
        x = alpha * r_ref[...].astype(F32) + x_ref[...].astype(F32)
    else:
        x_ref, g_ref, b_ref, o_ref = refs
        x = x_ref[...].astype(F32)
    mu = jnp.mean(x, axis=-1, keepdims=True)
    xc = x - mu
    var = jnp.mean(xc * xc, axis=-1, keepdims=True)
    y = xc * lax.rsqrt(var + LN_EPS) * g_ref[...] + b_ref[...]
    o_ref[...] = y.astype(o_ref.dtype)


def layer_norm_rows(x, g, b, res=None, alpha=1.0, out_dtype=BF16, tm=512):
    n, d = x.shape
    row = pl.BlockSpec((tm, d), lambda i: (i, 0))
    vec = pl.BlockSpec((1, d), lambda i: (0, 0))
    ins = [x] + ([res] if res is not None else []) + [g.reshape(1, d), b.reshape(1, d)]
    specs = [row] + ([row] if res is not None else []) + [vec, vec]
    return pl.pallas_call(
        functools.partial(_ln_kernel, alpha, res is not None),
        grid=(n // tm,),
        in_specs=specs,
        out_specs=row,
        out_shape=jax.ShapeDtypeStruct((n, d), out_dtype),
        compiler_params=_cparams("parallel"),
        name="layer_norm",
    )(*ins)


def _mm_kernel(x_ref, w_ref, o_ref):
    o_ref[...] = jnp.dot(x_ref[...], w_ref[...], preferred_element_type=F32).astype(o_ref.dtype)


def matmul(x, w, out_dtype, layer, tm=1024, tn=1024):
    n, k = x.shape
    m = w.shape[-1]
    tm, tn = min(tm, n), min(tn, m)
    assert m % tn == 0
    return pl.pallas_call(
        _mm_kernel,
        grid=(n // tm, m // tn),
        in_specs=[pl.BlockSpec((tm, k), lambda i, j: (i, 0)),
                  pl.BlockSpec((None, k, tn), lambda i, j: (layer, 0, j))],
        out_specs=pl.BlockSpec((tm, tn), lambda i, j: (i, j)),
        out_shape=jax.ShapeDtypeStruct((n, m), out_dtype),
        compiler_params=_cparams("parallel", "parallel"),
        name="matmul",
    )(x, w)


def _mm_nt_kernel(x_ref, w_ref, o_ref):
    o_ref[...] = lax.dot_general(x_ref[...], w_ref[0], (((1,), (1,)), ((), ())),
                                 preferred_element_type=F32).astype(o_ref.dtype)


def matmul_nt(x, wt, out_dtype, layer, tm=1024, tn=1024, row0=0, rows=None):
    n, k = x.shape
    m = rows if rows is not None else wt.shape[1] - row0
    tm, tn = min(tm, n), min(tn, m)
    assert m % tn == 0 and row0 % BF16_SUBLANES == 0
    return pl.pallas_call(
        _mm_nt_kernel,
        grid=(n // tm, m // tn),
        in_specs=[pl.BlockSpec((tm, k), lambda i, j: (i, 0)),
                  pl.BlockSpec((pl.Element(1), pl.Element(tn), pl.Element(k)),
                               lambda i, j: (layer, pl.multiple_of(row0 + j * tn, BF16_SUBLANES), 0))],
        out_specs=pl.BlockSpec((tm, tn), lambda i, j: (i, j)),
        out_shape=jax.ShapeDtypeStruct((n, m), out_dtype),
        compiler_params=_cparams("parallel", "parallel"),
        name="matmul_nt",
    )(x, wt)


def _rope_table_kernel(pos_ref, inv_ref, c_ref, sa_ref, sb_ref):
    ang = pos_ref[...].astype(F32) * inv_ref[...]
    lane = lax.broadcasted_iota(jnp.int32, ang.shape, 1)
    sn = jnp.sin(ang)
    c_ref[...] = jnp.cos(ang)
    sa_ref[...] = jnp.where(lane < ROPE_DIM // 2, -sn, 0.0)
    sb_ref[...] = jnp.where((lane >= ROPE_DIM // 2) & (lane < ROPE_DIM), sn, 0.0)


def rope_tables(positions, tm=1024):
    n = positions.size
    half = ROPE_DIM // 2
    inv = ROPE_THETA ** (-np.arange(half, dtype=np.float32) * 2.0 / ROPE_DIM)
    inv_full = np.zeros((1, LANES), np.float32)
    inv_full[0, :half] = inv
    inv_full[0, half:ROPE_DIM] = inv
    tm = min(tm, n)
    out = jax.ShapeDtypeStruct((n, LANES), F32)
    spec = pl.BlockSpec((tm, LANES), lambda i: (i, 0))
    return pl.pallas_call(
        _rope_table_kernel,
        grid=(n // tm,),
        in_specs=[pl.BlockSpec((tm, 1), lambda i: (i, 0)), pl.BlockSpec((1, LANES), lambda i: (0, 0))],
        out_specs=[spec, spec, spec],
        out_shape=[out, out, out],
        compiler_params=_cparams("parallel"),
        name="rope_tables",
    )(positions.reshape(n, 1), jnp.asarray(inv_full))


def _rope(x, c, sa, sb):
    return x * c + pltpu.roll(x, LANES - ROPE_DIM // 2, 1) * sa + pltpu.roll(x, ROPE_DIM // 2, 1) * sb


def _gelu_tanh(x):
    return 0.5 * x * (1.0 + jnp.tanh(0.7978845608028654 * (x + 0.044715 * x * x * x)))


def _top_mask(val, lane, n_pick, axis=-1):
    sel = jnp.zeros(val.shape, F32)
    for _ in range(n_pick):
        m = jnp.max(val, axis=axis, keepdims=True)
        idx = jnp.min(jnp.where(val == m, lane, LANES), axis=axis, keepdims=True)
        pick = lane == idx
        sel = jnp.where(pick, 1.0, sel)
        val = jnp.where(pick, -jnp.inf, val)
    return sel


def _attn_kernel(q_ref, kc_ref, vc_ref, ks_ref, vs_ref, kw_ref, vw_ref, gate_ref,
                 cq_ref, saq_ref, sbq_ref, ck_ref, sak_ref, sbk_ref,
                 posk_ref, posv_ref, w1k_ref, w2k_ref, w1v_ref, w2v_ref, ov_ref, nege_ref,
                 o_ref, kcmp_s, vcmp_s, kx_s, kwr_s, vsx_s, vwx_s, tmp_s, sca_s, scb_s, m_s, acc_s, sw_s, wbias_s):
    seq = ks_ref.shape[0]
    nh = seq // CMP_STRIDE
    n_cmp = (seq - CMP_BLOCK) // CMP_STRIDE + 1
    hpg = HEADS_PER_GROUP
    scale = HEAD_DIM ** -0.5
    scale2 = scale * 1.4426950408889634
    qt = pl.program_id(2)
    nt = (((1,), (1,)), ((), ()))

    @pl.when(qt == 0)
    def _per_sequence():
        def compress(t_ref, pos_ref, w1_ref, w2_ref, out_s):
            tmp_s[...] = t_ref[...].astype(F32)
            first = jnp.zeros((nh, HEAD_DIM), F32)
            second = jnp.zeros((nh, HEAD_DIM), F32)
            for j in range(CMP_BLOCK):
                x = tmp_s[pl.ds(j % CMP_STRIDE, nh, stride=CMP_STRIDE), :] + pos_ref[j:j + 1, :]
                p = jnp.dot(x.astype(BF16), w1_ref[j], preferred_element_type=F32)
                if j < CMP_STRIDE:
                    first = first + p
                else:
                    second = second + p
            pre = first + pltpu.roll(second, nh - 1, 0)
            hid = _gelu_tanh(pre).astype(BF16)
            out_s[...] = jnp.dot(hid, w2_ref[...], preferred_element_type=F32).astype(BF16)

        compress(kc_ref, posk_ref, w1k_ref, w2k_ref, kcmp_s)
        compress(vc_ref, posv_ref, w1v_ref, w2v_ref, vcmp_s)
        ck, sak, sbk = ck_ref[...], sak_ref[...], sbk_ref[...]
        kx_s[:, :HEAD_DIM] = _rope(ks_ref[...].astype(F32), ck, sak, sbk).astype(BF16)
        kx_s[:, HEAD_DIM:] = nege_ref[...]
        ones = jnp.ones((seq, HEAD_DIM), BF16)
        vsx_s[:, :HEAD_DIM] = vs_ref[...]
        vsx_s[:, HEAD_DIM:] = ones
        vwx_s[:, :HEAD_DIM] = vw_ref[...]
        vwx_s[:, HEAD_DIM:] = ones
        kwr_s[...] = _rope(kw_ref[...].astype(F32), ck, sak, sbk).astype(BF16)

    t0 = qt * TQ
    t = t0 + lax.broadcasted_iota(jnp.int32, (TQ, 1), 0)
    lane = lax.broadcasted_iota(jnp.int32, (TQ, LANES), 1)
    q = q_ref[...]
    qf = q.astype(F32)
    cq, saq, sbq = cq_ref[...], saq_ref[...], sbq_ref[...]
    heads = [slice(h * HEAD_DIM, (h + 1) * HEAD_DIM) for h in range(hpg)]
    q_raw = jnp.concatenate([q[:, s] for s in heads], axis=0)
    q_rot = jnp.concatenate([_rope(qf[:, s], cq, saq, sbq).astype(BF16) for s in heads], axis=0)

    s = lax.dot_general(q_raw, kcmp_s[...], nt, preferred_element_type=F32) * scale
    s3 = s.reshape(hpg, TQ, nh)
    vis = (lane * CMP_STRIDE + (CMP_BLOCK - 1) <= t) & (lane < n_cmp)
    s3 = jnp.where(vis[None], s3, NEG)
    e3 = jnp.exp(s3 - jnp.max(s3, axis=-1, keepdims=True))
    inv = 1.0 / jnp.sum(e3, axis=-1, keepdims=True)
    pb = jnp.where(vis[None], e3 * inv, 0.0).astype(BF16)
    o_c = jnp.dot(pb.reshape(hpg * TQ, nh), vcmp_s[...], preferred_element_type=F32).reshape(hpg, TQ, HEAD_DIM)

    psum = jnp.sum(pb.astype(F32), axis=0)
    imp = lax.dot_general(ov_ref[...], psum, nt, preferred_element_type=F32, precision=lax.Precision.HIGHEST)
    n_slc = imp.shape[0]
    blk = lax.broadcasted_iota(jnp.int32, (n_slc, TQ), 0)
    blk_t = (t0 + lax.broadcasted_iota(jnp.int32, (n_slc, TQ), 1)) // SLC_BLOCK
    causal = blk <= blk_t
    forced = (blk == 0) | (blk == blk_t) | (blk == blk_t - 1)
    val = jnp.where(forced, jnp.inf, jnp.where(causal, imp, -jnp.inf))
    not_sel = jnp.where(causal, 1.0 - _top_mask(val, blk, SLC_TOPN, axis=0), 1.0)
    not_sel = jnp.concatenate([not_sel.T, jnp.zeros((TQ, HEAD_DIM - n_slc), F32)], axis=1).astype(BF16)
    q_ext = jnp.concatenate([q_rot, jnp.concatenate([not_sel] * hpg, axis=0)], axis=1)

    w0 = pl.multiple_of(jnp.maximum(t0 - WINDOW, 0), TQ)
    kk = kwr_s[pl.ds(w0, WSPAN), :]
    vv = vwx_s[pl.ds(w0, WSPAN), :]
    sw_s[...] = lax.dot_general(q_rot, kk, nt, preferred_element_type=F32)
    kpos = w0 + lax.broadcasted_iota(jnp.int32, (TQ, WSPAN), 1)
    wbias_s[...] = jnp.where((kpos <= t) & (t - kpos < WINDOW), 0.0, NEG)
    m_w = jnp.max(sw_s[...].reshape(hpg, TQ, WSPAN) + wbias_s[...][None], axis=-1, keepdims=True)
    pw = jnp.exp2((sw_s[...].reshape(hpg, TQ, WSPAN) + wbias_s[...][None] - m_w) * scale2)
    o_w = jnp.dot(pw.astype(BF16).reshape(hpg * TQ, WSPAN), vv, preferred_element_type=F32)
    o_w = (o_w[:, :HEAD_DIM] / o_w[:, HEAD_DIM:]).reshape(hpg, TQ, HEAD_DIM)

    g = jax.nn.sigmoid(gate_ref[...].astype(F32))
    o_cw = [g[:, h:h + 1] * o_c[h] + g[:, 2 * hpg + h:2 * hpg + h + 1] * o_w[h] for h in range(hpg)]

    n_chunks = (t0 + TQ + KC - 1) // KC
    last = n_chunks - 1
    tri_bias = jnp.where(lax.broadcasted_iota(jnp.int32, (TQ, TQ), 1) <= lax.broadcasted_iota(jnp.int32, (TQ, TQ), 0),
                         0.0, NEG)

    def scores(c):
        kk = kx_s[pl.ds(pl.multiple_of(c * KC, KC), KC), :]
        return lax.dot_general(q_ext, kk, nt, preferred_element_type=F32)

    def softmax_pv(sc_ref, c, first):
        vv = vsx_s[pl.ds(pl.multiple_of(c * KC, KC), KC), :]
        m_new = jnp.max(sc_ref[...].reshape(hpg, TQ, KC), axis=-1, keepdims=True)
        if not first:
            m_old = m_s[...].reshape(hpg, TQ, 1)
            m_new = jnp.maximum(m_old, m_new)
        p = jnp.exp2((sc_ref[...].reshape(hpg, TQ, KC) - m_new) * scale2)
        pv = jnp.dot(p.astype(BF16).reshape(hpg * TQ, KC), vv, preferred_element_type=F32)
        if first:
            acc_s[...] = pv
        else:
            a = jnp.exp2((m_old - m_new) * scale2).reshape(hpg * TQ, 1)
            acc_s[...] = a * acc_s[...] + pv
        m_s[...] = m_new.reshape(hpg * TQ, 1)

    sca_s[...] = scores(last)
    own = pl.ds(pl.multiple_of(t0 - last * KC, TQ), TQ)
    for h in range(hpg):
        sca_s[h * TQ:(h + 1) * TQ, own] += tri_bias
    scb_s[...] = scores(0)
    softmax_pv(sca_s, last, True)

    def chunk_pair(i, carry):
        c0 = 2 * i
        sca_s[...] = scores(jnp.minimum(c0 + 1, last))
        softmax_pv(scb_s, c0, False)

        @pl.when(c0 + 1 < last)
        def _():
            scb_s[...] = scores(jnp.minimum(c0 + 2, last))
            softmax_pv(sca_s, c0 + 1, False)

        return carry

    lax.fori_loop(0, (last + 1) // 2, chunk_pair, 0)
    acc = acc_s[...]
    o_s = (acc[:, :HEAD_DIM] / acc[:, HEAD_DIM:]).reshape(hpg, TQ, HEAD_DIM)
    for h in range(hpg):
        o_ref[:, heads[h]] = (o_cw[h] + g[:, hpg + h:hpg + h + 1] * o_s[h]).astype(BF16)


def _overlap_matrix(n_half, n_cmp, n_slc):
    c = np.arange(n_half)[None, :] * CMP_STRIDE
    s = np.arange(n_slc)[:, None] * SLC_BLOCK
    ov = np.clip(np.minimum(c + CMP_BLOCK, s + SLC_BLOCK) - np.maximum(c, s), 0, None) / CMP_STRIDE
    ov[:, n_cmp:] = 0.0
    return ov.astype(np.float32)


def nsa_attention(proj, proj_g, tabs, cmp_params, batch, seq):
    n = batch * seq
    nq = seq // TQ
    nh = seq // CMP_STRIDE
    n_cmp = (seq - CMP_BLOCK) // CMP_STRIDE + 1
    n_slc = seq // SLC_BLOCK
    assert nh == LANES and n_slc <= LANES and seq % KC == 0 and seq >= WSPAN
    c_tab, sa_tab, sb_tab = tabs
    posk, posv, w1k, w2k, w1v, w2v = cmp_params
    ov = jnp.asarray(_overlap_matrix(nh, n_cmp, n_slc))
    neg_onehot = np.where(np.arange(seq)[:, None] // SLC_BLOCK == np.arange(HEAD_DIM)[None, :], NEG, 0.0)
    expand = jnp.asarray(neg_onehot.astype(np.float32), dtype=BF16)

    gw = HEADS_PER_GROUP * HEAD_DIM
    qspec = pl.BlockSpec((TQ, gw), lambda b, g, i: (b * nq + i, g))

    def kvspec(slab):
        return pl.BlockSpec((seq, HEAD_DIM), lambda b, g, i: (b, COL_KV // HEAD_DIM + slab * KV_GROUPS + g))

    gspec = pl.BlockSpec((TQ, LANES), lambda b, g, i: (b * nq + i, g))
    tq_spec = pl.BlockSpec((TQ, LANES), lambda b, g, i: (b * nq + i, 0))
    tk_spec = pl.BlockSpec((seq, LANES), lambda b, g, i: (b, 0))

    def full(a):
        return pl.BlockSpec(a.shape, lambda b, g, i: (0,) * a.ndim)

    consts = [posk, posv, w1k, w2k, w1v, w2v, ov, expand]
    return pl.pallas_call(
        _attn_kernel,
        grid=(batch, KV_GROUPS, nq),
        in_specs=[qspec] + [kvspec(s) for s in range(6)] + [gspec] + [tq_spec] * 3 + [tk_spec] * 3
                 + [full(a) for a in consts],
        out_specs=pl.BlockSpec((TQ, gw), lambda b, g, i: (b * nq + i, g)),
        out_shape=jax.ShapeDtypeStruct((n, ATTN_WIDTH), BF16),
        scratch_shapes=[pltpu.VMEM((nh, HEAD_DIM), BF16), pltpu.VMEM((nh, HEAD_DIM), BF16),
                        pltpu.VMEM((seq, 2 * HEAD_DIM), BF16), pltpu.VMEM((seq, HEAD_DIM), BF16),
                        pltpu.VMEM((seq, 2 * HEAD_DIM), BF16), pltpu.VMEM((seq, 2 * HEAD_DIM), BF16),
                        pltpu.VMEM((seq, HEAD_DIM), F32),
                        pltpu.VMEM((HEADS_PER_GROUP * TQ, KC), F32), pltpu.VMEM((HEADS_PER_GROUP * TQ, KC), F32),
                        pltpu.VMEM((HEADS_PER_GROUP * TQ, 1), F32), pltpu.VMEM((HEADS_PER_GROUP * TQ, 2 * HEAD_DIM), F32),
                        pltpu.VMEM((HEADS_PER_GROUP * TQ, WSPAN), F32), pltpu.VMEM((TQ, WSPAN), F32)],
        compiler_params=_cparams("parallel", "parallel", "arbitrary"),
        name="nsa_attention",
    )(proj, proj, proj, proj, proj, proj, proj, proj_g, c_tab, sa_tab, sb_tab, c_tab, sa_tab, sb_tab, *consts)


def _hgrn_level_tables():
    t = np.arange(HC)[:, None]
    s = np.arange(HC)[None, :]
    lv = np.full((HC, HC), -1, np.int32)
    for i, m in enumerate(HG_LEVELS):
        ok = ((t // m) % 2 == 1) & (s // m == t // m - 1)
        lv[ok] = i
    lv[t == s] = len(HG_LEVELS)
    tril = (s <= t).astype(np.float32)
    return lv, tril


def _hgrn_kernel(q_ref, f_ref, i_ref, g_ref, lb_ref, ng_ref, lv_ref, tril_ref, o_ref, st_s, b_s):
    @pl.when(pl.program_id(2) == 0)
    def _():
        st_s[...] = jnp.zeros_like(st_s)

    nt = (((1,), (1,)), ((), ()))
    tn = (((0,), (0,)), ((), ()))
    heads = range(HG_STEP_HEADS)
    cols = [slice(h * HG_DK, (h + 1) * HG_DK) for h in heads]
    lv, tril = lv_ref[...], tril_ref[...]
    lb, ng = lb_ref[0], ng_ref[0]
    row = lax.broadcasted_iota(jnp.int32, (HC, HG_DK), 0)

    f = [lb[:, s] + (1.0 - lb[:, s]) * jax.nn.sigmoid(f_ref[:, s].astype(F32)) for s in cols]
    logf = [jnp.log(x) for x in f]
    kk = [1.0 - x for x in f]
    q = [q_ref[:, s].astype(F32) for s in cols]
    vb = [i_ref[:, s] for s in cols]
    b = [jnp.dot(tril, x, preferred_element_type=F32, precision=lax.Precision.HIGHEST) for x in logf]
    for h in heads:
        b_s[h] = b[h]

    st = [st_s[h] for h in heads]
    o = [lax.dot_general((q[h] * jnp.exp(b[h])).astype(BF16), st[h].astype(BF16), nt, preferred_element_type=F32)
         for h in heads]

    nb = HC // SUBLANES
    r3 = lax.broadcasted_iota(jnp.int32, (nb, SUBLANES, HG_DK), 1)
    b3 = [x.reshape(nb, SUBLANES, HG_DK) for x in b]
    a = [lax.dot_general(q[h].astype(BF16), kk[h].astype(BF16), nt, preferred_element_type=F32) for h in heads]
    a = [jnp.where(lv == len(HG_LEVELS), x, 0.0) for x in a]
    for i, m in enumerate(HG_LEVELS):
        odd = (row // m) % 2 == 1
        for h in heads:
            if m >= SUBLANES:
                ref_rows = [jnp.broadcast_to(b_s[h, pl.ds((2 * j + 1) * m - 1, 1), :], (2 * m, HG_DK))
                            for j in range(HC // (2 * m))]
                ref = jnp.concatenate(ref_rows, axis=0)
            else:
                ref = jnp.broadcast_to(b3[h][:, m - 1:m, :], b3[h].shape)
                for j in range(1, SUBLANES // (2 * m)):
                    r = (2 * j + 1) * m - 1
                    ref = jnp.where(r3 >= 2 * j * m, jnp.broadcast_to(b3[h][:, r:r + 1, :], b3[h].shape), ref)
                ref = ref.reshape(HC, HG_DK)
            x = (jnp.where(odd, q[h], kk[h]) * jnp.exp(-jnp.abs(b[h] - ref))).astype(BF16)
            am = lax.dot_general(x, x, nt, preferred_element_type=F32)
            a[h] = jnp.where(lv == i, am, a[h])
    o = [o[h] + jnp.dot(a[h].astype(BF16), vb[h], preferred_element_type=F32) for h in heads]

    for h in heads:
        b_last = b_s[h, pl.ds(HC - 1, 1), :]
        kd = (kk[h] * jnp.exp(b_last - b[h])).astype(BF16)
        st_s[h] = st[h] * jnp.exp(b_last) + lax.dot_general(vb[h], kd, tn, preferred_element_type=F32)

        oh = o[h] * lax.rsqrt(jnp.mean(o[h] * o[h], axis=-1, keepdims=True) + RMS_EPS) * ng[:, cols[h]]
        gg = g_ref[:, cols[h]].astype(F32)
        o_ref[:, cols[h]] = (oh * (gg * jax.nn.sigmoid(gg))).astype(BF16)


def hgrn2(proj, lb, norm_g, batch, seq):
    n = batch * seq
    nc = seq // HC
    hs = HG_STEP_HEADS
    wide = hs * HG_DK
    lv, tril = _hgrn_level_tables()
    base = COL_HG // wide

    def slab(k):
        return pl.BlockSpec((HC, wide), lambda b, h, c: (b * nc + c, base + k * (HG_HEADS // hs) + h))

    vec = pl.BlockSpec((1, 1, wide), lambda b, h, c: (h, 0, 0))
    const = pl.BlockSpec((HC, HC), lambda b, h, c: (0, 0))
    return pl.pallas_call(
        _hgrn_kernel,
        grid=(batch, HG_HEADS // hs, nc),
        in_specs=[slab(0), slab(1), slab(2), slab(3), vec, vec, const, const],
        out_specs=pl.BlockSpec((HC, wide), lambda b, h, c: (b * nc + c, h)),
        out_shape=jax.ShapeDtypeStruct((n, HG_WIDTH), BF16),
        scratch_shapes=[pltpu.VMEM((hs, HG_DV, HG_DK), F32), pltpu.VMEM((hs, HC, HG_DK), F32)],
        compiler_params=_cparams("parallel", "parallel", "arbitrary"),
        name="hgrn2",
    )(proj, proj, proj, proj, lb.reshape(HG_HEADS // hs, 1, wide), norm_g.reshape(HG_HEADS // hs, 1, wide),
      jnp.asarray(lv), jnp.asarray(tril))


def _pool_kernel(p_ref, w_ref, sc_ref, o_ref):
    gi = pl.program_id(1)
    x = p_ref[...].astype(F32)
    t = lax.broadcasted_iota(jnp.int32, x.shape, 0)
    acc = x
    for k in range(len(POOL_WINDOWS)):
        sh = 1 << k
        nxt = acc + jnp.where(t >= sh, pltpu.roll(acc, sh, 0), 0.0)
        acc = jnp.where(k <= gi, nxt, acc)
    width = lax.shift_left(jnp.int32(2), gi)
    cnt = jnp.minimum(t + 1, width).astype(F32)
    mixed = acc / cnt - x
    y = jnp.dot(mixed.astype(BF16), w_ref[0], preferred_element_type=F32) * sc_ref[0]
    o_ref[...] = y.astype(BF16)


def multiscale_pool(proj, w_pool, scale, batch, seq):
    n = batch * seq
    ng = len(POOL_WINDOWS)
    base = COL_POOL // POOL_GROUP
    return pl.pallas_call(
        _pool_kernel,
        grid=(batch, ng),
        in_specs=[pl.BlockSpec((seq, POOL_GROUP), lambda b, g: (b, base + g)),
                  pl.BlockSpec((1, POOL_GROUP, POOL_GROUP), lambda b, g: (g, 0, 0)),
                  pl.BlockSpec((1, 1, POOL_GROUP), lambda b, g: (g, 0, 0))],
        out_specs=pl.BlockSpec((seq, POOL_GROUP), lambda b, g: (b, g)),
        out_shape=jax.ShapeDtypeStruct((n, POOL_WIDTH), BF16),
        compiler_params=_cparams("parallel", "parallel"),
        name="multiscale_pool",
    )(proj, w_pool, scale.reshape(ng, 1, POOL_GROUP))


def _merge_kernel(ya_ref, yb_ref, yc_ref, wa_ref, wb_ref, wc_ref, ga_ref, gb_ref, gc_ref, o_ref):
    def branch(y_ref, w_ref, g_ref):
        up = jnp.dot(y_ref[...], w_ref[...], preferred_element_type=F32)
        return (0.5 + 0.5 * jnp.tanh(0.5 * g_ref[...].astype(F32))) * up

    o_ref[...] = (branch(ya_ref, wa_ref, ga_ref) + branch(yb_ref, wb_ref, gb_ref)
                  + branch(yc_ref, wc_ref, gc_ref)).astype(o_ref.dtype)


def merge_branches(ya, yb, yc, wa, wb, wc, proj, d_model, layer, tm=1024, tn=512):
    n = ya.shape[0]
    tm = min(tm, n)
    gbase = COL_MG // tn
    step = d_model // tn

    def y_spec(a):
        return pl.BlockSpec((tm, a.shape[1]), lambda i, j: (i, 0))

    def w_spec(a):
        return pl.BlockSpec((None, a.shape[1], tn), lambda i, j: (layer, 0, j))

    def g_spec(k):
        return pl.BlockSpec((tm, tn), lambda i, j: (i, gbase + k * step + j))

    return pl.pallas_call(
        _merge_kernel,
        grid=(n // tm, d_model // tn),
        in_specs=[y_spec(ya), y_spec(yb), y_spec(yc), w_spec(wa), w_spec(wb), w_spec(wc),
                  g_spec(0), g_spec(1), g_spec(2)],
        out_specs=pl.BlockSpec((tm, tn), lambda i, j: (i, j)),
        out_shape=jax.ShapeDtypeStruct((n, d_model), BF16),
        compiler_params=_cparams("parallel", "parallel"),
        name="merge_branches",
    )(ya, yb, yc, wa, wb, wc, proj, proj, proj)


def _router_kernel(x_ref, w_ref, b_ref, o_ref):
    logits = lax.dot_general(w_ref[...], x_ref[...], (((1,), (1,)), ((), ())),
                             preferred_element_type=F32)
    scores = jax.nn.sigmoid(logits)
    row = lax.broadcasted_iota(jnp.int32, scores.shape, 0)
    real = row < N_EXPERTS
    sel = _top_mask(jnp.where(real, scores + b_ref[...], -jnp.inf), row, TOP_K, axis=0)
    w = jnp.where(real, sel * scores, 0.0)
    w = w / jnp.sum(w, axis=0, keepdims=True) * ROUTE_SCALE
    o_ref[...] = w.T


def moe_router(x, router_w, router_b, tm=512):
    n, d = x.shape
    tm = min(tm, n)
    w = jnp.zeros((LANES, d), BF16).at[:N_EXPERTS].set(router_w.T.astype(BF16))
    b = jnp.zeros((LANES, 1), F32).at[:N_EXPERTS, 0].set(router_b.astype(F32))
    return pl.pallas_call(
        _router_kernel,
        grid=(n // tm,),
        in_specs=[pl.BlockSpec((tm, d), lambda i: (i, 0)), pl.BlockSpec((LANES, d), lambda i: (0, 0)),
                  pl.BlockSpec((LANES, 1), lambda i: (0, 0))],
        out_specs=pl.BlockSpec((tm, LANES), lambda i: (i, 0)),
        out_shape=jax.ShapeDtypeStruct((n, LANES), F32),
        compiler_params=_cparams("parallel"),
        name="moe_router",
    )(x, w, b)


def _moe_up_kernel(ec, x_ref, w1_ref, w3_ref, gate_ref, o_ref):
    x = x_ref[...]
    w1 = jnp.concatenate([w1_ref[e] for e in range(ec)], axis=1)
    w3 = jnp.concatenate([w3_ref[e] for e in range(ec)], axis=1)
    h1 = jnp.dot(x, w1, preferred_element_type=F32)
    h3 = jnp.dot(x, w3, preferred_element_type=F32)
    hd = h1 * jax.nn.sigmoid(h1) * h3
    g = gate_ref[...]
    lane = lax.broadcasted_iota(jnp.int32, g.shape, 1)
    first = pl.program_id(1) * ec
    for e in range(ec):
        col = jnp.sum(jnp.where(lane == first + e, g, 0.0), axis=-1, keepdims=True)
        cols = slice(e * D_EXPERT, (e + 1) * D_EXPERT)
        o_ref[:, cols] = (hd[:, cols] * col).astype(BF16)


def moe_hidden(x, gate, w1, w3, layer, tm=1024, ec=4):
    n, d = x.shape
    tm = min(tm, n)
    th = ec * D_EXPERT
    w_spec = pl.BlockSpec((None, ec, d, D_EXPERT), lambda i, j: (layer, j, 0, 0))
    return pl.pallas_call(
        functools.partial(_moe_up_kernel, ec),
        grid=(n // tm, N_EXPERTS // ec),
        in_specs=[pl.BlockSpec((tm, d), lambda i, j: (i, 0)), w_spec, w_spec,
                  pl.BlockSpec((tm, LANES), lambda i, j: (i, 0))],
        out_specs=pl.BlockSpec((tm, th), lambda i, j: (i, j)),
        out_shape=jax.ShapeDtypeStruct((n, N_EXPERTS * D_EXPERT), BF16),
        compiler_params=_cparams("parallel", "parallel"),
        name="moe_hidden",
    )(x, w1, w3, gate)


def _glu_kernel(x_ref, w1_ref, w3_ref, o_ref):
    x = x_ref[...]
    h1 = jnp.dot(x, w1_ref[...], preferred_element_type=F32)
    h3 = jnp.dot(x, w3_ref[...], preferred_element_type=F32)
    o_ref[...] = (h1 * jax.nn.sigmoid(h1) * h3).astype(BF16)


def shared_hidden(x, ws1, ws3, layer, tm=1024):
    n, d = x.shape
    tm = min(tm, n)
    w_spec = pl.BlockSpec((None, d, D_SHARED), lambda i: (layer, 0, 0))
    return pl.pallas_call(
        _glu_kernel,
        grid=(n // tm,),
        in_specs=[pl.BlockSpec((tm, d), lambda i: (i, 0)), w_spec, w_spec],
        out_specs=pl.BlockSpec((tm, D_SHARED), lambda i: (i, 0)),
        out_shape=jax.ShapeDtypeStruct((n, D_SHARED), BF16),
        compiler_params=_cparams("parallel"),
        name="shared_hidden",
    )(x, ws1, ws3)


def _moe_down_kernel(hr_ref, hs_ref, w2_ref, ws2_ref, o_ref):
    y = jnp.dot(hr_ref[...], w2_ref[...], preferred_element_type=F32)
    y = y + jnp.dot(hs_ref[...], ws2_ref[...], preferred_element_type=F32)
    o_ref[...] = y.astype(o_ref.dtype)


def moe_down(hid_r, hid_s, w2, ws2, layer, tm=1024, tn=512):
    n, kr = hid_r.shape
    ks = hid_s.shape[1]
    d = w2.shape[-1]
    tm = min(tm, n)
    return pl.pallas_call(
        _moe_down_kernel,
        grid=(n // tm, d // tn),
        in_specs=[pl.BlockSpec((tm, kr), lambda i, j: (i, 0)),
                  pl.BlockSpec((tm, ks), lambda i, j: (i, 0)),
                  pl.BlockSpec((None, kr, tn), lambda i, j: (layer, 0, j)),
                  pl.BlockSpec((None, ks, tn), lambda i, j: (layer, 0, j))],
        out_specs=pl.BlockSpec((tm, tn), lambda i, j: (i, j)),
        out_shape=jax.ShapeDtypeStruct((n, d), BF16),
        compiler_params=_cparams("parallel", "parallel"),
        name="moe_down",
    )(hid_r, hid_s, w2, ws2)


def _branch_gate_weights(w_t):
    depth, _, d_model = w_t.shape
    ag = w_t[:, ATTN_COLS:ATTN_COLS + 3 * ATTN_HEADS, :].reshape(depth, 3, KV_GROUPS, HEADS_PER_GROUP, d_model)
    ag = jnp.moveaxis(ag, 2, 1).reshape(depth, KV_GROUPS, 3 * HEADS_PER_GROUP, d_model)
    ag = jnp.pad(ag, ((0, 0), (0, 0), (0, LANES - 3 * HEADS_PER_GROUP), (0, 0)))
    return ag.reshape(depth, KV_GROUPS * LANES, d_model)


def _hgrn_lower_bounds(logits):
    lbs = jnp.cumsum(jax.nn.softmax(logits.astype(F32), axis=0), axis=0)
    return lbs - lbs[0:1]


def kernel(x, positions, ln_in_g, ln_in_b, w_in, cmp_pos_k, cmp_pos_v, cmp_w1_k, cmp_w2_k, cmp_w1_v, cmp_w2_v,
           hg_lb_logits, hg_norm_g, pool_w, pool_scale, w_up_attn, w_up_hg, w_up_pool, w_o, ln1_g, ln1_b,
           router_w, router_b, w1, w3, w2, ws1, ws3, ws2, ln2_g, ln2_b):
    batch, seq, d = x.shape
    n = batch * seq
    depth = w_in.shape[0]
    lbs = _hgrn_lower_bounds(hg_lb_logits)
    tabs = rope_tables(positions)
    w_in_t = jnp.swapaxes(w_in, 1, 2).astype(BF16)
    w_gate_t = _branch_gate_weights(w_in_t)
    w1b, w3b = w1.astype(BF16), w3.astype(BF16)
    w2b = w2.astype(BF16).reshape(depth, N_EXPERTS * D_EXPERT, d)
    ws1b, ws3b, ws2b = ws1.astype(BF16), ws3.astype(BF16), ws2.astype(BF16)
    wab, whb, wpb, wob = w_up_attn.astype(BF16), w_up_hg.astype(BF16), w_up_pool.astype(BF16), w_o.astype(BF16)
    h = layer_norm_rows(x.reshape(n, d), ln_in_g, ln_in_b)
    for l in range(depth):
        proj_a = matmul_nt(h, w_in_t, BF16, l, tm=512, tn=ATTN_COLS // 2, rows=ATTN_COLS)
        proj_g = matmul_nt(h, w_gate_t, BF16, l)
        proj_b = matmul_nt(h, w_in_t, BF16, l, row0=ATTN_COLS + 3 * ATTN_HEADS)
        cmp_params = (cmp_pos_k[l], cmp_pos_v[l],
                      cmp_w1_k[l].reshape(CMP_BLOCK, HEAD_DIM, HEAD_DIM).astype(BF16), cmp_w2_k[l].astype(BF16),
                      cmp_w1_v[l].reshape(CMP_BLOCK, HEAD_DIM, HEAD_DIM).astype(BF16), cmp_w2_v[l].astype(BF16))
        ya = nsa_attention(proj_a, proj_g, tabs, cmp_params, batch, seq)
        yb = hgrn2(proj_b, lbs[l], hg_norm_g[l], batch, seq)
        yc = multiscale_pool(proj_b, pool_w[l].astype(BF16), pool_scale[l], batch, seq)
        merged = merge_branches(ya, yb, yc, wab, whb, wpb, proj_b, d, l)
        mix = matmul(merged, wob, BF16, l)
        h = layer_norm_rows(mix, ln1_g[l], ln1_b[l], res=h, alpha=DN_ALPHA)
        gate = moe_router(h, router_w[l], router_b[l])
        hid_r = moe_hidden(h, gate, w1b, w3b, l)
        hid_s = shared_hidden(h, ws1b, ws3b, l)
        ffn = moe_down(hid_r, hid_s, w2b, ws2b, l)
        h = layer_norm_rows(ffn, ln2_g[l], ln2_b[l], res=h, alpha=DN_ALPHA,
                            out_dtype=F32 if l == depth - 1 else BF16)
    return h.reshape(batch, seq, d)
```

```python
import functools

import numpy as np
import jax
import jax.numpy as jnp
from jax import lax
from jax.experimental import pallas as pl
from jax.experimental.pallas import tpu as pltpu

F32 = jnp.float32
BF16 = jnp.bfloat16

DEPTH = 2
HEAD_DIM = 128
ATTN_HEADS = 16
KV_GROUPS = 2
HEADS_PER_GROUP = ATTN_HEADS // KV_GROUPS
ATTN_WIDTH = ATTN_HEADS * HEAD_DIM
KV_WIDTH = KV_GROUPS * HEAD_DIM
ROPE_DIM = HEAD_DIM // 4
ROPE_THETA = 500000.0
CMP_BLOCK = 32
CMP_STRIDE = 16
SLC_BLOCK = 32
SLC_TOPN = 8
WINDOW = 512
HG_HEADS = 8
HG_DK = 128
HG_DV = 128
HG_KWIDTH = HG_HEADS * HG_DK
HG_WIDTH = HG_HEADS * HG_DV
POOL_WINDOWS = (2, 4, 8, 16)
POOL_GROUP = 256
POOL_WIDTH = POOL_GROUP * len(POOL_WINDOWS)
N_EXPERTS = 64
TOP_K = 8
D_EXPERT = 128
D_SHARED = 256
ROUTE_SCALE = 2.5
DN_ALPHA = (2.0 * DEPTH) ** 0.25
LN_EPS = 1e-5
RMS_EPS = 1e-6
NEG = -1e30

LANES = 128
SUBLANES = 8
BF16_SUBLANES = 16
VMEM_LIMIT = 56 * 1024 * 1024

COL_KV = ATTN_WIDTH
ATTN_COLS = ATTN_WIDTH + 6 * KV_WIDTH
COL_HG = 0
COL_POOL = COL_HG + 4 * HG_WIDTH
COL_MG = COL_POOL + POOL_WIDTH

TQ = 256
KC = 512
WSPAN = WINDOW + TQ
HC = 128
HG_LEVELS = (64, 32, 16, 8, 4, 2, 1)
HG_STEP_HEADS = 8


def _cparams(*sem):
    return pltpu.CompilerParams(dimension_semantics=sem, vmem_limit_bytes=VMEM_LIMIT)


def _ln_kernel(alpha, has_res, tail, *refs):
    n_in = 2 if has_res else 1
    g_ref, b_ref = refs[n_in], refs[n_in + 1]
    rest = refs[n_in + 2:]
    x = refs[0][...].astype(F32)
    if has_res:
        x = alpha * refs[1][...].astype(F32) + x
    mu = jnp.mean(x, axis=-1, keepdims=True)
    xc = x - mu
    var = jnp.mean(xc * xc, axis=-1, keepdims=True)
    y = xc * lax.rsqrt(var + LN_EPS) * g_ref[...] + b_ref[...]
    if tail is None:
        (o_ref,) = rest
        o_ref[...] = y.astype(o_ref.dtype)
        return
    h = y.astype(BF16)
    if tail == "proj":
        w_ref, o_ref, p_ref = rest
        p_ref[...] = lax.dot_general(h, w_ref[...], (((1,), (1,)), ((), ())),
                                     preferred_element_type=F32).astype(p_ref.dtype)
    else:
        w_ref, rb_ref, o_ref, gate_ref = rest
        gate_ref[...] = _route(w_ref[...], rb_ref[...], h)
    o_ref[...] = h


def layer_norm_rows(x, g, b, res=None, alpha=1.0, out_dtype=BF16, tm=512, proj=None, router=None):
    n, d = x.shape
    row = pl.BlockSpec((tm, d), lambda i: (i, 0))
    vec = pl.BlockSpec((1, d), lambda i: (0, 0))
    ins = [x] + ([res] if res is not None else []) + [g.reshape(1, d), b.reshape(1, d)]
    specs = [row] + ([row] if res is not None else []) + [vec, vec]
    out_specs, out_shape, tail = [row], [jax.ShapeDtypeStruct((n, d), out_dtype)], None
    if proj is not None:
        wt, layer = proj
        m = wt.shape[1]
        tail = "proj"
        ins.append(wt)
        specs.append(pl.BlockSpec((None, m, d), lambda i: (layer, 0, 0)))
        out_specs.append(pl.BlockSpec((tm, m), lambda i: (i, 0)))
        out_shape.append(jax.ShapeDtypeStruct((n, m), BF16))
    elif router is not None:
        router_w, router_b = router
        tail = "router"
        ins += [jnp.zeros((LANES, d), BF16).at[:N_EXPERTS].set(router_w.T.astype(BF16)),
                jnp.zeros((LANES, 1), F32).at[:N_EXPERTS, 0].set(router_b.astype(F32))]
        specs += [pl.BlockSpec((LANES, d), lambda i: (0, 0)), pl.BlockSpec((LANES, 1), lambda i: (0, 0))]
        out_specs.append(pl.BlockSpec((tm, LANES), lambda i: (i, 0)))
        out_shape.append(jax.ShapeDtypeStruct((n, LANES), F32))
    assert tail is None or out_dtype == BF16
    outs = pl.pallas_call(
        functools.partial(_ln_kernel, alpha, res is not None, tail),
        grid=(n // tm,),
        in_specs=specs,
        out_specs=out_specs,
        out_shape=out_shape,
        compiler_params=_cparams("parallel"),
        name="layer_norm",
    )(*ins)
    return outs[0] if tail is None else tuple(outs)


def _mm_kernel(x_ref, w_ref, o_ref):
    o_ref[...] = jnp.dot(x_ref[...], w_ref[...], preferred_element_type=F32).astype(o_ref.dtype)


def matmul(x, w, out_dtype, layer, tm=1024, tn=1024):
    n, k = x.shape
    m = w.shape[-1]
    tm, tn = min(tm, n), min(tn, m)
    assert m % tn == 0
    return pl.pallas_call(
        _mm_kernel,
        grid=(n // tm, m // tn),
        in_specs=[pl.BlockSpec((tm, k), lambda i, j: (i, 0)),
                  pl.BlockSpec((None, k, tn), lambda i, j: (layer, 0, j))],
        out_specs=pl.BlockSpec((tm, tn), lambda i, j: (i, j)),
        out_shape=jax.ShapeDtypeStruct((n, m), out_dtype),
        compiler_params=_cparams("parallel", "parallel"),
        name="matmul",
    )(x, w)


def _mm_nt_kernel(x_ref, w_ref, o_ref):
    o_ref[...] = lax.dot_general(x_ref[...], w_ref[0], (((1,), (1,)), ((), ())),
                                 preferred_element_type=F32).astype(o_ref.dtype)


def matmul_nt(x, wt, out_dtype, layer, tm=1024, tn=1024, row0=0, rows=None):
    n, k = x.shape
    m = rows if rows is not None else wt.shape[1] - row0
    tm, tn = min(tm, n), min(tn, m)
    assert m % tn == 0 and row0 % BF16_SUBLANES == 0
    return pl.pallas_call(
        _mm_nt_kernel,
        grid=(n // tm, m // tn),
        in_specs=[pl.BlockSpec((tm, k), lambda i, j: (i, 0)),
                  pl.BlockSpec((pl.Element(1), pl.Element(tn), pl.Element(k)),
                               lambda i, j: (layer, pl.multiple_of(row0 + j * tn, BF16_SUBLANES), 0))],
        out_specs=pl.BlockSpec((tm, tn), lambda i, j: (i, j)),
        out_shape=jax.ShapeDtypeStruct((n, m), out_dtype),
        compiler_params=_cparams("parallel", "parallel"),
        name="matmul_nt",
    )(x, wt)


def _rope_table_kernel(pos_ref, inv_ref, c_ref, sa_ref, sb_ref):
    ang = pos_ref[...].astype(F32) * inv_ref[...]
    lane = lax.broadcasted_iota(jnp.int32, ang.shape, 1)
    sn = jnp.sin(ang)
    c_ref[...] = jnp.cos(ang)
    sa_ref[...] = jnp.where(lane < ROPE_DIM // 2, -sn, 0.0)
    sb_ref[...] = jnp.where((lane >= ROPE_DIM // 2) & (lane < ROPE_DIM), sn, 0.0)


def rope_tables(positions, tm=1024):
    n = positions.size
    half = ROPE_DIM // 2
    inv = ROPE_THETA ** (-np.arange(half, dtype=np.float32) * 2.0 / ROPE_DIM)
    inv_full = np.zeros((1, LANES), np.float32)
    inv_full[0, :half] = inv
    inv_full[0, half:ROPE_DIM] = inv
    tm = min(tm, n)
    out = jax.ShapeDtypeStruct((n, LANES), F32)
    spec = pl.BlockSpec((tm, LANES), lambda i: (i, 0))
    return pl.pallas_call(
        _rope_table_kernel,
        grid=(n // tm,),
        in_specs=[pl.BlockSpec((tm, 1), lambda i: (i, 0)), pl.BlockSpec((1, LANES), lambda i: (0, 0))],
        out_specs=[spec, spec, spec],
        out_shape=[out, out, out],
        compiler_params=_cparams("parallel"),
        name="rope_tables",
    )(positions.reshape(n, 1), jnp.asarray(inv_full))


def _rope(x, c, sa, sb):
    return x * c + pltpu.roll(x, LANES - ROPE_DIM // 2, 1) * sa + pltpu.roll(x, ROPE_DIM // 2, 1) * sb


def _gelu_tanh(x):
    return 0.5 * x * (1.0 + jnp.tanh(0.7978845608028654 * (x + 0.044715 * x * x * x)))


def _top_mask(val, lane, n_pick, axis=-1):
    sel = jnp.zeros(val.shape, F32)
    for _ in range(n_pick):
        m = jnp.max(val, axis=axis, keepdims=True)
        idx = jnp.min(jnp.where(val == m, lane, LANES), axis=axis, keepdims=True)
        pick = lane == idx
        sel = jnp.where(pick, 1.0, sel)
        val = jnp.where(pick, -jnp.inf, val)
    return sel


def _attn_kernel(q_ref, kc_ref, vc_ref, ks_ref, vs_ref, kw_ref, vw_ref, gate_ref,
                 cq_ref, saq_ref, sbq_ref, ck_ref, sak_ref, sbk_ref,
                 posk_ref, posv_ref, w1k_ref, w2k_ref, w1v_ref, w2v_ref, ov_ref, nege_ref,
                 o_ref, kcmp_s, vcmp_s, kx_s, kwr_s, vsx_s, vwx_s, tmp_s, sca_s, scb_s, m_s, acc_s, sw_s, wbias_s):
    seq = ks_ref.shape[0]
    nh = seq // CMP_STRIDE
    n_cmp = (seq - CMP_BLOCK) // CMP_STRIDE + 1
    hpg = HEADS_PER_GROUP
    scale = HEAD_DIM ** -0.5
    scale2 = scale * 1.4426950408889634
    qt = pl.program_id(2)
    nt = (((1,), (1,)), ((), ()))

    @pl.when(qt == 0)
    def _per_sequence():
        def compress(t_ref, pos_ref, w1_ref, w2_ref, out_s):
            tmp_s[...] = t_ref[...].astype(F32)
            first = jnp.zeros((nh, HEAD_DIM), F32)
            second = jnp.zeros((nh, HEAD_DIM), F32)
            for j in range(CMP_BLOCK):
                x = tmp_s[pl.ds(j % CMP_STRIDE, nh, stride=CMP_STRIDE), :] + pos_ref[j:j + 1, :]
                p = jnp.dot(x.astype(BF16), w1_ref[j], preferred_element_type=F32)
                if j < CMP_STRIDE:
                    first = first + p
                else:
                    second = second + p
            pre = first + pltpu.roll(second, nh - 1, 0)
            hid = _gelu_tanh(pre).astype(BF16)
            out_s[...] = jnp.dot(hid, w2_ref[...], preferred_element_type=F32).astype(BF16)

        compress(kc_ref, posk_ref, w1k_ref, w2k_ref, kcmp_s)
        compress(vc_ref, posv_ref, w1v_ref, w2v_ref, vcmp_s)
        ck, sak, sbk = ck_ref[...], sak_ref[...], sbk_ref[...]
        kx_s[:, :HEAD_DIM] = _rope(ks_ref[...].astype(F32), ck, sak, sbk).astype(BF16)
        kx_s[:, HEAD_DIM:] = nege_ref[...]
        ones = jnp.ones((seq, HEAD_DIM), BF16)
        vsx_s[:, :HEAD_DIM] = vs_ref[...]
        vsx_s[:, HEAD_DIM:] = ones
        vwx_s[:, :HEAD_DIM] = vw_ref[...]
        vwx_s[:, HEAD_DIM:] = ones
        kwr_s[...] = _rope(kw_ref[...].astype(F32), ck, sak, sbk).astype(BF16)

    t0 = qt * TQ
    t = t0 + lax.broadcasted_iota(jnp.int32, (TQ, 1), 0)
    lane = lax.broadcasted_iota(jnp.int32, (TQ, LANES), 1)
    q = q_ref[...]
    qf = q.astype(F32)
    cq, saq, sbq = cq_ref[...], saq_ref[...], sbq_ref[...]
    heads = [slice(h * HEAD_DIM, (h + 1) * HEAD_DIM) for h in range(hpg)]
    q_raw = jnp.concatenate([q[:, s] for s in heads], axis=0)
    q_rot = jnp.concatenate([_rope(qf[:, s], cq, saq, sbq).astype(BF16) for s in heads], axis=0)

    s = lax.dot_general(q_raw, kcmp_s[...], nt, preferred_element_type=F32) * scale
    s3 = s.reshape(hpg, TQ, nh)
    vis = (lane * CMP_STRIDE + (CMP_BLOCK - 1) <= t) & (lane < n_cmp)
    s3 = jnp.where(vis[None], s3, NEG)
    e3 = jnp.exp(s3 - jnp.max(s3, axis=-1, keepdims=True))
    inv = 1.0 / jnp.sum(e3, axis=-1, keepdims=True)
    pb = jnp.where(vis[None], e3 * inv, 0.0).astype(BF16)
    o_c = jnp.dot(pb.reshape(hpg * TQ, nh), vcmp_s[...], preferred_element_type=F32).reshape(hpg, TQ, HEAD_DIM)

    psum = jnp.sum(pb.astype(F32), axis=0)
    imp = lax.dot_general(ov_ref[...], psum, nt, preferred_element_type=F32, precision=lax.Precision.HIGHEST)
    n_slc = imp.shape[0]
    blk = lax.broadcasted_iota(jnp.int32, (n_slc, TQ), 0)
    blk_t = (t0 + lax.broadcasted_iota(jnp.int32, (n_slc, TQ), 1)) // SLC_BLOCK
    causal = blk <= blk_t
    forced = (blk == 0) | (blk == blk_t) | (blk == blk_t - 1)
    val = jnp.where(forced, jnp.inf, jnp.where(causal, imp, -jnp.inf))
    not_sel = jnp.where(causal, 1.0 - _top_mask(val, blk, SLC_TOPN, axis=0), 1.0)
    not_sel = jnp.concatenate([not_sel.T, jnp.zeros((TQ, HEAD_DIM - n_slc), F32)], axis=1).astype(BF16)
    q_ext = jnp.concatenate([q_rot, jnp.concatenate([not_sel] * hpg, axis=0)], axis=1)

    w0 = pl.multiple_of(jnp.maximum(t0 - WINDOW, 0), TQ)
    kk = kwr_s[pl.ds(w0, WSPAN), :]
    vv = vwx_s[pl.ds(w0, WSPAN), :]
    sw_s[...] = lax.dot_general(q_rot, kk, nt, preferred_element_type=F32)
    kpos = w0 + lax.broadcasted_iota(jnp.int32, (TQ, WSPAN), 1)
    wbias_s[...] = jnp.where((kpos <= t) & (t - kpos < WINDOW), 0.0, NEG)
    m_w = jnp.max(sw_s[...].reshape(hpg, TQ, WSPAN) + wbias_s[...][None], axis=-1, keepdims=True)
    pw = jnp.exp2((sw_s[...].reshape(hpg, TQ, WSPAN) + wbias_s[...][None] - m_w) * scale2)
    o_w = jnp.dot(pw.astype(BF16).reshape(hpg * TQ, WSPAN), vv, preferred_element_type=F32)
    o_w = (o_w[:, :HEAD_DIM] / o_w[:, HEAD_DIM:]).reshape(hpg, TQ, HEAD_DIM)

    g = jax.nn.sigmoid(gate_ref[...].astype(F32))
    o_cw = [g[:, h:h + 1] * o_c[h] + g[:, 2 * hpg + h:2 * hpg + h + 1] * o_w[h] for h in range(hpg)]

    n_chunks = (t0 + TQ + KC - 1) // KC
    last = n_chunks - 1
    tri_bias = jnp.where(lax.broadcasted_iota(jnp.int32, (TQ, TQ), 1) <= lax.broadcasted_iota(jnp.int32, (TQ, TQ), 0),
                         0.0, NEG)

    def scores(c):
        kk = kx_s[pl.ds(pl.multiple_of(c * KC, KC), KC), :]
        return lax.dot_general(q_ext, kk, nt, preferred_element_type=F32)

    def softmax_pv(sc_ref, c, first):
        vv = vsx_s[pl.ds(pl.multiple_of(c * KC, KC), KC), :]
        m_new = jnp.max(sc_ref[...].reshape(hpg, TQ, KC), axis=-1, keepdims=True)
        if not first:
            m_old = m_s[...].reshape(hpg, TQ, 1)
            m_new = jnp.maximum(m_old, m_new)
        p = jnp.exp2((sc_ref[...].reshape(hpg, TQ, KC) - m_new) * scale2)
        pv = jnp.dot(p.astype(BF16).reshape(hpg * TQ, KC), vv, preferred_element_type=F32)
        if first:
            acc_s[...] = pv
        else:
            a = jnp.exp2((m_old - m_new) * scale2).reshape(hpg * TQ, 1)
            acc_s[...] = a * acc_s[...] + pv
        m_s[...] = m_new.reshape(hpg * TQ, 1)

    sca_s[...] = scores(last)
    own = pl.ds(pl.multiple_of(t0 - last * KC, TQ), TQ)
    for h in range(hpg):
        sca_s[h * TQ:(h + 1) * TQ, own] += tri_bias
    scb_s[...] = scores(0)
    softmax_pv(sca_s, last, True)

    def chunk_pair(i, carry):
        c0 = 2 * i
        sca_s[...] = scores(jnp.minimum(c0 + 1, last))
        softmax_pv(scb_s, c0, False)

        @pl.when(c0 + 1 < last)
        def _():
            scb_s[...] = scores(jnp.minimum(c0 + 2, last))
            softmax_pv(sca_s, c0 + 1, False)

        return carry

    lax.fori_loop(0, (last + 1) // 2, chunk_pair, 0)
    acc = acc_s[...]
    o_s = (acc[:, :HEAD_DIM] / acc[:, HEAD_DIM:]).reshape(hpg, TQ, HEAD_DIM)
    for h in range(hpg):
        o_ref[:, heads[h]] = (o_cw[h] + g[:, hpg + h:hpg + h + 1] * o_s[h]).astype(BF16)


def _overlap_matrix(n_half, n_cmp, n_slc):
    c = np.arange(n_half)[None, :] * CMP_STRIDE
    s = np.arange(n_slc)[:, None] * SLC_BLOCK
    ov = np.clip(np.minimum(c + CMP_BLOCK, s + SLC_BLOCK) - np.maximum(c, s), 0, None) / CMP_STRIDE
    ov[:, n_cmp:] = 0.0
    return ov.astype(np.float32)


def nsa_attention(proj, proj_g, tabs, cmp_params, batch, seq):
    n = batch * seq
    nq = seq // TQ
    nh = seq // CMP_STRIDE
    n_cmp = (seq - CMP_BLOCK) // CMP_STRIDE + 1
    n_slc = seq // SLC_BLOCK
    assert nh == LANES and n_slc <= LANES and seq % KC == 0 and seq >= WSPAN
    c_tab, sa_tab, sb_tab = tabs
    posk, posv, w1k, w2k, w1v, w2v = cmp_params
    ov = jnp.asarray(_overlap_matrix(nh, n_cmp, n_slc))
    neg_onehot = np.where(np.arange(seq)[:, None] // SLC_BLOCK == np.arange(HEAD_DIM)[None, :], NEG, 0.0)
    expand = jnp.asarray(neg_onehot.astype(np.float32), dtype=BF16)

    gw = HEADS_PER_GROUP * HEAD_DIM
    qspec = pl.BlockSpec((TQ, gw), lambda b, g, i: (b * nq + i, g))

    def kvspec(slab):
        return pl.BlockSpec((seq, HEAD_DIM), lambda b, g, i: (b, COL_KV // HEAD_DIM + slab * KV_GROUPS + g))

    gspec = pl.BlockSpec((TQ, LANES), lambda b, g, i: (b * nq + i, g))
    tq_spec = pl.BlockSpec((TQ, LANES), lambda b, g, i: (b * nq + i, 0))
    tk_spec = pl.BlockSpec((seq, LANES), lambda b, g, i: (b, 0))

    def full(a):
        return pl.BlockSpec(a.shape, lambda b, g, i: (0,) * a.ndim)

    consts = [posk, posv, w1k, w2k, w1v, w2v, ov, expand]
    return pl.pallas_call(
        _attn_kernel,
        grid=(batch, KV_GROUPS, nq),
        in_specs=[qspec] + [kvspec(s) for s in range(6)] + [gspec] + [tq_spec] * 3 + [tk_spec] * 3
                 + [full(a) for a in consts],
        out_specs=pl.BlockSpec((TQ, gw), lambda b, g, i: (b * nq + i, g)),
        out_shape=jax.ShapeDtypeStruct((n, ATTN_WIDTH), BF16),
        scratch_shapes=[pltpu.VMEM((nh, HEAD_DIM), BF16), pltpu.VMEM((nh, HEAD_DIM), BF16),
                        pltpu.VMEM((seq, 2 * HEAD_DIM), BF16), pltpu.VMEM((seq, HEAD_DIM), BF16),
                        pltpu.VMEM((seq, 2 * HEAD_DIM), BF16), pltpu.VMEM((seq, 2 * HEAD_DIM), BF16),
                        pltpu.VMEM((seq, HEAD_DIM), F32),
                        pltpu.VMEM((HEADS_PER_GROUP * TQ, KC), F32), pltpu.VMEM((HEADS_PER_GROUP * TQ, KC), F32),
                        pltpu.VMEM((HEADS_PER_GROUP * TQ, 1), F32), pltpu.VMEM((HEADS_PER_GROUP * TQ, 2 * HEAD_DIM), F32),
                        pltpu.VMEM((HEADS_PER_GROUP * TQ, WSPAN), F32), pltpu.VMEM((TQ, WSPAN), F32)],
        compiler_params=_cparams("parallel", "parallel", "arbitrary"),
        name="nsa_attention",
    )(proj, proj, proj, proj, proj, proj, proj, proj_g, c_tab, sa_tab, sb_tab, c_tab, sa_tab, sb_tab, *consts)


def _hgrn_level_tables():
    t = np.arange(HC)[:, None]
    s = np.arange(HC)[None, :]
    lv = np.full((HC, HC), -1, np.int32)
    for i, m in enumerate(HG_LEVELS):
        ok = ((t // m) % 2 == 1) & (s // m == t // m - 1)
        lv[ok] = i
    lv[t == s] = len(HG_LEVELS)
    tril = (s <= t).astype(np.float32)
    return lv, tril


def _hgrn_kernel(q_ref, f_ref, i_ref, g_ref, lb_ref, ng_ref, lv_ref, tril_ref, o_ref, st_s, b_s):
    @pl.when(pl.program_id(2) == 0)
    def _():
        st_s[...] = jnp.zeros_like(st_s)

    nt = (((1,), (1,)), ((), ()))
    tn = (((0,), (0,)), ((), ()))
    heads = range(HG_STEP_HEADS)
    cols = [slice(h * HG_DK, (h + 1) * HG_DK) for h in heads]
    lv, tril = lv_ref[...], tril_ref[...]
    lb, ng = lb_ref[0], ng_ref[0]
    row = lax.broadcasted_iota(jnp.int32, (HC, HG_DK), 0)

    f = [lb[:, s] + (1.0 - lb[:, s]) * jax.nn.sigmoid(f_ref[:, s].astype(F32)) for s in cols]
    logf = [jnp.log(x) for x in f]
    kk = [1.0 - x for x in f]
    q = [q_ref[:, s].astype(F32) for s in cols]
    vb = [i_ref[:, s] for s in cols]
    b = [jnp.dot(tril, x, preferred_element_type=F32, precision=lax.Precision.HIGHEST) for x in logf]
    for h in heads:
        b_s[h] = b[h]

    st = [st_s[h] for h in heads]
    o = [lax.dot_general((q[h] * jnp.exp(b[h])).astype(BF16), st[h].astype(BF16), nt, preferred_element_type=F32)
         for h in heads]

    nb = HC // SUBLANES
    r3 = lax.broadcasted_iota(jnp.int32, (nb, SUBLANES, HG_DK), 1)
    b3 = [x.reshape(nb, SUBLANES, HG_DK) for x in b]
    a = [lax.dot_general(q[h].astype(BF16), kk[h].astype(BF16), nt, preferred_element_type=F32) for h in heads]
    a = [jnp.where(lv == len(HG_LEVELS), x, 0.0) for x in a]
    for i, m in enumerate(HG_LEVELS):
        odd = (row // m) % 2 == 1
        for h in heads:
            if m >= SUBLANES:
                ref_rows = [jnp.broadcast_to(b_s[h, pl.ds((2 * j + 1) * m - 1, 1), :], (2 * m, HG_DK))
                            for j in range(HC // (2 * m))]
                ref = jnp.concatenate(ref_rows, axis=0)
            else:
                ref = jnp.broadcast_to(b3[h][:, m - 1:m, :], b3[h].shape)
                for j in range(1, SUBLANES // (2 * m)):
                    r = (2 * j + 1) * m - 1
                    ref = jnp.where(r3 >= 2 * j * m, jnp.broadcast_to(b3[h][:, r:r + 1, :], b3[h].shape), ref)
                ref = ref.reshape(HC, HG_DK)
            x = (jnp.where(odd, q[h], kk[h]) * jnp.exp(-jnp.abs(b[h] - ref))).astype(BF16)
            am = lax.dot_general(x, x, nt, preferred_element_type=F32)
            a[h] = jnp.where(lv == i, am, a[h])
    o = [o[h] + jnp.dot(a[h].astype(BF16), vb[h], preferred_element_type=F32) for h in heads]

    for h in heads:
        b_last = b_s[h, pl.ds(HC - 1, 1), :]
        kd = (kk[h] * jnp.exp(b_last - b[h])).astype(BF16)
        st_s[h] = st[h] * jnp.exp(b_last) + lax.dot_general(vb[h], kd, tn, preferred_element_type=F32)

        oh = o[h] * lax.rsqrt(jnp.mean(o[h] * o[h], axis=-1, keepdims=True) + RMS_EPS) * ng[:, cols[h]]
        gg = g_ref[:, cols[h]].astype(F32)
        o_ref[:, cols[h]] = (oh * (gg * jax.nn.sigmoid(gg))).astype(BF16)


def hgrn2(proj, lb, norm_g, batch, seq):
    n = batch * seq
    nc = seq // HC
    hs = HG_STEP_HEADS
    wide = hs * HG_DK
    lv, tril = _hgrn_level_tables()
    base = COL_HG // wide

    def slab(k):
        return pl.BlockSpec((HC, wide), lambda b, h, c: (b * nc + c, base + k * (HG_HEADS // hs) + h))

    vec = pl.BlockSpec((1, 1, wide), lambda b, h, c: (h, 0, 0))
    const = pl.BlockSpec((HC, HC), lambda b, h, c: (0, 0))
    return pl.pallas_call(
        _hgrn_kernel,
        grid=(batch, HG_HEADS // hs, nc),
        in_specs=[slab(0), slab(1), slab(2), slab(3), vec, vec, const, const],
        out_specs=pl.BlockSpec((HC, wide), lambda b, h, c: (b * nc + c, h)),
        out_shape=jax.ShapeDtypeStruct((n, HG_WIDTH), BF16),
        scratch_shapes=[pltpu.VMEM((hs, HG_DV, HG_DK), F32), pltpu.VMEM((hs, HC, HG_DK), F32)],
        compiler_params=_cparams("parallel", "parallel", "arbitrary"),
        name="hgrn2",
    )(proj, proj, proj, proj, lb.reshape(HG_HEADS // hs, 1, wide), norm_g.reshape(HG_HEADS // hs, 1, wide),
      jnp.asarray(lv), jnp.asarray(tril))


def _pool_kernel(p_ref, w_ref, sc_ref, o_ref):
    gi = pl.program_id(1)
    x = p_ref[...].astype(F32)
    t = lax.broadcasted_iota(jnp.int32, x.shape, 0)
    acc = x
    for k in range(len(POOL_WINDOWS)):
        sh = 1 << k
        nxt = acc + jnp.where(t >= sh, pltpu.roll(acc, sh, 0), 0.0)
        acc = jnp.where(k <= gi, nxt, acc)
    width = lax.shift_left(jnp.int32(2), gi)
    cnt = jnp.minimum(t + 1, width).astype(F32)
    mixed = acc / cnt - x
    y = jnp.dot(mixed.astype(BF16), w_ref[0], preferred_element_type=F32) * sc_ref[0]
    o_ref[...] = y.astype(BF16)


def multiscale_pool(proj, w_pool, scale, batch, seq):
    n = batch * seq
    ng = len(POOL_WINDOWS)
    base = COL_POOL // POOL_GROUP
    return pl.pallas_call(
        _pool_kernel,
        grid=(batch, ng),
        in_specs=[pl.BlockSpec((seq, POOL_GROUP), lambda b, g: (b, base + g)),
                  pl.BlockSpec((1, POOL_GROUP, POOL_GROUP), lambda b, g: (g, 0, 0)),
                  pl.BlockSpec((1, 1, POOL_GROUP), lambda b, g: (g, 0, 0))],
        out_specs=pl.BlockSpec((seq, POOL_GROUP), lambda b, g: (b, g)),
        out_shape=jax.ShapeDtypeStruct((n, POOL_WIDTH), BF16),
        compiler_params=_cparams("parallel", "parallel"),
        name="multiscale_pool",
    )(proj, w_pool, scale.reshape(ng, 1, POOL_GROUP))


def _merge_kernel(ya_ref, yb_ref, yc_ref, wa_ref, wb_ref, wc_ref, ga_ref, gb_ref, gc_ref, o_ref):
    def branch(y_ref, w_ref, g_ref):
        up = jnp.dot(y_ref[...], w_ref[...], preferred_element_type=F32)
        return (0.5 + 0.5 * jnp.tanh(0.5 * g_ref[...].astype(F32))) * up

    o_ref[...] = (branch(ya_ref, wa_ref, ga_ref) + branch(yb_ref, wb_ref, gb_ref)
                  + branch(yc_ref, wc_ref, gc_ref)).astype(o_ref.dtype)


def merge_branches(ya, yb, yc, wa, wb, wc, proj, d_model, layer, tm=1024, tn=512):
    n = ya.shape[0]
    tm = min(tm, n)
    gbase = COL_MG // tn
    step = d_model // tn

    def y_spec(a):
        return pl.BlockSpec((tm, a.shape[1]), lambda i, j: (i, 0))

    def w_spec(a):
        return pl.BlockSpec((None, a.shape[1], tn), lambda i, j: (layer, 0, j))

    def g_spec(k):
        return pl.BlockSpec((tm, tn), lambda i, j: (i, gbase + k * step + j))

    return pl.pallas_call(
        _merge_kernel,
        grid=(n // tm, d_model // tn),
        in_specs=[y_spec(ya), y_spec(yb), y_spec(yc), w_spec(wa), w_spec(wb), w_spec(wc),
                  g_spec(0), g_spec(1), g_spec(2)],
        out_specs=pl.BlockSpec((tm, tn), lambda i, j: (i, j)),
        out_shape=jax.ShapeDtypeStruct((n, d_model), BF16),
        compiler_params=_cparams("parallel", "parallel"),
        name="merge_branches",
    )(ya, yb, yc, wa, wb, wc, proj, proj, proj)


def _route(w, bias, h):
    logits = lax.dot_general(w, h, (((1,), (1,)), ((), ())), preferred_element_type=F32)
    scores = jax.nn.sigmoid(logits)
    row = lax.broadcasted_iota(jnp.int32, scores.shape, 0)
    real = row < N_EXPERTS
    sel = _top_mask(jnp.where(real, scores + bias, -jnp.inf), row, TOP_K, axis=0)
    g = jnp.where(real, sel * scores, 0.0)
    g = g / jnp.sum(g, axis=0, keepdims=True) * ROUTE_SCALE
    return g.T


def _moe_up_kernel(ec, x_ref, w1_ref, w3_ref, gate_ref, o_ref):
    x = x_ref[...]
    w1 = jnp.concatenate([w1_ref[e] for e in range(ec)], axis=1)
    w3 = jnp.concatenate([w3_ref[e] for e in range(ec)], axis=1)
    h1 = jnp.dot(x, w1, preferred_element_type=F32)
    h3 = jnp.dot(x, w3, preferred_element_type=F32)
    hd = h1 * jax.nn.sigmoid(h1) * h3
    g = gate_ref[...]
    lane = lax.broadcasted_iota(jnp.int32, g.shape, 1)
    first = pl.program_id(1) * ec
    for e in range(ec):
        col = jnp.sum(jnp.where(lane == first + e, g, 0.0), axis=-1, keepdims=True)
        cols = slice(e * D_EXPERT, (e + 1) * D_EXPERT)
        o_ref[:, cols] = (hd[:, cols] * col).astype(BF16)


def moe_hidden(x, gate, w1, w3, layer, tm=1024, ec=4):
    n, d = x.shape
    tm = min(tm, n)
    th = ec * D_EXPERT
    w_spec = pl.BlockSpec((None, ec, d, D_EXPERT), lambda i, j: (layer, j, 0, 0))
    return pl.pallas_call(
        functools.partial(_moe_up_kernel, ec),
        grid=(n // tm, N_EXPERTS // ec),
        in_specs=[pl.BlockSpec((tm, d), lambda i, j: (i, 0)), w_spec, w_spec,
                  pl.BlockSpec((tm, LANES), lambda i, j: (i, 0))],
        out_specs=pl.BlockSpec((tm, th), lambda i, j: (i, j)),
        out_shape=jax.ShapeDtypeStruct((n, N_EXPERTS * D_EXPERT), BF16),
        compiler_params=_cparams("parallel", "parallel"),
        name="moe_hidden",
    )(x, w1, w3, gate)


def _glu_kernel(x_ref, w1_ref, w3_ref, o_ref):
    x = x_ref[...]
    h1 = jnp.dot(x, w1_ref[...], preferred_element_type=F32)
    h3 = jnp.dot(x, w3_ref[...], preferred_element_type=F32)
    o_ref[...] = (h1 * jax.nn.sigmoid(h1) * h3).astype(BF16)


def shared_hidden(x, ws1, ws3, layer, tm=1024):
    n, d = x.shape
    tm = min(tm, n)
    w_spec = pl.BlockSpec((None, d, D_SHARED), lambda i: (layer, 0, 0))
    return pl.pallas_call(
        _glu_kernel,
        grid=(n // tm,),
        in_specs=[pl.BlockSpec((tm, d), lambda i: (i, 0)), w_spec, w_spec],
        out_specs=pl.BlockSpec((tm, D_SHARED), lambda i: (i, 0)),
        out_shape=jax.ShapeDtypeStruct((n, D_SHARED), BF16),
        compiler_params=_cparams("parallel"),
        name="shared_hidden",
    )(x, ws1, ws3)


def _moe_down_kernel(hr_ref, hs_ref, w2_ref, ws2_ref, o_ref):
    y = jnp.dot(hr_ref[...], w2_ref[...], preferred_element_type=F32)
    y = y + jnp.dot(hs_ref[...], ws2_ref[...], preferred_element_type=F32)
    o_ref[...] = y.astype(o_ref.dtype)


def moe_down(hid_r, hid_s, w2, ws2, layer, tm=1024, tn=512):
    n, kr = hid_r.shape
    ks = hid_s.shape[1]
    d = w2.shape[-1]
    tm = min(tm, n)
    return pl.pallas_call(
        _moe_down_kernel,
        grid=(n // tm, d // tn),
        in_specs=[pl.BlockSpec((tm, kr), lambda i, j: (i, 0)),
                  pl.BlockSpec((tm, ks), lambda i, j: (i, 0)),
                  pl.BlockSpec((None, kr, tn), lambda i, j: (layer, 0, j)),
                  pl.BlockSpec((None, ks, tn), lambda i, j: (layer, 0, j))],
        out_specs=pl.BlockSpec((tm, tn), lambda i, j: (i, j)),
        out_shape=jax.ShapeDtypeStruct((n, d), BF16),
        compiler_params=_cparams("parallel", "parallel"),
        name="moe_down",
    )(hid_r, hid_s, w2, ws2)


def _branch_gate_weights(w_t):
    depth, _, d_model = w_t.shape
    ag = w_t[:, ATTN_COLS:ATTN_COLS + 3 * ATTN_HEADS, :].reshape(depth, 3, KV_GROUPS, HEADS_PER_GROUP, d_model)
    ag = jnp.moveaxis(ag, 2, 1).reshape(depth, KV_GROUPS, 3 * HEADS_PER_GROUP, d_model)
    ag = jnp.pad(ag, ((0, 0), (0, 0), (0, LANES - 3 * HEADS_PER_GROUP), (0, 0)))
    return ag.reshape(depth, KV_GROUPS * LANES, d_model)


def _hgrn_lower_bounds(logits):
    lbs = jnp.cumsum(jax.nn.softmax(logits.astype(F32), axis=0), axis=0)
    return lbs - lbs[0:1]


def kernel(x, positions, ln_in_g, ln_in_b, w_in, cmp_pos_k, cmp_pos_v, cmp_w1_k, cmp_w2_k, cmp_w1_v, cmp_w2_v,
           hg_lb_logits, hg_norm_g, pool_w, pool_scale, w_up_attn, w_up_hg, w_up_pool, w_o, ln1_g, ln1_b,
           router_w, router_b, w1, w3, w2, ws1, ws3, ws2, ln2_g, ln2_b):
    batch, seq, d = x.shape
    n = batch * seq
    depth = w_in.shape[0]
    lbs = _hgrn_lower_bounds(hg_lb_logits)
    tabs = rope_tables(positions)
    w_in_t = jnp.swapaxes(w_in, 1, 2).astype(BF16)
    w_gate_t = _branch_gate_weights(w_in_t)
    w1b, w3b = w1.astype(BF16), w3.astype(BF16)
    w2b = w2.astype(BF16).reshape(depth, N_EXPERTS * D_EXPERT, d)
    ws1b, ws3b, ws2b = ws1.astype(BF16), ws3.astype(BF16), ws2.astype(BF16)
    wab, whb, wpb, wob = w_up_attn.astype(BF16), w_up_hg.astype(BF16), w_up_pool.astype(BF16), w_o.astype(BF16)
    h, proj_g = layer_norm_rows(x.reshape(n, d), ln_in_g, ln_in_b, proj=(w_gate_t, 0))
    for l in range(depth):
        proj_a = matmul_nt(h, w_in_t, BF16, l, tm=512, tn=ATTN_COLS // 2, rows=ATTN_COLS)
        proj_b = matmul_nt(h, w_in_t, BF16, l, row0=ATTN_COLS + 3 * ATTN_HEADS)
        cmp_params = (cmp_pos_k[l], cmp_pos_v[l],
                      cmp_w1_k[l].reshape(CMP_BLOCK, HEAD_DIM, HEAD_DIM).astype(BF16), cmp_w2_k[l].astype(BF16),
                      cmp_w1_v[l].reshape(CMP_BLOCK, HEAD_DIM, HEAD_DIM).astype(BF16), cmp_w2_v[l].astype(BF16))
        ya = nsa_attention(proj_a, proj_g, tabs, cmp_params, batch, seq)
        yb = hgrn2(proj_b, lbs[l], hg_norm_g[l], batch, seq)
        yc = multiscale_pool(proj_b, pool_w[l].astype(BF16), pool_scale[l], batch, seq)
        merged = merge_branches(ya, yb, yc, wab, whb, wpb, proj_b, d, l)
        mix = matmul(merged, wob, BF16, l)
        h, gate = layer_norm_rows(mix, ln1_g[l], ln1_b[l], res=h, alpha=DN_ALPHA, router=(router_w[l], router_b[l]))
        hid_r = moe_hidden(h, gate, w1b, w3b, l)
        hid_s = shared_hidden(h, ws1b, ws3b, l)
        ffn = moe_down(hid_r, hid_s, w2b, ws2b, l)
        if l + 1 < depth:
            h, proj_g = layer_norm_rows(ffn, ln2_g[l], ln2_b[l], res=h, alpha=DN_ALPHA, proj=(w_gate_t, l + 1))
        else:
            h = layer_norm_rows(ffn, ln2_g[l], ln2_b[l], res=h, alpha=DN_ALPHA, out_dtype=F32)
    return h.reshape(batch, seq, d)
```

```python
import functools

import numpy as np
import jax
import jax.numpy as jnp
from jax import lax
from jax.experimental import pallas as pl
from jax.experimental.pallas import tpu as pltpu

F32 = jnp.float32
BF16 = jnp.bfloat16

DEPTH = 2
HEAD_DIM = 128
ATTN_HEADS = 16
KV_GROUPS = 2
HEADS_PER_GROUP = ATTN_HEADS // KV_GROUPS
ATTN_WIDTH = ATTN_HEADS * HEAD_DIM
KV_WIDTH = KV_GROUPS * HEAD_DIM
ROPE_DIM = HEAD_DIM // 4
ROPE_THETA = 500000.0
CMP_BLOCK = 32
CMP_STRIDE = 16
SLC_BLOCK = 32
SLC_TOPN = 8
WINDOW = 512
HG_HEADS = 8
HG_DK = 128
HG_DV = 128
HG_KWIDTH = HG_HEADS * HG_DK
HG_WIDTH = HG_HEADS * HG_DV
POOL_WINDOWS = (2, 4, 8, 16)
POOL_GROUP = 256
POOL_WIDTH = POOL_GROUP * len(POOL_WINDOWS)
N_EXPERTS = 64
TOP_K = 8
D_EXPERT = 128
D_SHARED = 256
ROUTE_SCALE = 2.5
DN_ALPHA = (2.0 * DEPTH) ** 0.25
LN_EPS = 1e-5
RMS_EPS = 1e-6
NEG = -1e30

LANES = 128
SUBLANES = 8
BF16_SUBLANES = 16
VMEM_LIMIT = 56 * 1024 * 1024

COL_KV = ATTN_WIDTH
ATTN_COLS = ATTN_WIDTH + 6 * KV_WIDTH
COL_HG = 0
COL_POOL = COL_HG + 4 * HG_WIDTH
COL_MG = COL_POOL + POOL_WIDTH

TQ = 256
KC = 512
WSPAN = WINDOW + TQ
HC = 128
HG_LEVELS = (64, 32, 16, 8, 4, 2, 1)
HG_STEP_HEADS = 8


def _cparams(*sem):
    return pltpu.CompilerParams(dimension_semantics=sem, vmem_limit_bytes=VMEM_LIMIT)


def _ln_kernel(alpha, has_res, tail, *refs):
    n_in = 2 if has_res else 1
    g_ref, b_ref = refs[n_in], refs[n_in + 1]
    rest = refs[n_in + 2:]
    x = refs[0][...].astype(F32)
    if has_res:
        x = alpha * refs[1][...].astype(F32) + x
    mu = jnp.mean(x, axis=-1, keepdims=True)
    xc = x - mu
    var = jnp.mean(xc * xc, axis=-1, keepdims=True)
    y = xc * lax.rsqrt(var + LN_EPS) * g_ref[...] + b_ref[...]
    if tail is None:
        (o_ref,) = rest
        o_ref[...] = y.astype(o_ref.dtype)
        return
    h = y.astype(BF16)
    if tail == "proj":
        w_ref, o_ref, p_ref = rest
        p_ref[...] = lax.dot_general(h, w_ref[...], (((1,), (1,)), ((), ())),
                                     preferred_element_type=F32).astype(p_ref.dtype)
    else:
        w_ref, rb_ref, s1_ref, s3_ref, o_ref, gate_ref, hs_ref = rest
        gate_ref[...] = _route(w_ref[...], rb_ref[...], h)
        h1 = jnp.dot(h, s1_ref[...], preferred_element_type=F32)
        h3 = jnp.dot(h, s3_ref[...], preferred_element_type=F32)
        hs_ref[...] = (h1 * jax.nn.sigmoid(h1) * h3).astype(BF16)
    o_ref[...] = h


def layer_norm_rows(x, g, b, res=None, alpha=1.0, out_dtype=BF16, tm=512, proj=None, router=None):
    n, d = x.shape
    row = pl.BlockSpec((tm, d), lambda i: (i, 0))
    vec = pl.BlockSpec((1, d), lambda i: (0, 0))
    ins = [x] + ([res] if res is not None else []) + [g.reshape(1, d), b.reshape(1, d)]
    specs = [row] + ([row] if res is not None else []) + [vec, vec]
    out_specs, out_shape, tail = [row], [jax.ShapeDtypeStruct((n, d), out_dtype)], None
    if proj is not None:
        wt, layer = proj
        m = wt.shape[1]
        tail = "proj"
        ins.append(wt)
        specs.append(pl.BlockSpec((None, m, d), lambda i: (layer, 0, 0)))
        out_specs.append(pl.BlockSpec((tm, m), lambda i: (i, 0)))
        out_shape.append(jax.ShapeDtypeStruct((n, m), BF16))
    elif router is not None:
        router_w, router_b, ws1, ws3, layer = router
        tail = "router"
        ins += [jnp.zeros((LANES, d), BF16).at[:N_EXPERTS].set(router_w.T.astype(BF16)),
                jnp.zeros((LANES, 1), F32).at[:N_EXPERTS, 0].set(router_b.astype(F32)), ws1, ws3]
        shared_w = pl.BlockSpec((None, d, D_SHARED), lambda i: (layer, 0, 0))
        specs += [pl.BlockSpec((LANES, d), lambda i: (0, 0)), pl.BlockSpec((LANES, 1), lambda i: (0, 0)),
                  shared_w, shared_w]
        out_specs += [pl.BlockSpec((tm, LANES), lambda i: (i, 0)), pl.BlockSpec((tm, D_SHARED), lambda i: (i, 0))]
        out_shape += [jax.ShapeDtypeStruct((n, LANES), F32), jax.ShapeDtypeStruct((n, D_SHARED), BF16)]
    assert tail is None or out_dtype == BF16
    outs = pl.pallas_call(
        functools.partial(_ln_kernel, alpha, res is not None, tail),
        grid=(n // tm,),
        in_specs=specs,
        out_specs=out_specs,
        out_shape=out_shape,
        compiler_params=_cparams("parallel"),
        name="layer_norm",
    )(*ins)
    return outs[0] if tail is None else tuple(outs)


def _mm_kernel(x_ref, w_ref, o_ref):
    o_ref[...] = jnp.dot(x_ref[...], w_ref[...], preferred_element_type=F32).astype(o_ref.dtype)


def matmul(x, w, out_dtype, layer, tm=1024, tn=1024):
    n, k = x.shape
    m = w.shape[-1]
    tm, tn = min(tm, n), min(tn, m)
    assert m % tn == 0
    return pl.pallas_call(
        _mm_kernel,
        grid=(n // tm, m // tn),
        in_specs=[pl.BlockSpec((tm, k), lambda i, j: (i, 0)),
                  pl.BlockSpec((None, k, tn), lambda i, j: (layer, 0, j))],
        out_specs=pl.BlockSpec((tm, tn), lambda i, j: (i, j)),
        out_shape=jax.ShapeDtypeStruct((n, m), out_dtype),
        compiler_params=_cparams("parallel", "parallel"),
        name="matmul",
    )(x, w)


def _mm_nt_kernel(x_ref, w_ref, o_ref):
    o_ref[...] = lax.dot_general(x_ref[...], w_ref[0], (((1,), (1,)), ((), ())),
                                 preferred_element_type=F32).astype(o_ref.dtype)


def matmul_nt(x, wt, out_dtype, layer, tm=1024, tn=1024, row0=0, rows=None):
    n, k = x.shape
    m = rows if rows is not None else wt.shape[1] - row0
    tm, tn = min(tm, n), min(tn, m)
    assert m % tn == 0 and row0 % BF16_SUBLANES == 0
    return pl.pallas_call(
        _mm_nt_kernel,
        grid=(n // tm, m // tn),
        in_specs=[pl.BlockSpec((tm, k), lambda i, j: (i, 0)),
                  pl.BlockSpec((pl.Element(1), pl.Element(tn), pl.Element(k)),
                               lambda i, j: (layer, pl.multiple_of(row0 + j * tn, BF16_SUBLANES), 0))],
        out_specs=pl.BlockSpec((tm, tn), lambda i, j: (i, j)),
        out_shape=jax.ShapeDtypeStruct((n, m), out_dtype),
        compiler_params=_cparams("parallel", "parallel"),
        name="matmul_nt",
    )(x, wt)


def _rope_table_kernel(pos_ref, inv_ref, c_ref, sa_ref, sb_ref):
    ang = pos_ref[...].astype(F32) * inv_ref[...]
    lane = lax.broadcasted_iota(jnp.int32, ang.shape, 1)
    sn = jnp.sin(ang)
    c_ref[...] = jnp.cos(ang)
    sa_ref[...] = jnp.where(lane < ROPE_DIM // 2, -sn, 0.0)
    sb_ref[...] = jnp.where((lane >= ROPE_DIM // 2) & (lane < ROPE_DIM), sn, 0.0)


def rope_tables(positions, tm=1024):
    n = positions.size
    half = ROPE_DIM // 2
    inv = ROPE_THETA ** (-np.arange(half, dtype=np.float32) * 2.0 / ROPE_DIM)
    inv_full = np.zeros((1, LANES), np.float32)
    inv_full[0, :half] = inv
    inv_full[0, half:ROPE_DIM] = inv
    tm = min(tm, n)
    out = jax.ShapeDtypeStruct((n, LANES), F32)
    spec = pl.BlockSpec((tm, LANES), lambda i: (i, 0))
    return pl.pallas_call(
        _rope_table_kernel,
        grid=(n // tm,),
        in_specs=[pl.BlockSpec((tm, 1), lambda i: (i, 0)), pl.BlockSpec((1, LANES), lambda i: (0, 0))],
        out_specs=[spec, spec, spec],
        out_shape=[out, out, out],
        compiler_params=_cparams("parallel"),
        name="rope_tables",
    )(positions.reshape(n, 1), jnp.asarray(inv_full))


def _rope(x, c, sa, sb):
    return x * c + pltpu.roll(x, LANES - ROPE_DIM // 2, 1) * sa + pltpu.roll(x, ROPE_DIM // 2, 1) * sb


def _gelu_tanh(x):
    return 0.5 * x * (1.0 + jnp.tanh(0.7978845608028654 * (x + 0.044715 * x * x * x)))


def _top_mask(val, lane, n_pick, axis=-1):
    sel = jnp.zeros(val.shape, F32)
    for _ in range(n_pick):
        m = jnp.max(val, axis=axis, keepdims=True)
        idx = jnp.min(jnp.where(val == m, lane, LANES), axis=axis, keepdims=True)
        pick = lane == idx
        sel = jnp.where(pick, 1.0, sel)
        val = jnp.where(pick, -jnp.inf, val)
    return sel


def _attn_kernel(q_ref, kc_ref, vc_ref, ks_ref, vs_ref, kw_ref, vw_ref, gate_ref,
                 cq_ref, saq_ref, sbq_ref, ck_ref, sak_ref, sbk_ref,
                 posk_ref, posv_ref, w1k_ref, w2k_ref, w1v_ref, w2v_ref, ov_ref, nege_ref,
                 o_ref, kcmp_s, vcmp_s, kx_s, kwr_s, vsx_s, vwx_s, tmp_s, sca_s, scb_s, m_s, acc_s, sw_s, wbias_s):
    seq = ks_ref.shape[0]
    nh = seq // CMP_STRIDE
    n_cmp = (seq - CMP_BLOCK) // CMP_STRIDE + 1
    hpg = HEADS_PER_GROUP
    scale = HEAD_DIM ** -0.5
    scale2 = scale * 1.4426950408889634
    qt = pl.program_id(2)
    nt = (((1,), (1,)), ((), ()))

    @pl.when(qt == 0)
    def _per_sequence():
        def compress(t_ref, pos_ref, w1_ref, w2_ref, out_s):
            tmp_s[...] = t_ref[...].astype(F32)
            first = jnp.zeros((nh, HEAD_DIM), F32)
            second = jnp.zeros((nh, HEAD_DIM), F32)
            for j in range(CMP_BLOCK):
                x = tmp_s[pl.ds(j % CMP_STRIDE, nh, stride=CMP_STRIDE), :] + pos_ref[j:j + 1, :]
                p = jnp.dot(x.astype(BF16), w1_ref[j], preferred_element_type=F32)
                if j < CMP_STRIDE:
                    first = first + p
                else:
                    second = second + p
            pre = first + pltpu.roll(second, nh - 1, 0)
            hid = _gelu_tanh(pre).astype(BF16)
            out_s[...] = jnp.dot(hid, w2_ref[...], preferred_element_type=F32).astype(BF16)

        compress(kc_ref, posk_ref, w1k_ref, w2k_ref, kcmp_s)
        compress(vc_ref, posv_ref, w1v_ref, w2v_ref, vcmp_s)
        ck, sak, sbk = ck_ref[...], sak_ref[...], sbk_ref[...]
        kx_s[:, :HEAD_DIM] = _rope(ks_ref[...].astype(F32), ck, sak, sbk).astype(BF16)
        kx_s[:, HEAD_DIM:] = nege_ref[...]
        ones = jnp.ones((seq, HEAD_DIM), BF16)
        vsx_s[:, :HEAD_DIM] = vs_ref[...]
        vsx_s[:, HEAD_DIM:] = ones
        vwx_s[:, :HEAD_DIM] = vw_ref[...]
        vwx_s[:, HEAD_DIM:] = ones
        kwr_s[...] = _rope(kw_ref[...].astype(F32), ck, sak, sbk).astype(BF16)

    t0 = qt * TQ
    t = t0 + lax.broadcasted_iota(jnp.int32, (TQ, 1), 0)
    lane = lax.broadcasted_iota(jnp.int32, (TQ, LANES), 1)
    q = q_ref[...]
    qf = q.astype(F32)
    cq, saq, sbq = cq_ref[...], saq_ref[...], sbq_ref[...]
    heads = [slice(h * HEAD_DIM, (h + 1) * HEAD_DIM) for h in range(hpg)]
    q_raw = jnp.concatenate([q[:, s] for s in heads], axis=0)
    q_rot = jnp.concatenate([_rope(qf[:, s], cq, saq, sbq).astype(BF16) for s in heads], axis=0)

    s = lax.dot_general(q_raw, kcmp_s[...], nt, preferred_element_type=F32) * scale
    s3 = s.reshape(hpg, TQ, nh)
    vis = (lane * CMP_STRIDE + (CMP_BLOCK - 1) <= t) & (lane < n_cmp)
    s3 = jnp.where(vis[None], s3, NEG)
    e3 = jnp.exp(s3 - jnp.max(s3, axis=-1, keepdims=True))
    inv = 1.0 / jnp.sum(e3, axis=-1, keepdims=True)
    pb = jnp.where(vis[None], e3 * inv, 0.0).astype(BF16)
    o_c = jnp.dot(pb.reshape(hpg * TQ, nh), vcmp_s[...], preferred_element_type=F32).reshape(hpg, TQ, HEAD_DIM)

    psum = jnp.sum(pb.astype(F32), axis=0)
    imp = lax.dot_general(ov_ref[...], psum, nt, preferred_element_type=F32, precision=lax.Precision.HIGHEST)
    n_slc = imp.shape[0]
    blk = lax.broadcasted_iota(jnp.int32, (n_slc, TQ), 0)
    blk_t = (t0 + lax.broadcasted_iota(jnp.int32, (n_slc, TQ), 1)) // SLC_BLOCK
    causal = blk <= blk_t
    forced = (blk == 0) | (blk == blk_t) | (blk == blk_t - 1)
    val = jnp.where(forced, jnp.inf, jnp.where(causal, imp, -jnp.inf))
    not_sel = jnp.where(causal, 1.0 - _top_mask(val, blk, SLC_TOPN, axis=0), 1.0)
    not_sel = jnp.concatenate([not_sel.T, jnp.zeros((TQ, HEAD_DIM - n_slc), F32)], axis=1).astype(BF16)
    q_ext = jnp.concatenate([q_rot, jnp.concatenate([not_sel] * hpg, axis=0)], axis=1)

    w0 = pl.multiple_of(jnp.maximum(t0 - WINDOW, 0), TQ)
    kk = kwr_s[pl.ds(w0, WSPAN), :]
    vv = vwx_s[pl.ds(w0, WSPAN), :]
    sw_s[...] = lax.dot_general(q_rot, kk, nt, preferred_element_type=F32)
    kpos = w0 + lax.broadcasted_iota(jnp.int32, (TQ, WSPAN), 1)
    wbias_s[...] = jnp.where((kpos <= t) & (t - kpos < WINDOW), 0.0, NEG)
    m_w = jnp.max(sw_s[...].reshape(hpg, TQ, WSPAN) + wbias_s[...][None], axis=-1, keepdims=True)
    pw = jnp.exp2((sw_s[...].reshape(hpg, TQ, WSPAN) + wbias_s[...][None] - m_w) * scale2)
    o_w = jnp.dot(pw.astype(BF16).reshape(hpg * TQ, WSPAN), vv, preferred_element_type=F32)
    o_w = (o_w[:, :HEAD_DIM] / o_w[:, HEAD_DIM:]).reshape(hpg, TQ, HEAD_DIM)

    g = jax.nn.sigmoid(gate_ref[...].astype(F32))
    o_cw = [g[:, h:h + 1] * o_c[h] + g[:, 2 * hpg + h:2 * hpg + h + 1] * o_w[h] for h in range(hpg)]

    n_chunks = (t0 + TQ + KC - 1) // KC
    last = n_chunks - 1
    tri_bias = jnp.where(lax.broadcasted_iota(jnp.int32, (TQ, TQ), 1) <= lax.broadcasted_iota(jnp.int32, (TQ, TQ), 0),
                         0.0, NEG)

    def scores(c):
        kk = kx_s[pl.ds(pl.multiple_of(c * KC, KC), KC), :]
        return lax.dot_general(q_ext, kk, nt, preferred_element_type=F32)

    def softmax_pv(sc_ref, c, first):
        vv = vsx_s[pl.ds(pl.multiple_of(c * KC, KC), KC), :]
        m_new = jnp.max(sc_ref[...].reshape(hpg, TQ, KC), axis=-1, keepdims=True)
        if not first:
            m_old = m_s[...].reshape(hpg, TQ, 1)
            m_new = jnp.maximum(m_old, m_new)
        p = jnp.exp2((sc_ref[...].reshape(hpg, TQ, KC) - m_new) * scale2)
        pv = jnp.dot(p.astype(BF16).reshape(hpg * TQ, KC), vv, preferred_element_type=F32)
        if first:
            acc_s[...] = pv
        else:
            a = jnp.exp2((m_old - m_new) * scale2).reshape(hpg * TQ, 1)
            acc_s[...] = a * acc_s[...] + pv
        m_s[...] = m_new.reshape(hpg * TQ, 1)

    sca_s[...] = scores(last)
    own = pl.ds(pl.multiple_of(t0 - last * KC, TQ), TQ)
    for h in range(hpg):
        sca_s[h * TQ:(h + 1) * TQ, own] += tri_bias
    scb_s[...] = scores(0)
    softmax_pv(sca_s, last, True)

    def chunk_pair(i, carry):
        c0 = 2 * i
        sca_s[...] = scores(jnp.minimum(c0 + 1, last))
        softmax_pv(scb_s, c0, False)

        @pl.when(c0 + 1 < last)
        def _():
            scb_s[...] = scores(jnp.minimum(c0 + 2, last))
            softmax_pv(sca_s, c0 + 1, False)

        return carry

    lax.fori_loop(0, (last + 1) // 2, chunk_pair, 0)
    acc = acc_s[...]
    o_s = (acc[:, :HEAD_DIM] / acc[:, HEAD_DIM:]).reshape(hpg, TQ, HEAD_DIM)
    for h in range(hpg):
        o_ref[:, heads[h]] = (o_cw[h] + g[:, hpg + h:hpg + h + 1] * o_s[h]).astype(BF16)


def _overlap_matrix(n_half, n_cmp, n_slc):
    c = np.arange(n_half)[None, :] * CMP_STRIDE
    s = np.arange(n_slc)[:, None] * SLC_BLOCK
    ov = np.clip(np.minimum(c + CMP_BLOCK, s + SLC_BLOCK) - np.maximum(c, s), 0, None) / CMP_STRIDE
    ov[:, n_cmp:] = 0.0
    return ov.astype(np.float32)


def nsa_attention(proj, proj_g, tabs, cmp_params, batch, seq):
    n = batch * seq
    nq = seq // TQ
    nh = seq // CMP_STRIDE
    n_cmp = (seq - CMP_BLOCK) // CMP_STRIDE + 1
    n_slc = seq // SLC_BLOCK
    assert nh == LANES and n_slc <= LANES and seq % KC == 0 and seq >= WSPAN
    c_tab, sa_tab, sb_tab = tabs
    posk, posv, w1k, w2k, w1v, w2v = cmp_params
    ov = jnp.asarray(_overlap_matrix(nh, n_cmp, n_slc))
    neg_onehot = np.where(np.arange(seq)[:, None] // SLC_BLOCK == np.arange(HEAD_DIM)[None, :], NEG, 0.0)
    expand = jnp.asarray(neg_onehot.astype(np.float32), dtype=BF16)

    gw = HEADS_PER_GROUP * HEAD_DIM
    qspec = pl.BlockSpec((TQ, gw), lambda b, g, i: (b * nq + i, g))

    def kvspec(slab):
        return pl.BlockSpec((seq, HEAD_DIM), lambda b, g, i: (b, COL_KV // HEAD_DIM + slab * KV_GROUPS + g))

    gspec = pl.BlockSpec((TQ, LANES), lambda b, g, i: (b * nq + i, g))
    tq_spec = pl.BlockSpec((TQ, LANES), lambda b, g, i: (b * nq + i, 0))
    tk_spec = pl.BlockSpec((seq, LANES), lambda b, g, i: (b, 0))

    def full(a):
        return pl.BlockSpec(a.shape, lambda b, g, i: (0,) * a.ndim)

    consts = [posk, posv, w1k, w2k, w1v, w2v, ov, expand]
    return pl.pallas_call(
        _attn_kernel,
        grid=(batch, KV_GROUPS, nq),
        in_specs=[qspec] + [kvspec(s) for s in range(6)] + [gspec] + [tq_spec] * 3 + [tk_spec] * 3
                 + [full(a) for a in consts],
        out_specs=pl.BlockSpec((TQ, gw), lambda b, g, i: (b * nq + i, g)),
        out_shape=jax.ShapeDtypeStruct((n, ATTN_WIDTH), BF16),
        scratch_shapes=[pltpu.VMEM((nh, HEAD_DIM), BF16), pltpu.VMEM((nh, HEAD_DIM), BF16),
                        pltpu.VMEM((seq, 2 * HEAD_DIM), BF16), pltpu.VMEM((seq, HEAD_DIM), BF16),
                        pltpu.VMEM((seq, 2 * HEAD_DIM), BF16), pltpu.VMEM((seq, 2 * HEAD_DIM), BF16),
                        pltpu.VMEM((seq, HEAD_DIM), F32),
                        pltpu.VMEM((HEADS_PER_GROUP * TQ, KC), F32), pltpu.VMEM((HEADS_PER_GROUP * TQ, KC), F32),
                        pltpu.VMEM((HEADS_PER_GROUP * TQ, 1), F32), pltpu.VMEM((HEADS_PER_GROUP * TQ, 2 * HEAD_DIM), F32),
                        pltpu.VMEM((HEADS_PER_GROUP * TQ, WSPAN), F32), pltpu.VMEM((TQ, WSPAN), F32)],
        compiler_params=_cparams("parallel", "parallel", "arbitrary"),
        name="nsa_attention",
    )(proj, proj, proj, proj, proj, proj, proj, proj_g, c_tab, sa_tab, sb_tab, c_tab, sa_tab, sb_tab, *consts)


def _hgrn_level_tables():
    t = np.arange(HC)[:, None]
    s = np.arange(HC)[None, :]
    lv = np.full((HC, HC), -1, np.int32)
    for i, m in enumerate(HG_LEVELS):
        ok = ((t // m) % 2 == 1) & (s // m == t // m - 1)
        lv[ok] = i
    lv[t == s] = len(HG_LEVELS)
    tril = (s <= t).astype(np.float32)
    return lv, tril


def _hgrn_kernel(q_ref, f_ref, i_ref, g_ref, lb_ref, ng_ref, lv_ref, tril_ref, o_ref, st_s, b_s):
    @pl.when(pl.program_id(2) == 0)
    def _():
        st_s[...] = jnp.zeros_like(st_s)

    nt = (((1,), (1,)), ((), ()))
    tn = (((0,), (0,)), ((), ()))
    heads = range(HG_STEP_HEADS)
    cols = [slice(h * HG_DK, (h + 1) * HG_DK) for h in heads]
    lv, tril = lv_ref[...], tril_ref[...]
    lb, ng = lb_ref[0], ng_ref[0]
    row = lax.broadcasted_iota(jnp.int32, (HC, HG_DK), 0)

    f = [lb[:, s] + (1.0 - lb[:, s]) * jax.nn.sigmoid(f_ref[:, s].astype(F32)) for s in cols]
    logf = [jnp.log(x) for x in f]
    kk = [1.0 - x for x in f]
    q = [q_ref[:, s].astype(F32) for s in cols]
    vb = [i_ref[:, s] for s in cols]
    b = [jnp.dot(tril, x, preferred_element_type=F32, precision=lax.Precision.HIGHEST) for x in logf]
    for h in heads:
        b_s[h] = b[h]

    st = [st_s[h] for h in heads]
    o = [lax.dot_general((q[h] * jnp.exp(b[h])).astype(BF16), st[h].astype(BF16), nt, preferred_element_type=F32)
         for h in heads]

    nb = HC // SUBLANES
    r3 = lax.broadcasted_iota(jnp.int32, (nb, SUBLANES, HG_DK), 1)
    b3 = [x.reshape(nb, SUBLANES, HG_DK) for x in b]
    a = [lax.dot_general(q[h].astype(BF16), kk[h].astype(BF16), nt, preferred_element_type=F32) for h in heads]
    a = [jnp.where(lv == len(HG_LEVELS), x, 0.0) for x in a]
    for i, m in enumerate(HG_LEVELS):
        odd = (row // m) % 2 == 1
        for h in heads:
            if m >= SUBLANES:
                ref_rows = [jnp.broadcast_to(b_s[h, pl.ds((2 * j + 1) * m - 1, 1), :], (2 * m, HG_DK))
                            for j in range(HC // (2 * m))]
                ref = jnp.concatenate(ref_rows, axis=0)
            else:
                ref = jnp.broadcast_to(b3[h][:, m - 1:m, :], b3[h].shape)
                for j in range(1, SUBLANES // (2 * m)):
                    r = (2 * j + 1) * m - 1
                    ref = jnp.where(r3 >= 2 * j * m, jnp.broadcast_to(b3[h][:, r:r + 1, :], b3[h].shape), ref)
                ref = ref.reshape(HC, HG_DK)
            x = (jnp.where(odd, q[h], kk[h]) * jnp.exp(-jnp.abs(b[h] - ref))).astype(BF16)
            am = lax.dot_general(x, x, nt, preferred_element_type=F32)
            a[h] = jnp.where(lv == i, am, a[h])
    o = [o[h] + jnp.dot(a[h].astype(BF16), vb[h], preferred_element_type=F32) for h in heads]

    for h in heads:
        b_last = b_s[h, pl.ds(HC - 1, 1), :]
        kd = (kk[h] * jnp.exp(b_last - b[h])).astype(BF16)
        st_s[h] = st[h] * jnp.exp(b_last) + lax.dot_general(vb[h], kd, tn, preferred_element_type=F32)

        oh = o[h] * lax.rsqrt(jnp.mean(o[h] * o[h], axis=-1, keepdims=True) + RMS_EPS) * ng[:, cols[h]]
        gg = g_ref[:, cols[h]].astype(F32)
        o_ref[:, cols[h]] = (oh * (gg * jax.nn.sigmoid(gg))).astype(BF16)


def hgrn2(proj, lb, norm_g, batch, seq):
    n = batch * seq
    nc = seq // HC
    hs = HG_STEP_HEADS
    wide = hs * HG_DK
    lv, tril = _hgrn_level_tables()
    base = COL_HG // wide

    def slab(k):
        return pl.BlockSpec((HC, wide), lambda b, h, c: (b * nc + c, base + k * (HG_HEADS // hs) + h))

    vec = pl.BlockSpec((1, 1, wide), lambda b, h, c: (h, 0, 0))
    const = pl.BlockSpec((HC, HC), lambda b, h, c: (0, 0))
    return pl.pallas_call(
        _hgrn_kernel,
        grid=(batch, HG_HEADS // hs, nc),
        in_specs=[slab(0), slab(1), slab(2), slab(3), vec, vec, const, const],
        out_specs=pl.BlockSpec((HC, wide), lambda b, h, c: (b * nc + c, h)),
        out_shape=jax.ShapeDtypeStruct((n, HG_WIDTH), BF16),
        scratch_shapes=[pltpu.VMEM((hs, HG_DV, HG_DK), F32), pltpu.VMEM((hs, HC, HG_DK), F32)],
        compiler_params=_cparams("parallel", "parallel", "arbitrary"),
        name="hgrn2",
    )(proj, proj, proj, proj, lb.reshape(HG_HEADS // hs, 1, wide), norm_g.reshape(HG_HEADS // hs, 1, wide),
      jnp.asarray(lv), jnp.asarray(tril))


def _pool_kernel(p_ref, w_ref, sc_ref, o_ref):
    gi = pl.program_id(1)
    x = p_ref[...].astype(F32)
    t = lax.broadcasted_iota(jnp.int32, x.shape, 0)
    acc = x
    for k in range(len(POOL_WINDOWS)):
        sh = 1 << k
        nxt = acc + jnp.where(t >= sh, pltpu.roll(acc, sh, 0), 0.0)
        acc = jnp.where(k <= gi, nxt, acc)
    width = lax.shift_left(jnp.int32(2), gi)
    cnt = jnp.minimum(t + 1, width).astype(F32)
    mixed = acc / cnt - x
    y = jnp.dot(mixed.astype(BF16), w_ref[0], preferred_element_type=F32) * sc_ref[0]
    o_ref[...] = y.astype(BF16)


def multiscale_pool(proj, w_pool, scale, batch, seq):
    n = batch * seq
    ng = len(POOL_WINDOWS)
    base = COL_POOL // POOL_GROUP
    return pl.pallas_call(
        _pool_kernel,
        grid=(batch, ng),
        in_specs=[pl.BlockSpec((seq, POOL_GROUP), lambda b, g: (b, base + g)),
                  pl.BlockSpec((1, POOL_GROUP, POOL_GROUP), lambda b, g: (g, 0, 0)),
                  pl.BlockSpec((1, 1, POOL_GROUP), lambda b, g: (g, 0, 0))],
        out_specs=pl.BlockSpec((seq, POOL_GROUP), lambda b, g: (b, g)),
        out_shape=jax.ShapeDtypeStruct((n, POOL_WIDTH), BF16),
        compiler_params=_cparams("parallel", "parallel"),
        name="multiscale_pool",
    )(proj, w_pool, scale.reshape(ng, 1, POOL_GROUP))


def _merge_kernel(ya_ref, yb_ref, yc_ref, wa_ref, wb_ref, wc_ref, ga_ref, gb_ref, gc_ref, o_ref):
    def branch(y_ref, w_ref, g_ref):
        up = jnp.dot(y_ref[...], w_ref[...], preferred_element_type=F32)
        return (0.5 + 0.5 * jnp.tanh(0.5 * g_ref[...].astype(F32))) * up

    o_ref[...] = (branch(ya_ref, wa_ref, ga_ref) + branch(yb_ref, wb_ref, gb_ref)
                  + branch(yc_ref, wc_ref, gc_ref)).astype(o_ref.dtype)


def merge_branches(ya, yb, yc, wa, wb, wc, proj, d_model, layer, tm=1024, tn=512):
    n = ya.shape[0]
    tm = min(tm, n)
    gbase = COL_MG // tn
    step = d_model // tn

    def y_spec(a):
        return pl.BlockSpec((tm, a.shape[1]), lambda i, j: (i, 0))

    def w_spec(a):
        return pl.BlockSpec((None, a.shape[1], tn), lambda i, j: (layer, 0, j))

    def g_spec(k):
        return pl.BlockSpec((tm, tn), lambda i, j: (i, gbase + k * step + j))

    return pl.pallas_call(
        _merge_kernel,
        grid=(n // tm, d_model // tn),
        in_specs=[y_spec(ya), y_spec(yb), y_spec(yc), w_spec(wa), w_spec(wb), w_spec(wc),
                  g_spec(0), g_spec(1), g_spec(2)],
        out_specs=pl.BlockSpec((tm, tn), lambda i, j: (i, j)),
        out_shape=jax.ShapeDtypeStruct((n, d_model), BF16),
        compiler_params=_cparams("parallel", "parallel"),
        name="merge_branches",
    )(ya, yb, yc, wa, wb, wc, proj, proj, proj)


def _route(w, bias, h):
    logits = lax.dot_general(w, h, (((1,), (1,)), ((), ())), preferred_element_type=F32)
    scores = jax.nn.sigmoid(logits)
    row = lax.broadcasted_iota(jnp.int32, scores.shape, 0)
    real = row < N_EXPERTS
    sel = _top_mask(jnp.where(real, scores + bias, -jnp.inf), row, TOP_K, axis=0)
    g = jnp.where(real, sel * scores, 0.0)
    g = g / jnp.sum(g, axis=0, keepdims=True) * ROUTE_SCALE
    return g.T


def _moe_up_kernel(ec, x_ref, w1_ref, w3_ref, gate_ref, o_ref):
    x = x_ref[...]
    w1 = jnp.concatenate([w1_ref[e] for e in range(ec)], axis=1)
    w3 = jnp.concatenate([w3_ref[e] for e in range(ec)], axis=1)
    h1 = jnp.dot(x, w1, preferred_element_type=F32)
    h3 = jnp.dot(x, w3, preferred_element_type=F32)
    hd = h1 * jax.nn.sigmoid(h1) * h3
    g = gate_ref[...]
    lane = lax.broadcasted_iota(jnp.int32, g.shape, 1)
    first = pl.program_id(1) * ec
    for e in range(ec):
        col = jnp.sum(jnp.where(lane == first + e, g, 0.0), axis=-1, keepdims=True)
        cols = slice(e * D_EXPERT, (e + 1) * D_EXPERT)
        o_ref[:, cols] = (hd[:, cols] * col).astype(BF16)


def moe_hidden(x, gate, w1, w3, layer, tm=1024, ec=4):
    n, d = x.shape
    tm = min(tm, n)
    th = ec * D_EXPERT
    w_spec = pl.BlockSpec((None, ec, d, D_EXPERT), lambda i, j: (layer, j, 0, 0))
    return pl.pallas_call(
        functools.partial(_moe_up_kernel, ec),
        grid=(n // tm, N_EXPERTS // ec),
        in_specs=[pl.BlockSpec((tm, d), lambda i, j: (i, 0)), w_spec, w_spec,
                  pl.BlockSpec((tm, LANES), lambda i, j: (i, 0))],
        out_specs=pl.BlockSpec((tm, th), lambda i, j: (i, j)),
        out_shape=jax.ShapeDtypeStruct((n, N_EXPERTS * D_EXPERT), BF16),
        compiler_params=_cparams("parallel", "parallel"),
        name="moe_hidden",
    )(x, w1, w3, gate)


def _moe_down_kernel(hr_ref, hs_ref, w2_ref, ws2_ref, o_ref):
    y = jnp.dot(hr_ref[...], w2_ref[...], preferred_element_type=F32)
    y = y + jnp.dot(hs_ref[...], ws2_ref[...], preferred_element_type=F32)
    o_ref[...] = y.astype(o_ref.dtype)


def moe_down(hid_r, hid_s, w2, ws2, layer, tm=1024, tn=512):
    n, kr = hid_r.shape
    ks = hid_s.shape[1]
    d = w2.shape[-1]
    tm = min(tm, n)
    return pl.pallas_call(
        _moe_down_kernel,
        grid=(n // tm, d // tn),
        in_specs=[pl.BlockSpec((tm, kr), lambda i, j: (i, 0)),
                  pl.BlockSpec((tm, ks), lambda i, j: (i, 0)),
                  pl.BlockSpec((None, kr, tn), lambda i, j: (layer, 0, j)),
                  pl.BlockSpec((None, ks, tn), lambda i, j: (layer, 0, j))],
        out_specs=pl.BlockSpec((tm, tn), lambda i, j: (i, j)),
        out_shape=jax.ShapeDtypeStruct((n, d), BF16),
        compiler_params=_cparams("parallel", "parallel"),
        name="moe_down",
    )(hid_r, hid_s, w2, ws2)


def _branch_gate_weights(w_t):
    depth, _, d_model = w_t.shape
    ag = w_t[:, ATTN_COLS:ATTN_COLS + 3 * ATTN_HEADS, :].reshape(depth, 3, KV_GROUPS, HEADS_PER_GROUP, d_model)
    ag = jnp.moveaxis(ag, 2, 1).reshape(depth, KV_GROUPS, 3 * HEADS_PER_GROUP, d_model)
    ag = jnp.pad(ag, ((0, 0), (0, 0), (0, LANES - 3 * HEADS_PER_GROUP), (0, 0)))
    return ag.reshape(depth, KV_GROUPS * LANES, d_model)


def _hgrn_lower_bounds(logits):
    lbs = jnp.cumsum(jax.nn.softmax(logits.astype(F32), axis=0), axis=0)
    return lbs - lbs[0:1]


def kernel(x, positions, ln_in_g, ln_in_b, w_in, cmp_pos_k, cmp_pos_v, cmp_w1_k, cmp_w2_k, cmp_w1_v, cmp_w2_v,
           hg_lb_logits, hg_norm_g, pool_w, pool_scale, w_up_attn, w_up_hg, w_up_pool, w_o, ln1_g, ln1_b,
           router_w, router_b, w1, w3, w2, ws1, ws3, ws2, ln2_g, ln2_b):
    batch, seq, d = x.shape
    n = batch * seq
    depth = w_in.shape[0]
    lbs = _hgrn_lower_bounds(hg_lb_logits)
    tabs = rope_tables(positions)
    w_in_t = jnp.swapaxes(w_in, 1, 2).astype(BF16)
    w_gate_t = _branch_gate_weights(w_in_t)
    w1b, w3b = w1.astype(BF16), w3.astype(BF16)
    w2b = w2.astype(BF16).reshape(depth, N_EXPERTS * D_EXPERT, d)
    ws1b, ws3b, ws2b = ws1.astype(BF16), ws3.astype(BF16), ws2.astype(BF16)
    wab, whb, wpb, wob = w_up_attn.astype(BF16), w_up_hg.astype(BF16), w_up_pool.astype(BF16), w_o.astype(BF16)
    h, proj_g = layer_norm_rows(x.reshape(n, d), ln_in_g, ln_in_b, proj=(w_gate_t, 0))
    for l in range(depth):
        proj_a = matmul_nt(h, w_in_t, BF16, l, tm=512, tn=ATTN_COLS // 2, rows=ATTN_COLS)
        proj_b = matmul_nt(h, w_in_t, BF16, l, row0=ATTN_COLS + 3 * ATTN_HEADS)
        cmp_params = (cmp_pos_k[l], cmp_pos_v[l],
                      cmp_w1_k[l].reshape(CMP_BLOCK, HEAD_DIM, HEAD_DIM).astype(BF16), cmp_w2_k[l].astype(BF16),
                      cmp_w1_v[l].reshape(CMP_BLOCK, HEAD_DIM, HEAD_DIM).astype(BF16), cmp_w2_v[l].astype(BF16))
        ya = nsa_attention(proj_a, proj_g, tabs, cmp_params, batch, seq)
        yb = hgrn2(proj_b, lbs[l], hg_norm_g[l], batch, seq)
        yc = multiscale_pool(proj_b, pool_w[l].astype(BF16), pool_scale[l], batch, seq)
        merged = merge_branches(ya, yb, yc, wab, whb, wpb, proj_b, d, l)
        mix = matmul(merged, wob, BF16, l)
        h, gate, hid_s = layer_norm_rows(mix, ln1_g[l], ln1_b[l], res=h, alpha=DN_ALPHA,
                                         router=(router_w[l], router_b[l], ws1b, ws3b, l))
        hid_r = moe_hidden(h, gate, w1b, w3b, l)
        ffn = moe_down(hid_r, hid_s, w2b, ws2b, l)
        if l + 1 < depth:
            h, proj_g = layer_norm_rows(ffn, ln2_g[l], ln2_b[l], res=h, alpha=DN_ALPHA, proj=(w_gate_t, l + 1))
        else:
            h = layer_norm_rows(ffn, ln2_g[l], ln2_b[l], res=h, alpha=DN_ALPHA, out_dtype=F32)
    return h.reshape(batch, seq, d)
```
